```python
import math
import jax
import jax.numpy as jnp
from jax import lax
import numpy as np

D_MODEL = 1024
BATCH = 8
SEQ = 4096
DEPTH = 1
DEC_BATCH = 32
DEC_SEQ = 8
PAST_LEN = 16384
PAGE_SIZE = 128

HEAD_DIM = 64
N_FOX_HEADS = 8
N_DIFF_HEADS = 4
FOX_WIDTH = N_FOX_HEADS * HEAD_DIM
DIFF_WIDTH = N_DIFF_HEADS * 2 * HEAD_DIM
P_TOTAL = 3 * FOX_WIDTH + N_FOX_HEADS + 3 * DIFF_WIDTH + 2 * D_MODEL
N_BUCKETS = 32
MAX_DISTANCE = 128
Q_BLOCK = 128
N_EXPERTS = 32
TOP_K = 4
D_FF = D_MODEL
SWIGLU_LIMIT = 7.0
SWIGLU_ALPHA = 1.702
MOE_BLOCK = 128
RMS_EPS = 1e-6
ATTN_SCALE = HEAD_DIM ** -0.5

kernel_name = 'fox_diff_parallel_moe_decode_step'


def rms_norm(x, g):
    xf = x.astype(jnp.float32)
    y = xf * lax.rsqrt(jnp.mean(xf * xf, axis=-1, keepdims=True) + RMS_EPS)
    return (y * g.astype(jnp.float32)).astype(x.dtype)


def split_points():
    sizes = (FOX_WIDTH, FOX_WIDTH, FOX_WIDTH, N_FOX_HEADS,
             DIFF_WIDTH, DIFF_WIDTH, DIFF_WIDTH, D_MODEL, D_MODEL)
    pts, acc = [], 0
    for s in sizes[:-1]:
        acc += s
        pts.append(acc)
    return pts


def t5_bucket(q_pos, k_pos):
    n = jnp.maximum(q_pos[:, None] - k_pos[None, :], 0)
    max_exact = N_BUCKETS // 2
    nf = jnp.maximum(n, 1).astype(jnp.float32)
    large = max_exact + (jnp.log(nf / max_exact) / math.log(MAX_DISTANCE / max_exact)
                         * (N_BUCKETS - max_exact)).astype(jnp.int32)
    large = jnp.minimum(large, N_BUCKETS - 1)
    return jnp.where(n < max_exact, n, large)


def causal_mask(q_pos, k_pos):
    return k_pos[None, :] <= q_pos[:, None]


def fox_block(q, k, v, cq, ck, q_pos, k_pos):
    s = jnp.einsum('bqhd,bkhd->bhqk', q, k, preferred_element_type=jnp.float32) * ATTN_SCALE
    s = s + jnp.moveaxis(cq, 2, 1)[:, :, :, None] - jnp.moveaxis(ck, 2, 1)[:, :, None, :]
    s = jnp.where(causal_mask(q_pos, k_pos)[None, None], s, -jnp.inf)
    p = jax.nn.softmax(s, axis=-1)
    return jnp.einsum('bhqk,bkhd->bqhd', p.astype(v.dtype), v)


def diff_block(q, k, v, rel_table, lam, q_pos, k_pos):
    s = jnp.einsum('bqhcd,bkhcd->bchqk', q, k, preferred_element_type=jnp.float32) * ATTN_SCALE
    rb = jnp.moveaxis(rel_table[t5_bucket(q_pos, k_pos)].astype(jnp.float32), 2, 0)
    s = jnp.where(causal_mask(q_pos, k_pos)[None, None, None], s + rb[None, None], -jnp.inf)
    p = jax.nn.softmax(s, axis=-1)
    a = p[:, 0] - lam * p[:, 1]
    return jnp.einsum('bhqk,bkhe->bqhe', a.astype(v.dtype), v)


def causal_sweep(block_fn, n_q, offset):
    outs = []
    for s in range(0, n_q, Q_BLOCK):
        e = min(s + Q_BLOCK, n_q)
        outs.append(block_fn(s, e, offset + e))
    return jnp.concatenate(outs, axis=1)


def attention_sublayer(h, past, offset, w_in, b_forget, g_fq, g_fk, g_dq, g_dk,
                       lam_q1, lam_k1, lam_q2, lam_k2, g_do, w_fox_up, w_diff_up, w_out,
                       rel_table, lambda_init):
    B, T, _ = h.shape
    p = h @ w_in
    q_f, k_f, v_f, f_logit, q_d, k_d, v_d, gate_f, gate_d = jnp.split(p, split_points(), axis=-1)
    q_f = rms_norm(q_f.reshape(B, T, N_FOX_HEADS, HEAD_DIM), g_fq)
    k_f = rms_norm(k_f.reshape(B, T, N_FOX_HEADS, HEAD_DIM), g_fk)
    v_f = v_f.reshape(B, T, N_FOX_HEADS, HEAD_DIM)
    logf = jax.nn.log_sigmoid(f_logit.astype(jnp.float32) + b_forget.astype(jnp.float32))
    q_d = rms_norm(q_d.reshape(B, T, N_DIFF_HEADS, 2, HEAD_DIM), g_dq)
    k_d = rms_norm(k_d.reshape(B, T, N_DIFF_HEADS, 2, HEAD_DIM), g_dk)
    v_d = v_d.reshape(B, T, N_DIFF_HEADS, 2 * HEAD_DIM)
    c_new = jnp.cumsum(logf, axis=1)
    new_rows = (k_f, v_f, logf, k_d.reshape(B, T, N_DIFF_HEADS, 2 * HEAD_DIM), v_d)
    if past is None:
        kf_all, vf_all, ck_all, kd_all, vd_all = k_f, v_f, c_new, k_d, v_d
    else:
        pk_f, pv_f, plogf, pk_d, pv_d = past
        plogf = plogf.astype(jnp.float32)
        suffix = lax.cumsum(plogf, axis=1, reverse=True) - plogf
        kf_all = jnp.concatenate([pk_f, k_f], axis=1)
        vf_all = jnp.concatenate([pv_f, v_f], axis=1)
        ck_all = jnp.concatenate([-suffix, c_new], axis=1)
        kd_all = jnp.concatenate([pk_d.reshape(B, -1, N_DIFF_HEADS, 2, HEAD_DIM), k_d], axis=1)
        vd_all = jnp.concatenate([pv_d, v_d], axis=1)
    lam = (jnp.exp(jnp.sum(lam_q1.astype(jnp.float32) * lam_k1.astype(jnp.float32)))
           - jnp.exp(jnp.sum(lam_q2.astype(jnp.float32) * lam_k2.astype(jnp.float32))) + lambda_init)

    def fox_fn(s, e, kend):
        return fox_block(q_f[:, s:e], kf_all[:, :kend], vf_all[:, :kend], c_new[:, s:e],
                         ck_all[:, :kend], offset + jnp.arange(s, e), jnp.arange(kend))

    def diff_fn(s, e, kend):
        return diff_block(q_d[:, s:e], kd_all[:, :kend], vd_all[:, :kend], rel_table, lam,
                          offset + jnp.arange(s, e), jnp.arange(kend))

    o_f = causal_sweep(fox_fn, T, offset).reshape(B, T, FOX_WIDTH)
    o_d = causal_sweep(diff_fn, T, offset)
    o_d = (rms_norm(o_d, g_do) * (1.0 - lambda_init)).reshape(B, T, DIFF_WIDTH)
    merged = jax.nn.sigmoid(gate_f) * (o_f @ w_fox_up) + jax.nn.sigmoid(gate_d) * (o_d @ w_diff_up)
    return merged @ w_out, new_rows


def moe_ffn(x, w_router, b_router, w_gu, b_gu, w_dn, b_dn):
    shape = x.shape
    t = x.reshape(-1, D_MODEL)
    T = t.shape[0]
    logits = (t @ w_router).astype(jnp.float32) + b_router.astype(jnp.float32)
    top_val, top_idx = lax.top_k(logits, TOP_K)
    gates = jax.nn.softmax(top_val, axis=-1)
    A = T * TOP_K
    flat_e = top_idx.reshape(A)
    order = jnp.argsort(flat_e)
    sorted_e = flat_e[order]
    counts = jnp.bincount(flat_e, length=N_EXPERTS)
    padded = (counts + MOE_BLOCK - 1) // MOE_BLOCK * MOE_BLOCK
    start = jnp.cumsum(counts) - counts
    pend = jnp.cumsum(padded)
    pstart = pend - padded
    slot_sorted = (pstart[sorted_e] + jnp.arange(A) - start[sorted_e]).astype(jnp.int32)
    slot = jnp.zeros((A,), jnp.int32).at[order].set(slot_sorted)
    n_blocks = -(-A // MOE_BLOCK) + N_EXPERTS
    n_slots = n_blocks * MOE_BLOCK
    slot_token = jnp.full((n_slots,), T, jnp.int32).at[slot].set(
        (jnp.arange(A) // TOP_K).astype(jnp.int32))
    block_expert = jnp.minimum(
        jnp.searchsorted(pend, jnp.arange(n_blocks) * MOE_BLOCK, side='right'), N_EXPERTS - 1)
    t_pad = jnp.concatenate([t, jnp.zeros((1, D_MODEL), t.dtype)], axis=0)
    xb = t_pad[slot_token].reshape(n_blocks, MOE_BLOCK, D_MODEL)

    def expert_block(args):
        xe, e = args
        hgu = xe @ w_gu[e] + b_gu[e]
        gate = jnp.minimum(hgu[:, 0::2], SWIGLU_LIMIT)
        up = jnp.clip(hgu[:, 1::2], -SWIGLU_LIMIT, SWIGLU_LIMIT)
        glu = gate * jax.nn.sigmoid(SWIGLU_ALPHA * gate)
        return ((up + 1.0) * glu) @ w_dn[e] + b_dn[e]

    yb = lax.map(expert_block, (xb, block_expert)).reshape(n_slots, D_MODEL)
    y = jnp.sum(yb[slot].reshape(T, TOP_K, D_MODEL) * gates[..., None].astype(yb.dtype), axis=1)
    return y.reshape(shape)


def gather_pages(pool, page_table):
    g = jnp.take(pool, page_table, axis=0)
    return g.reshape((page_table.shape[0], page_table.shape[1] * pool.shape[1]) + pool.shape[2:])


def setup_inputs(seed: int = 0) -> dict:
    key = jax.random.key(seed)
    ks = jax.random.split(key, 32)
    f32 = jnp.float32

    def nrm(k, shape, scale):
        return scale * jax.random.normal(k, shape, f32)

    n_pages = PAST_LEN // PAGE_SIZE
    n_phys = (DEC_BATCH * n_pages * 5) // 4
    page_table = jax.random.permutation(ks[7], n_phys)[:DEC_BATCH * n_pages].reshape(
        DEC_BATCH, n_pages).astype(jnp.int32)
    return {
        'x_prompt': nrm(ks[0], (BATCH, SEQ, D_MODEL), 1.0),
        'x_sample': nrm(ks[1], (DEC_BATCH, DEC_SEQ, D_MODEL), 1.0),
        'cache_fox_k': nrm(ks[2], (DEPTH, n_phys, PAGE_SIZE, N_FOX_HEADS, HEAD_DIM), 1.0),
        'cache_fox_v': nrm(ks[3], (DEPTH, n_phys, PAGE_SIZE, N_FOX_HEADS, HEAD_DIM), 1.0),
        'cache_fox_logf': jax.nn.log_sigmoid(3.0 + nrm(ks[4], (DEPTH, n_phys, PAGE_SIZE, N_FOX_HEADS), 1.0)),
        'cache_diff_k': nrm(ks[5], (DEPTH, n_phys, PAGE_SIZE, N_DIFF_HEADS, 2 * HEAD_DIM), 1.0),
        'cache_diff_v': nrm(ks[6], (DEPTH, n_phys, PAGE_SIZE, N_DIFF_HEADS, 2 * HEAD_DIM), 1.0),
        'page_table': page_table,
        'rel_bias_table': nrm(ks[8], (N_BUCKETS, N_DIFF_HEADS), 0.5),
        'attn_norm_g': 1.0 + nrm(ks[9], (DEPTH, D_MODEL), 0.01),
        'w_in': nrm(ks[10], (DEPTH, D_MODEL, P_TOTAL), D_MODEL ** -0.5),
        'b_forget': 3.0 + nrm(ks[11], (DEPTH, N_FOX_HEADS), 0.5),
        'fox_q_norm_g': 1.0 + nrm(ks[12], (DEPTH, HEAD_DIM), 0.01),
        'fox_k_norm_g': 1.0 + nrm(ks[13], (DEPTH, HEAD_DIM), 0.01),
        'diff_q_norm_g': 1.0 + nrm(ks[14], (DEPTH, HEAD_DIM), 0.01),
        'diff_k_norm_g': 1.0 + nrm(ks[15], (DEPTH, HEAD_DIM), 0.01),
        'lambda_q1': nrm(ks[16], (DEPTH, HEAD_DIM), 0.1),
        'lambda_k1': nrm(ks[17], (DEPTH, HEAD_DIM), 0.1),
        'lambda_q2': nrm(ks[18], (DEPTH, HEAD_DIM), 0.1),
        'lambda_k2': nrm(ks[19], (DEPTH, HEAD_DIM), 0.1),
        'diff_out_norm_g': 1.0 + nrm(ks[20], (DEPTH, 2 * HEAD_DIM), 0.01),
        'w_fox_up': nrm(ks[21], (DEPTH, FOX_WIDTH, D_MODEL), FOX_WIDTH ** -0.5),
        'w_diff_up': nrm(ks[22], (DEPTH, DIFF_WIDTH, D_MODEL), DIFF_WIDTH ** -0.5),
        'w_out': nrm(ks[23], (DEPTH, D_MODEL, D_MODEL), D_MODEL ** -0.5),
        'ffn_norm_g': 1.0 + nrm(ks[24], (DEPTH, D_MODEL), 0.01),
        'w_router': nrm(ks[25], (DEPTH, D_MODEL, N_EXPERTS), D_MODEL ** -0.5),
        'b_router': nrm(ks[26], (DEPTH, N_EXPERTS), 0.01),
        'w_gate_up': nrm(ks[27], (DEPTH, N_EXPERTS, D_MODEL, 2 * D_FF), D_MODEL ** -0.5),
        'b_gate_up': nrm(ks[28], (DEPTH, N_EXPERTS, 2 * D_FF), 0.01),
        'w_down': nrm(ks[29], (DEPTH, N_EXPERTS, D_FF, D_MODEL), D_FF ** -0.5),
        'b_down': nrm(ks[30], (DEPTH, N_EXPERTS, D_MODEL), 0.01),
    }


def reference(x_prompt, x_sample, cache_fox_k, cache_fox_v, cache_fox_logf, cache_diff_k,
              cache_diff_v, page_table, rel_bias_table, attn_norm_g, w_in, b_forget,
              fox_q_norm_g, fox_k_norm_g, diff_q_norm_g, diff_k_norm_g, lambda_q1, lambda_k1,
              lambda_q2, lambda_k2, diff_out_norm_g, w_fox_up, w_diff_up, w_out, ffn_norm_g,
              w_router, b_router, w_gate_up, b_gate_up, w_down, b_down):
    yp, ys = x_prompt, x_sample
    rows_prompt, rows_sample = [], []
    for l in range(DEPTH):
        lambda_init = 0.8 - 0.6 * math.exp(-0.3 * l)
        attn_w = (w_in[l], b_forget[l], fox_q_norm_g[l], fox_k_norm_g[l], diff_q_norm_g[l],
                  diff_k_norm_g[l], lambda_q1[l], lambda_k1[l], lambda_q2[l], lambda_k2[l],
                  diff_out_norm_g[l], w_fox_up[l], w_diff_up[l], w_out[l], rel_bias_table, lambda_init)
        moe_w = (w_router[l], b_router[l], w_gate_up[l], b_gate_up[l], w_down[l], b_down[l])
        dy, rows_p = attention_sublayer(rms_norm(yp, attn_norm_g[l]), None, 0, *attn_w)
        yp = yp + dy
        yp = yp + moe_ffn(rms_norm(yp, ffn_norm_g[l]), *moe_w)
        past = (gather_pages(cache_fox_k[l], page_table), gather_pages(cache_fox_v[l], page_table),
                gather_pages(cache_fox_logf[l], page_table), gather_pages(cache_diff_k[l], page_table),
                gather_pages(cache_diff_v[l], page_table))
        dy, rows_s = attention_sublayer(rms_norm(ys, attn_norm_g[l]), past, past[0].shape[1], *attn_w)
        ys = ys + dy
        ys = ys + moe_ffn(rms_norm(ys, ffn_norm_g[l]), *moe_w)
        rows_prompt.append(rows_p)
        rows_sample.append(rows_s)
    new_fox_k_prompt = jnp.stack([r[0] for r in rows_prompt])
    new_fox_v_prompt = jnp.stack([r[1] for r in rows_prompt])
    new_fox_logf_prompt = jnp.stack([r[2] for r in rows_prompt])
    new_diff_k_prompt = jnp.stack([r[3] for r in rows_prompt])
    new_diff_v_prompt = jnp.stack([r[4] for r in rows_prompt])
    new_fox_k_sample = jnp.stack([r[0] for r in rows_sample])
    new_fox_v_sample = jnp.stack([r[1] for r in rows_sample])
    new_fox_logf_sample = jnp.stack([r[2] for r in rows_sample])
    new_diff_k_sample = jnp.stack([r[3] for r in rows_sample])
    new_diff_v_sample = jnp.stack([r[4] for r in rows_sample])
    return (yp, ys, new_fox_k_prompt, new_fox_v_prompt, new_fox_logf_prompt, new_diff_k_prompt,
            new_diff_v_prompt, new_fox_k_sample, new_fox_v_sample, new_fox_logf_sample,
            new_diff_k_sample, new_diff_v_sample)
```

```python
import functools
import math

import jax
import jax.numpy as jnp
from jax import lax
from jax.experimental import pallas as pl
from jax.experimental.pallas import tpu as pltpu

F32 = jnp.float32
BF16 = jnp.bfloat16

HEAD_DIM = 64
N_FOX_HEADS = 8
N_DIFF_HEADS = 4
FOX_WIDTH = N_FOX_HEADS * HEAD_DIM
DIFF_WIDTH = N_DIFF_HEADS * 2 * HEAD_DIM
N_BUCKETS = 32
MAX_DISTANCE = 128
N_EXPERTS = 32
TOP_K = 4
SWIGLU_LIMIT = 7.0
SWIGLU_ALPHA = 1.702
RMS_EPS = 1e-6
ATTN_SCALE = HEAD_DIM ** -0.5
LANES = 128
VMEM_LIMIT = 56 * 1024 * 1024


def _cparams(sem):
    return pltpu.CompilerParams(dimension_semantics=sem, vmem_limit_bytes=VMEM_LIMIT)


def _split3(x):
    x1 = x.astype(BF16)
    r1 = x - x1.astype(F32)
    x2 = r1.astype(BF16)
    x3 = (r1 - x2.astype(F32)).astype(BF16)
    return x1, x2, x3


def _log_sigmoid(x):
    return -(jnp.maximum(-x, 0.0) + jnp.log1p(jnp.exp(-jnp.abs(x))))


def _proj_kernel(x_ref, g_ref, w_ref, wfl_ref, bfl_ref, ones_ref, gains_ref,
                 qf_ref, kf_ref, vf_ref, lf_ref, qd_ref, kd_ref, vd_ref, gf_ref, gd_ref):
    x = x_ref[...]
    ms = jnp.mean(x * x, axis=-1, keepdims=True)
    xn = (x * lax.rsqrt(ms + RMS_EPS) * g_ref[...]).astype(BF16)

    def seg(lo, width):
        return jnp.dot(xn, w_ref[:, lo:lo + width], preferred_element_type=F32)

    ones_bd = ones_ref[...]

    def head_norm(p, gain):
        sq = p * p
        hi = sq.astype(BF16)
        lo = (sq - hi.astype(F32)).astype(BF16)
        ss = (jnp.dot(hi, ones_bd, preferred_element_type=F32)
              + jnp.dot(lo, ones_bd, preferred_element_type=F32))
        return p * lax.rsqrt(ss * (1.0 / HEAD_DIM) + RMS_EPS) * gain

    w = FOX_WIDTH
    qf_ref[...] = (head_norm(seg(0, w), gains_ref[0:1, :]) * ATTN_SCALE).astype(BF16)
    kf_ref[...] = head_norm(seg(w, w), gains_ref[1:2, :])
    vf_ref[...] = seg(2 * w, w)
    qd_ref[...] = (head_norm(seg(3 * w, w), gains_ref[2:3, :]) * ATTN_SCALE).astype(BF16)
    kd_ref[...] = head_norm(seg(4 * w, w), gains_ref[3:4, :])
    vd_ref[...] = seg(5 * w, w)
    d = gf_ref.shape[1]
    gf_ref[...] = jax.nn.sigmoid(seg(6 * w, d))
    gd_ref[...] = jax.nn.sigmoid(seg(6 * w + d, d))
    fl = jnp.dot(xn, wfl_ref[...], preferred_element_type=F32) + bfl_ref[...]
    lf_ref[...] = _log_sigmoid(fl)


def _in_projection(x2d, g, w_main, w_fl, b_fl, ones_bd, gains, tm):
    n, d = x2d.shape
    w = FOX_WIDTH
    tm = min(tm, n)
    assert n % tm == 0
    row = lambda i: (i, 0)
    const = lambda i: (0, 0)
    out_shape = (
        jax.ShapeDtypeStruct((n, w), BF16), jax.ShapeDtypeStruct((n, w), F32),
        jax.ShapeDtypeStruct((n, w), F32), jax.ShapeDtypeStruct((n, LANES), F32),
        jax.ShapeDtypeStruct((n, w), BF16), jax.ShapeDtypeStruct((n, w), F32),
        jax.ShapeDtypeStruct((n, w), F32), jax.ShapeDtypeStruct((n, d), F32),
        jax.ShapeDtypeStruct((n, d), F32),
    )
    out_specs = tuple(pl.BlockSpec((tm, s.shape[1]), row) for s in out_shape)
    return pl.pallas_call(
        _proj_kernel,
        grid=(n // tm,),
        in_specs=[
            pl.BlockSpec((tm, d), row),
            pl.BlockSpec((1, d), const),
            pl.BlockSpec(w_main.shape, const),
            pl.BlockSpec(w_fl.shape, const),
            pl.BlockSpec((1, LANES), const),
            pl.BlockSpec(ones_bd.shape, const),
            pl.BlockSpec(gains.shape, const),
        ],
        out_specs=out_specs,
        out_shape=out_shape,
        compiler_params=_cparams(("arbitrary",)),
        name="in_projection",
    )(x2d, g, w_main, w_fl, b_fl, ones_bd, gains)


def _scan_kernel(x_ref, tri_ref, o_ref, carry_sc):
    @pl.when(pl.program_id(1) == 0)
    def _():
        carry_sc[...] = jnp.zeros_like(carry_sc)

    x = x_ref[0]
    h = x.shape[0]
    x3 = jnp.concatenate(_split3(x), axis=0)
    y3 = jnp.dot(x3, tri_ref[...], preferred_element_type=F32)
    y = y3[0:h] + y3[h:2 * h] + y3[2 * h:3 * h] + carry_sc[:, 0:1]
    o_ref[0] = y
    carry_sc[...] = jnp.broadcast_to(y[:, -1:], carry_sc.shape)


def _prefix_sum_time(xt, tl):
    b, h, t = xt.shape
    tl = min(tl, t)
    assert t % tl == 0
    tri = (jnp.arange(tl)[:, None] <= jnp.arange(tl)[None, :]).astype(BF16)
    return pl.pallas_call(
        _scan_kernel,
        grid=(b, t // tl),
        in_specs=[pl.BlockSpec((1, h, tl), lambda i, j: (i, 0, j)),
                  pl.BlockSpec((tl, tl), lambda i, j: (0, 0))],
        out_specs=pl.BlockSpec((1, h, tl), lambda i, j: (i, 0, j)),
        out_shape=jax.ShapeDtypeStruct((b, h, t), F32),
        scratch_shapes=[pltpu.VMEM((h, LANES), F32)],
        compiler_params=_cparams(("arbitrary", "arbitrary")),
        name="logf_prefix_sum",
    )(xt, tri)


def _lane_tile(x, n):
    return x if n == 1 else jnp.concatenate([x] * n, axis=1)


def _online_softmax(s, m_ref, l_ref, idx):
    tk = s.shape[1]
    m_prev = m_ref[idx]
    m_next = jnp.maximum(m_prev, jnp.max(s, axis=1, keepdims=True))
    p = jnp.exp(s - _lane_tile(m_next, tk // LANES))
    alpha = jnp.exp(m_prev - m_next)
    l_ref[idx] = alpha * l_ref[idx] + jnp.sum(p, axis=1, keepdims=True)
    m_ref[idx] = m_next
    return p, alpha


def _half_mask(shape_lanes):
    lane = lax.broadcasted_iota(jnp.int32, (1, shape_lanes), 1)
    return lane < HEAD_DIM


_NT = (((1,), (1,)), ((), ()))


def _fox_kernel(q_ref, k_ref, v_ref, cq_ref, ckt_ref, o_ref, m_sc, l_sc, acc_sc):
    i = pl.program_id(1)
    j = pl.program_id(2)
    tq = q_ref.shape[1]
    tk = k_ref.shape[1]

    @pl.when(j == 0)
    def _():
        m_sc[...] = jnp.full_like(m_sc, -jnp.inf)
        l_sc[...] = jnp.zeros_like(l_sc)
        acc_sc[...] = jnp.zeros_like(acc_sc)

    first_half = _half_mask(LANES)

    def step(masked):
        if masked:
            row = lax.broadcasted_iota(jnp.int32, (tq, tk), 0)
            col = lax.broadcasted_iota(jnp.int32, (tq, tk), 1)
            keep = col <= row
        for hp in range(N_FOX_HEADS // 2):
            cols = slice(hp * LANES, (hp + 1) * LANES)
            q2 = q_ref[0, :, cols]
            k2 = k_ref[0, :, cols].astype(BF16)
            v2 = v_ref[0, :, cols].astype(BF16)
            pv = []
            alphas = []
            for sub in range(2):
                h = 2 * hp + sub
                half = first_half if sub == 0 else jnp.logical_not(first_half)
                qm = jnp.where(half, q2, jnp.zeros_like(q2))
                s = lax.dot_general(qm, k2, _NT, preferred_element_type=F32)
                s = s + cq_ref[0, :, h:h + 1] - ckt_ref[0, h:h + 1, :]
                if masked:
                    s = jnp.where(keep, s, -jnp.inf)
                p, alpha = _online_softmax(s, m_sc, l_sc, h)
                pv.append(jnp.dot(p.astype(BF16), v2, preferred_element_type=F32))
                alphas.append(alpha)
            acc_sc[hp] = (jnp.where(first_half, alphas[0], alphas[1]) * acc_sc[hp]
                          + jnp.where(first_half, pv[0], pv[1]))

    @pl.when(j < i)
    def _():
        step(False)

    @pl.when(j == i)
    def _():
        step(True)
        for hp in range(N_FOX_HEADS // 2):
            l2 = jnp.where(first_half, l_sc[2 * hp], l_sc[2 * hp + 1])
            o_ref[0, :, hp * LANES:(hp + 1) * LANES] = (acc_sc[hp] / l2).astype(o_ref.dtype)


def _fox_attention(q, k, v, c, ct, tq):
    b, t, w = q.shape
    tq = min(tq, t)
    assert t % tq == 0 and tq % LANES == 0
    nq = t // tq
    qmap = lambda bi, i, j: (bi, i, 0)
    kmap = lambda bi, i, j: (bi, jnp.minimum(j, i), 0)
    return pl.pallas_call(
        _fox_kernel,
        grid=(b, nq, nq),
        in_specs=[
            pl.BlockSpec((1, tq, w), qmap),
            pl.BlockSpec((1, tq, w), kmap),
            pl.BlockSpec((1, tq, w), kmap),
            pl.BlockSpec((1, tq, N_FOX_HEADS), qmap),
            pl.BlockSpec((1, N_FOX_HEADS, tq), lambda bi, i, j: (bi, 0, jnp.minimum(j, i))),
        ],
        out_specs=pl.BlockSpec((1, tq, w), qmap),
        out_shape=jax.ShapeDtypeStruct((b, t, w), BF16),
        scratch_shapes=[pltpu.VMEM((N_FOX_HEADS, tq, LANES), F32),
                        pltpu.VMEM((N_FOX_HEADS, tq, LANES), F32),
                        pltpu.VMEM((N_FOX_HEADS // 2, tq, LANES), F32)],
        compiler_params=_cparams(("arbitrary", "arbitrary", "arbitrary")),
        name="fox_attention",
    )(q, k, v, c, ct)


def _diff_finalize(acc1, l1, acc2, l2, lam, g_out, out_scale):
    o = acc1 / l1 - lam * (acc2 / l2)
    ms = jnp.mean(o * o, axis=-1, keepdims=True)
    return o * lax.rsqrt(ms + RMS_EPS) * g_out * out_scale


def _diff_kernel(lam_ref, q_ref, k_ref, v_ref, bias_ref, go_ref, o_ref, m_sc, l_sc, acc_sc, *, out_scale):
    i = pl.program_id(1)
    j = pl.program_id(2)

    @pl.when(j == 0)
    def _():
        m_sc[...] = jnp.full_like(m_sc, -jnp.inf)
        l_sc[...] = jnp.zeros_like(l_sc)
        acc_sc[...] = jnp.zeros_like(acc_sc)

    first_half = _half_mask(LANES)

    def step(bias_idx):
        for h in range(N_DIFF_HEADS):
            cols = slice(h * LANES, (h + 1) * LANES)
            q2 = q_ref[0, :, cols]
            k2 = k_ref[0, :, cols].astype(BF16)
            v2 = v_ref[0, :, cols].astype(BF16)
            for c in range(2):
                idx = 2 * h + c
                half = first_half if c == 0 else jnp.logical_not(first_half)
                qm = jnp.where(half, q2, jnp.zeros_like(q2))
                s = lax.dot_general(qm, k2, _NT, preferred_element_type=F32)
                if bias_idx is not None:
                    s = s + bias_ref[bias_idx, h]
                p, alpha = _online_softmax(s, m_sc, l_sc, idx)
                acc_sc[idx] = alpha * acc_sc[idx] + jnp.dot(p.astype(BF16), v2, preferred_element_type=F32)

    @pl.when(j < i - 1)
    def _():
        step(None)

    @pl.when(j == i - 1)
    def _():
        step(1)

    @pl.when(j == i)
    def _():
        step(0)
        lam = lam_ref[0]
        for h in range(N_DIFF_HEADS):
            o = _diff_finalize(acc_sc[2 * h], l_sc[2 * h], acc_sc[2 * h + 1], l_sc[2 * h + 1],
                               lam, go_ref[...], out_scale)
            o_ref[0, :, h * LANES:(h + 1) * LANES] = o.astype(o_ref.dtype)


def _t5_bucket_1d(n):
    max_exact = N_BUCKETS // 2
    nf = jnp.maximum(n, 1).astype(F32)
    large = max_exact + (jnp.log(nf / max_exact) / math.log(MAX_DISTANCE / max_exact)
                         * (N_BUCKETS - max_exact)).astype(jnp.int32)
    large = jnp.minimum(large, N_BUCKETS - 1)
    return jnp.where(n < max_exact, n, large)


def _rel_bias_tile(rel_table, dist, valid):
    rel = rel_table.astype(F32) - rel_table[N_BUCKETS - 1].astype(F32)[None, :]
    tile = jnp.moveaxis(rel[_t5_bucket_1d(jnp.maximum(dist, 0))], -1, 0)
    return jnp.where(valid[None], tile, -jnp.inf)


def _diff_attention(q, k, v, rel_table, lam, g_out, out_scale, tq):
    b, t, w = q.shape
    tq = min(tq, t)
    assert t % tq == 0 and tq >= MAX_DISTANCE and tq % LANES == 0
    nq = t // tq
    r = jnp.arange(tq, dtype=jnp.int32)
    d0 = r[:, None] - r[None, :]
    bias = jnp.stack([_rel_bias_tile(rel_table, d0, d0 >= 0),
                      _rel_bias_tile(rel_table, d0 + tq, jnp.ones_like(d0, dtype=bool))])
    qmap = lambda bi, i, j: (bi, i, 0)
    kmap = lambda bi, i, j: (bi, jnp.minimum(j, i), 0)
    return pl.pallas_call(
        functools.partial(_diff_kernel, out_scale=out_scale),
        grid=(b, nq, nq),
        in_specs=[
            pl.BlockSpec(memory_space=pltpu.SMEM),
            pl.BlockSpec((1, tq, w), qmap),
            pl.BlockSpec((1, tq, w), kmap),
            pl.BlockSpec((1, tq, w), kmap),
            pl.BlockSpec(bias.shape, lambda bi, i, j: (0, 0, 0, 0)),
            pl.BlockSpec((1, LANES), lambda bi, i, j: (0, 0)),
        ],
        out_specs=pl.BlockSpec((1, tq, w), qmap),
        out_shape=jax.ShapeDtypeStruct((b, t, w), BF16),
        scratch_shapes=[pltpu.VMEM((2 * N_DIFF_HEADS, tq, LANES), F32),
                        pltpu.VMEM((2 * N_DIFF_HEADS, tq, LANES), F32),
                        pltpu.VMEM((2 * N_DIFF_HEADS, tq, LANES), F32)],
        compiler_params=_cparams(("arbitrary", "arbitrary", "arbitrary")),
        name="diff_attention",
    )(lam, q, k, v, bias, g_out)


def _out_kernel(x_ref, of_ref, od_ref, gf_ref, gd_ref, wf_ref, wd_ref, wo_ref, g2_ref, wr_ref, br_ref,
                y_ref, h2_ref, lg_ref):
    merged = (gf_ref[...] * jnp.dot(of_ref[...], wf_ref[...], preferred_element_type=F32)
              + gd_ref[...] * jnp.dot(od_ref[...], wd_ref[...], preferred_element_type=F32))
    y = x_ref[...] + jnp.dot(merged.astype(BF16), wo_ref[...], preferred_element_type=F32)
    y_ref[...] = y
    ms = jnp.mean(y * y, axis=-1, keepdims=True)
    h2 = y * lax.rsqrt(ms + RMS_EPS) * g2_ref[...]
    h2_ref[...] = h2
    lg_ref[...] = jnp.dot(h2.astype(BF16), wr_ref[...], preferred_element_type=F32) + br_ref[...]


def _out_projection(x2d, o_f, o_d, gate_f, gate_d, w_fu, w_du, w_o, g2, w_r, b_r, tm):
    n, d = x2d.shape
    tm = min(tm, n)
    assert n % tm == 0
    row = lambda i: (i, 0)
    const = lambda i: (0, 0)
    return pl.pallas_call(
        _out_kernel,
        grid=(n // tm,),
        in_specs=[
            pl.BlockSpec((tm, d), row), pl.BlockSpec((tm, o_f.shape[1]), row),
            pl.BlockSpec((tm, o_d.shape[1]), row), pl.BlockSpec((tm, d), row), pl.BlockSpec((tm, d), row),
            pl.BlockSpec(w_fu.shape, const), pl.BlockSpec(w_du.shape, const), pl.BlockSpec(w_o.shape, const),
            pl.BlockSpec((1, d), const), pl.BlockSpec(w_r.shape, const), pl.BlockSpec((1, LANES), const),
        ],
        out_specs=(pl.BlockSpec((tm, d), row), pl.BlockSpec((tm, d), row), pl.BlockSpec((tm, LANES), row)),
        out_shape=(jax.ShapeDtypeStruct((n, d), F32), jax.ShapeDtypeStruct((n, d), F32),
                   jax.ShapeDtypeStruct((n, LANES), F32)),
        compiler_params=_cparams(("arbitrary",)),
        name="out_projection",
    )(x2d, o_f, o_d, gate_f, gate_d, w_fu, w_du, w_o, g2, w_r, b_r)


def _row_gather_start(src_hbm, idx_ref, n_rows, dst, sem):
    def body(r, carry):
        tok = idx_ref[0, 0, r]
        pltpu.make_async_copy(src_hbm.at[pl.ds(tok, 1)], dst.at[pl.ds(r, 1)], sem).start()
        return carry
    lax.fori_loop(0, n_rows, body, 0)


def _row_gather_wait(src_hbm, n_rows, dst, sem):
    pltpu.make_async_copy(src_hbm.at[pl.ds(0, n_rows)], dst, sem).wait()


def _moe_kernel(be_ref, nused_ref, idx_ref, idx_next_ref, h_hbm, wg_ref, wu_ref, bg_ref, bu_ref,
                wd_ref, bd_ref, y_ref, xbuf, sem):
    i = pl.program_id(0)
    n_used = nused_ref[0]
    bm = xbuf.shape[1]
    slot = lax.rem(i, 2)

    @pl.when(jnp.logical_and(i == 0, n_used > 0))
    def _():
        _row_gather_start(h_hbm, idx_ref, bm, xbuf.at[0], sem.at[0])

    @pl.when(i + 1 < n_used)
    def _():
        _row_gather_start(h_hbm, idx_next_ref, bm, xbuf.at[1 - slot], sem.at[1 - slot])

    @pl.when(i < n_used)
    def _():
        _row_gather_wait(h_hbm, bm, xbuf.at[slot], sem.at[slot])
        x = xbuf[slot].astype(BF16)
        gate = jnp.dot(x, wg_ref[0], preferred_element_type=F32) + bg_ref[0]
        up = jnp.dot(x, wu_ref[0], preferred_element_type=F32) + bu_ref[0]
        gate = jnp.minimum(gate, SWIGLU_LIMIT)
        up = jnp.clip(up, -SWIGLU_LIMIT, SWIGLU_LIMIT)
        glu = gate * jax.nn.sigmoid(SWIGLU_ALPHA * gate)
        act = ((up + 1.0) * glu).astype(BF16)
        y_ref[...] = jnp.dot(act, wd_ref[0], preferred_element_type=F32) + bd_ref[0]

    @pl.when(i >= n_used)
    def _():
        y_ref[...] = jnp.zeros_like(y_ref)


def _expert_ffn(h2, slot_token, block_expert, n_used, w_g, w_u, b_g, b_u, w_d, b_d, bm):
    n, d = h2.shape
    n_blocks = block_expert.shape[0]
    dff = w_g.shape[2]
    idx3 = slot_token.reshape(n_blocks, 1, bm)
    wmap = lambda i, be, nu: (be[i], 0, 0)
    grid_spec = pltpu.PrefetchScalarGridSpec(
        num_scalar_prefetch=2,
        grid=(n_blocks,),
        in_specs=[
            pl.BlockSpec((1, 1, bm), lambda i, be, nu: (i, 0, 0), memory_space=pltpu.SMEM),
            pl.BlockSpec((1, 1, bm), lambda i, be, nu: (jnp.minimum(i + 1, n_blocks - 1), 0, 0),
                         memory_space=pltpu.SMEM),
            pl.BlockSpec(memory_space=pl.ANY),
            pl.BlockSpec((1, d, dff), wmap), pl.BlockSpec((1, d, dff), wmap),
            pl.BlockSpec((1, 1, dff), wmap), pl.BlockSpec((1, 1, dff), wmap),
            pl.BlockSpec((1, dff, d), wmap), pl.BlockSpec((1, 1, d), wmap),
        ],
        out_specs=pl.BlockSpec((bm, d), lambda i, be, nu: (i, 0)),
        scratch_shapes=[pltpu.VMEM((2, bm, d), F32), pltpu.SemaphoreType.DMA((2,))],
    )
    return pl.pallas_call(
        _moe_kernel,
        grid_spec=grid_spec,
        out_shape=jax.ShapeDtypeStruct((n_blocks * bm, d), F32),
        compiler_params=_cparams(("arbitrary",)),
        name="expert_ffn",
    )(block_expert, n_used, idx3, idx3, h2, w_g, w_u, b_g, b_u, w_d, b_d)


def _combine_kernel(idx_ref, idx_next_ref, yb_hbm, resid_ref, gates_ref, o_ref, buf, sem):
    i = pl.program_id(0)
    n = pl.num_programs(0)
    rows = buf.shape[1]
    tc = resid_ref.shape[0]
    slot = lax.rem(i, 2)

    @pl.when(i == 0)
    def _():
        _row_gather_start(yb_hbm, idx_ref, rows, buf.at[0], sem.at[0])

    @pl.when(i + 1 < n)
    def _():
        _row_gather_start(yb_hbm, idx_next_ref, rows, buf.at[1 - slot], sem.at[1 - slot])

    _row_gather_wait(yb_hbm, rows, buf.at[slot], sem.at[slot])
    acc = jnp.zeros(o_ref.shape, F32)
    for k in range(TOP_K):
        acc = acc + buf[slot, k * tc:(k + 1) * tc, :] * gates_ref[:, k:k + 1]
    o_ref[...] = resid_ref[...] + acc


def _combine(yb, slot_km, gates, resid, tc):
    n, d = resid.shape
    n_tiles = n // tc
    rows = TOP_K * tc
    return pl.pallas_call(
        _combine_kernel,
        grid=(n_tiles,),
        in_specs=[
            pl.BlockSpec((1, 1, rows), lambda i: (i, 0, 0), memory_space=pltpu.SMEM),
            pl.BlockSpec((1, 1, rows), lambda i: (jnp.minimum(i + 1, n_tiles - 1), 0, 0),
                         memory_space=pltpu.SMEM),
            pl.BlockSpec(memory_space=pl.ANY),
            pl.BlockSpec((tc, d), lambda i: (i, 0)),
            pl.BlockSpec((tc, TOP_K), lambda i: (i, 0)),
        ],
        out_specs=pl.BlockSpec((tc, d), lambda i: (i, 0)),
        out_shape=jax.ShapeDtypeStruct((n, d), F32),
        scratch_shapes=[pltpu.VMEM((2, rows, d), F32), pltpu.SemaphoreType.DMA((2,))],
        compiler_params=_cparams(("arbitrary",)),
        name="moe_combine",
    )(slot_km, slot_km, yb, resid, gates)


def _route(logits, bm):
    n = logits.shape[0]
    top_val, top_idx = lax.top_k(logits, TOP_K)
    gates = jax.nn.softmax(top_val, axis=-1)
    a = n * TOP_K
    flat_e = top_idx.reshape(a)
    order = jnp.argsort(flat_e)
    sorted_e = flat_e[order]
    counts = jnp.bincount(flat_e, length=N_EXPERTS)
    padded = (counts + bm - 1) // bm * bm
    start = jnp.cumsum(counts) - counts
    pend = jnp.cumsum(padded)
    pstart = pend - padded
    slot_sorted = (pstart[sorted_e] + jnp.arange(a) - start[sorted_e]).astype(jnp.int32)
    slot = jnp.zeros((a,), jnp.int32).at[order].set(slot_sorted)
    n_blocks = -(-a // bm) + N_EXPERTS
    slot_token = jnp.zeros((n_blocks * bm,), jnp.int32).at[slot].set(
        (jnp.arange(a) // TOP_K).astype(jnp.int32))
    block_expert = jnp.minimum(
        jnp.searchsorted(pend, jnp.arange(n_blocks) * bm, side='right'), N_EXPERTS - 1).astype(jnp.int32)
    n_used = (pend[-1] // bm).astype(jnp.int32).reshape(1)
    return gates, slot.reshape(n, TOP_K), slot_token, block_expert, n_used


def _moe_layer(y1, h2, logits, moe_w, bm, tc):
    n, d = y1.shape
    w_g, w_u, b_g, b_u, w_d, b_d = moe_w
    bm = min(bm, n)
    tc = min(tc, n)
    gates, slot, slot_token, block_expert, n_used = _route(logits[:, :N_EXPERTS], bm)
    yb = _expert_ffn(h2, slot_token, block_expert, n_used, w_g, w_u, b_g, b_u, w_d, b_d, bm)
    slot_km = slot.reshape(n // tc, tc, TOP_K).transpose(0, 2, 1).reshape(n // tc, 1, TOP_K * tc)
    return _combine(yb, slot_km, gates, y1, tc)


def _sample_attn_kernel(pt_ref, lam_ref, qsf_ref, qsd_ref, cncol_ref, cnrow_ref, snew_ref, blast_ref,
                        knf_ref, vnf_ref, knd_ref, vnd_ref, tri_ref, go_ref, *rest, pps, out_scale):
    page_refs = rest[:5 * pps]
    of_ref, od_ref, m_sc, l_sc, accf_sc, accd_sc, carry_sc = rest[5 * pps:]
    kf_refs = page_refs[0 * pps:1 * pps]
    vf_refs = page_refs[1 * pps:2 * pps]
    kd_refs = page_refs[2 * pps:3 * pps]
    vd_refs = page_refs[3 * pps:4 * pps]
    lft_refs = page_refs[4 * pps:5 * pps]
    g = pl.program_id(1)
    n_g = pl.num_programs(1)
    nh = N_FOX_HEADS
    half_rows = nh * cncol_ref.shape[1] // 2 // nh
    w = FOX_WIDTH

    @pl.when(g == 0)
    def _():
        m_sc[...] = jnp.full_like(m_sc, -jnp.inf)
        l_sc[...] = jnp.zeros_like(l_sc)
        accf_sc[...] = jnp.zeros_like(accf_sc)
        accd_sc[...] = jnp.zeros_like(accd_sc)
        carry_sc[...] = jnp.zeros_like(carry_sc)

    qsf = qsf_ref[0]
    qsd = qsd_ref[0]

    def update(s, vf, vd):
        p, alpha = _online_softmax(s, m_sc, l_sc, slice(None))
        pb = p.astype(BF16)
        reps = w // LANES
        accf_sc[...] = (_lane_tile(alpha[:half_rows], reps) * accf_sc[...]
                        + jnp.dot(pb[:half_rows], vf, preferred_element_type=F32))
        accd_sc[...] = (_lane_tile(alpha[half_rows:], reps) * accd_sc[...]
                        + jnp.dot(pb[half_rows:], vd, preferred_element_type=F32))

    def past_chunk(with_bias):
        carry = carry_sc[...]
        suf = [None] * pps
        for p in reversed(range(pps)):
            x = lft_refs[p][0]
            x3 = jnp.concatenate(_split3(x), axis=0)
            y3 = jnp.dot(x3, tri_ref[...], preferred_element_type=F32)
            within = y3[0:nh] + y3[nh:2 * nh] + y3[2 * nh:3 * nh]
            suf[p] = within + carry
            carry = carry + (within[:, 0:1] + x[:, 0:1])
        carry_sc[...] = carry
        sufc = jnp.concatenate(suf, axis=1)
        r = sufc.shape[1]
        tok = half_rows // nh
        suf_rows = jnp.concatenate(
            [jnp.broadcast_to(sufc[h:h + 1, :], (tok, r)) for h in range(nh)], axis=0)
        kf = jnp.concatenate([ref[0].astype(BF16) for ref in kf_refs], axis=0)
        kd = jnp.concatenate([ref[0].astype(BF16) for ref in kd_refs], axis=0)
        s_f = lax.dot_general(qsf, kf, _NT, preferred_element_type=F32)
        s_f = s_f + _lane_tile(cncol_ref[0, :half_rows, :], r // LANES) + suf_rows
        s_d = lax.dot_general(qsd, kd, _NT, preferred_element_type=F32)
        s = jnp.concatenate([s_f, s_d], axis=0)
        if with_bias:
            s = s + blast_ref[...]
        vf = jnp.concatenate([ref[0].astype(BF16) for ref in vf_refs], axis=0)
        vd = jnp.concatenate([ref[0].astype(BF16) for ref in vd_refs], axis=0)
        update(s, vf, vd)

    @pl.when(g == 0)
    def _():
        past_chunk(True)

    @pl.when(g > 0)
    def _():
        past_chunk(False)

    @pl.when(g == n_g - 1)
    def _():
        s_f = lax.dot_general(qsf, knf_ref[0].astype(BF16), _NT, preferred_element_type=F32)
        s_d = lax.dot_general(qsd, knd_ref[0].astype(BF16), _NT, preferred_element_type=F32)
        s = jnp.concatenate([s_f, s_d], axis=0) + (cncol_ref[0] - cnrow_ref[0]) + snew_ref[...]
        update(s, vnf_ref[0].astype(BF16), vnd_ref[0].astype(BF16))

        tok = half_rows // nh
        row_grp = lax.broadcasted_iota(jnp.int32, (half_rows, w), 0) // tok
        col_grp = lax.broadcasted_iota(jnp.int32, (half_rows, w), 1) // HEAD_DIM
        own = row_grp == col_grp
        l = l_sc[...]
        nf = jnp.where(own, accf_sc[...] / l[:half_rows, 0:1], 0.0)
        o_f = nf[0:tok]
        for h in range(1, nh):
            o_f = o_f + nf[h * tok:(h + 1) * tok]
        of_ref[0] = o_f.astype(of_ref.dtype)

        nd = accd_sc[...] / l[half_rows:, 0:1]
        lam = lam_ref[0]
        for h in range(N_DIFF_HEADS):
            cols = slice(h * LANES, (h + 1) * LANES)
            a1 = nd[(2 * h) * tok:(2 * h + 1) * tok, cols]
            a2 = nd[(2 * h + 1) * tok:(2 * h + 2) * tok, cols]
            o = a1 - lam * a2
            ms = jnp.mean(o * o, axis=-1, keepdims=True)
            od_ref[0, :, cols] = (o * lax.rsqrt(ms + RMS_EPS) * go_ref[...] * out_scale).astype(od_ref.dtype)


def _sample_attention(qf, qd, kf_new, vf_new, kd_new, vd_new, c_new, caches, page_table, rel_table, lam,
                      g_out, out_scale, pps):
    db, s_new, w = qf.shape
    ck_f, cv_f, ck_d, cv_d, clf_t = caches
    page = ck_f.shape[1]
    n_pages = page_table.shape[1]
    pps = min(pps, n_pages)
    assert n_pages % pps == 0 and page == LANES and s_new * N_FOX_HEADS * 2 == LANES
    n_g = n_pages // pps
    r = pps * page
    nh = N_FOX_HEADS
    half_rows = nh * s_new

    grp_mask = (jnp.arange(half_rows)[:, None] // s_new) == (jnp.arange(w)[None, :] // HEAD_DIM)
    qsf = jnp.where(grp_mask[None], jnp.tile(qf, (1, nh, 1)), jnp.zeros((), qf.dtype))
    qsd = jnp.where(grp_mask[None], jnp.tile(qd, (1, nh, 1)), jnp.zeros((), qd.dtype))

    c_rows = jnp.transpose(c_new, (0, 2, 1)).reshape(db, half_rows)
    cn_col = jnp.concatenate([c_rows, jnp.zeros_like(c_rows)], axis=1)[:, :, None]
    cn_col = jnp.broadcast_to(cn_col, (db, LANES, LANES))
    c_keys = jnp.repeat(jnp.transpose(c_new, (0, 2, 1)), s_new, axis=1)
    cn_row = jnp.zeros((db, LANES, LANES), F32).at[:, :half_rows, :s_new].set(c_keys)

    t_of_row = jnp.arange(LANES, dtype=jnp.int32) % s_new
    key = jnp.arange(LANES, dtype=jnp.int32)
    valid_new = (key[None, :] <= t_of_row[:, None]) & (key[None, :] < s_new)
    is_diff = (jnp.arange(LANES) >= half_rows)
    head_of_row = jnp.clip((jnp.arange(LANES) - half_rows) // (2 * s_new), 0, N_DIFF_HEADS - 1)
    rel = rel_table.astype(F32) - rel_table[N_BUCKETS - 1].astype(F32)[None, :]

    def rel_rows(dist):
        vals = rel[_t5_bucket_1d(jnp.maximum(dist, 0)), head_of_row[:, None]]
        return jnp.where(is_diff[:, None], vals, 0.0)

    s_new_tile = jnp.where(valid_new, rel_rows(t_of_row[:, None] - key[None, :]), -jnp.inf)
    key_r = jnp.arange(r, dtype=jnp.int32)
    b_last = rel_rows(r - key_r[None, :] + t_of_row[:, None])

    def pad_rows(x):
        return jnp.pad(x, ((0, 0), (0, LANES - s_new), (0, 0)))

    tri = (jnp.arange(page)[:, None] > jnp.arange(page)[None, :]).astype(BF16)

    bmap = lambda b, g, pt: (b, 0, 0)
    const2 = lambda b, g, pt: (0, 0)

    def page_map(p):
        return lambda b, g, pt: (pt[b, (n_g - 1 - g) * pps + p], 0, 0)

    in_specs = [
        pl.BlockSpec(memory_space=pltpu.SMEM),
        pl.BlockSpec((1, half_rows, w), bmap), pl.BlockSpec((1, half_rows, w), bmap),
        pl.BlockSpec((1, LANES, LANES), bmap), pl.BlockSpec((1, LANES, LANES), bmap),
        pl.BlockSpec((LANES, LANES), const2), pl.BlockSpec((LANES, r), const2),
        pl.BlockSpec((1, LANES, w), bmap), pl.BlockSpec((1, LANES, w), bmap),
        pl.BlockSpec((1, LANES, w), bmap), pl.BlockSpec((1, LANES, w), bmap),
        pl.BlockSpec((page, page), const2), pl.BlockSpec((1, LANES), const2),
    ]
    operands = [lam, qsf, qsd, cn_col, cn_row, s_new_tile, b_last,
                pad_rows(kf_new), pad_rows(vf_new), pad_rows(kd_new), pad_rows(vd_new), tri, g_out]
    for arr in (ck_f, cv_f, ck_d, cv_d):
        for p in range(pps):
            in_specs.append(pl.BlockSpec((1, page, w), page_map(p)))
            operands.append(arr)
    for p in range(pps):
        in_specs.append(pl.BlockSpec((1, nh, page), page_map(p)))
        operands.append(clf_t)

    grid_spec = pltpu.PrefetchScalarGridSpec(
        num_scalar_prefetch=1,
        grid=(db, n_g),
        in_specs=in_specs,
        out_specs=(pl.BlockSpec((1, s_new, w), bmap), pl.BlockSpec((1, s_new, w), bmap)),
        scratch_shapes=[pltpu.VMEM((LANES, LANES), F32), pltpu.VMEM((LANES, LANES), F32),
                        pltpu.VMEM((half_rows, w), F32), pltpu.VMEM((half_rows, w), F32),
                        pltpu.VMEM((nh, LANES), F32)],
    )
    return pl.pallas_call(
        functools.partial(_sample_attn_kernel, pps=pps, out_scale=out_scale),
        grid_spec=grid_spec,
        out_shape=(jax.ShapeDtypeStruct((db, s_new, w), BF16), jax.ShapeDtypeStruct((db, s_new, w), BF16)),
        compiler_params=_cparams(("arbitrary", "arbitrary")),
        name="sample_attention",
    )(page_table, *operands)


PROJ_ROWS = 256
ATTN_BLOCK = 512
SCAN_BLOCK = 512
MOE_ROWS = 256
COMBINE_ROWS = 128
PAGES_PER_STEP = 8


def _split_points(d_model):
    sizes = (FOX_WIDTH, FOX_WIDTH, FOX_WIDTH, N_FOX_HEADS, DIFF_WIDTH, DIFF_WIDTH, DIFF_WIDTH, d_model)
    pts, acc = [], 0
    for s in sizes:
        acc += s
        pts.append(acc)
    return pts


def _layer_weights(l, d_model, attn_norm_g, w_in, b_forget, fox_q_norm_g, fox_k_norm_g, diff_q_norm_g,
                   diff_k_norm_g, lambda_q1, lambda_k1, lambda_q2, lambda_k2, diff_out_norm_g, w_fox_up,
                   w_diff_up, w_out, ffn_norm_g, w_router, b_router, w_gate_up, b_gate_up, w_down, b_down):
    lambda_init = 0.8 - 0.6 * math.exp(-0.3 * l)
    wq_f, wk_f, wv_f, w_fl, wq_d, wk_d, wv_d, wg_f, wg_d = jnp.split(w_in[l], _split_points(d_model), axis=1)
    w_main = jnp.concatenate([wq_f, wk_f, wv_f, wq_d, wk_d, wv_d, wg_f, wg_d], axis=1).astype(BF16)
    w_fl = jnp.pad(w_fl, ((0, 0), (0, LANES - N_FOX_HEADS))).astype(BF16)
    b_fl = jnp.pad(b_forget[l].astype(F32), (0, LANES - N_FOX_HEADS)).reshape(1, LANES)
    grp = jnp.arange(FOX_WIDTH) // HEAD_DIM
    ones_bd = (grp[:, None] == grp[None, :]).astype(BF16)
    reps = FOX_WIDTH // HEAD_DIM
    gains = jnp.stack([jnp.tile(g[l].astype(F32), reps)
                       for g in (fox_q_norm_g, fox_k_norm_g, diff_q_norm_g, diff_k_norm_g)])
    lam = (jnp.exp(jnp.sum(lambda_q1[l].astype(F32) * lambda_k1[l].astype(F32)))
           - jnp.exp(jnp.sum(lambda_q2[l].astype(F32) * lambda_k2[l].astype(F32))) + lambda_init)
    w_r = jnp.pad(w_router[l], ((0, 0), (0, LANES - N_EXPERTS))).astype(BF16)
    b_r = jnp.pad(b_router[l].astype(F32), (0, LANES - N_EXPERTS)).reshape(1, LANES)
    wgu = w_gate_up[l]
    bgu = b_gate_up[l].astype(F32)
    moe_w = (wgu[:, :, 0::2].astype(BF16), wgu[:, :, 1::2].astype(BF16),
             bgu[:, None, 0::2], bgu[:, None, 1::2],
             w_down[l].astype(BF16), b_down[l].astype(F32)[:, None, :])
    return dict(
        lambda_init=lambda_init, g1=attn_norm_g[l].astype(F32).reshape(1, d_model), w_main=w_main, w_fl=w_fl,
        b_fl=b_fl, ones_bd=ones_bd, gains=gains, lam=lam.reshape(1).astype(F32),
        g_out=diff_out_norm_g[l].astype(F32).reshape(1, LANES),
        w_fu=w_fox_up[l].astype(BF16), w_du=w_diff_up[l].astype(BF16), w_o=w_out[l].astype(BF16),
        g2=ffn_norm_g[l].astype(F32).reshape(1, d_model), w_r=w_r, b_r=b_r, moe_w=moe_w)


def _project(x, lw):
    b, t, d = x.shape
    outs = _in_projection(x.reshape(b * t, d), lw['g1'], lw['w_main'], lw['w_fl'], lw['b_fl'],
                          lw['ones_bd'], lw['gains'], PROJ_ROWS)
    qf, kf, vf, lf, qd, kd, vd, gf, gd = outs
    logf = lf[:, :N_FOX_HEADS].reshape(b, t, N_FOX_HEADS)
    r3 = lambda a: a.reshape(b, t, a.shape[-1])
    return r3(qf), r3(kf), r3(vf), logf, r3(qd), r3(kd), r3(vd), gf, gd


def _finish_layer(x, o_f, o_d, gf, gd, lw, moe_rows):
    b, t, d = x.shape
    n = b * t
    y1, h2, logits = _out_projection(x.reshape(n, d), o_f.reshape(n, -1), o_d.reshape(n, -1), gf, gd,
                                     lw['w_fu'], lw['w_du'], lw['w_o'], lw['g2'], lw['w_r'], lw['b_r'],
                                     PROJ_ROWS)
    return _moe_layer(y1, h2, logits, lw['moe_w'], moe_rows, COMBINE_ROWS).reshape(b, t, d)


def _new_rows(kf, vf, logf, kd, vd):
    b, t, _ = kf.shape
    return (kf.reshape(b, t, N_FOX_HEADS, HEAD_DIM), vf.reshape(b, t, N_FOX_HEADS, HEAD_DIM), logf,
            kd.reshape(b, t, N_DIFF_HEADS, 2 * HEAD_DIM), vd.reshape(b, t, N_DIFF_HEADS, 2 * HEAD_DIM))


def _prompt_layer(x, lw, rel_table):
    qf, kf, vf, logf, qd, kd, vd, gf, gd = _project(x, lw)
    ct = _prefix_sum_time(jnp.transpose(logf, (0, 2, 1)), SCAN_BLOCK)
    c = jnp.transpose(ct, (0, 2, 1))
    o_f = _fox_attention(qf, kf, vf, c, ct, ATTN_BLOCK)
    o_d = _diff_attention(qd, kd, vd, rel_table, lw['lam'], lw['g_out'], 1.0 - lw['lambda_init'], ATTN_BLOCK)
    return _finish_layer(x, o_f, o_d, gf, gd, lw, MOE_ROWS), _new_rows(kf, vf, logf, kd, vd)


def _sample_layer(x, lw, rel_table, caches, page_table):
    qf, kf, vf, logf, qd, kd, vd, gf, gd = _project(x, lw)
    b, t, _ = logf.shape
    logf_t = jnp.pad(jnp.transpose(logf, (0, 2, 1)), ((0, 0), (0, 0), (0, LANES - t)))
    c = jnp.transpose(_prefix_sum_time(logf_t, LANES)[:, :, :t], (0, 2, 1))
    o_f, o_d = _sample_attention(qf, qd, kf, vf, kd, vd, c, caches, page_table, rel_table, lw['lam'],
                                 lw['g_out'], 1.0 - lw['lambda_init'], PAGES_PER_STEP)
    return _finish_layer(x, o_f, o_d, gf, gd, lw, LANES), _new_rows(kf, vf, logf, kd, vd)


def kernel(x_prompt, x_sample, cache_fox_k, cache_fox_v, cache_fox_logf, cache_diff_k, cache_diff_v, page_table, rel_bias_table, attn_norm_g, w_in, b_forget, fox_q_norm_g, fox_k_norm_g, diff_q_norm_g, diff_k_norm_g, lambda_q1, lambda_k1, lambda_q2, lambda_k2, diff_out_norm_g, w_fox_up, w_diff_up, w_out, ffn_norm_g, w_router, b_router, w_gate_up, b_gate_up, w_down, b_down):
    depth = w_in.shape[0]
    d_model = x_prompt.shape[-1]
    yp, ys = x_prompt, x_sample
    rows_p, rows_s = [], []
    for l in range(depth):
        lw = _layer_weights(l, d_model, attn_norm_g, w_in, b_forget, fox_q_norm_g, fox_k_norm_g,
                            diff_q_norm_g, diff_k_norm_g, lambda_q1, lambda_k1, lambda_q2, lambda_k2,
                            diff_out_norm_g, w_fox_up, w_diff_up, w_out, ffn_norm_g, w_router, b_router,
                            w_gate_up, b_gate_up, w_down, b_down)
        yp, rp = _prompt_layer(yp, lw, rel_bias_table)
        n_phys, page = cache_fox_k.shape[1:3]
        flat = lambda a: a[l].reshape(n_phys, page, -1)
        caches = (flat(cache_fox_k), flat(cache_fox_v), flat(cache_diff_k), flat(cache_diff_v),
                  jnp.transpose(cache_fox_logf[l].astype(F32), (0, 2, 1)))
        ys, rs = _sample_layer(ys, lw, rel_bias_table, caches, page_table)
        rows_p.append(rp)
        rows_s.append(rs)
    stack = lambda rows, i: jnp.stack([r[i] for r in rows])
    return (yp, ys) + tuple(stack(rows_p, i) for i in range(5)) + tuple(stack(rows_s, i) for i in range(5))
```

```python
import functools
import math

import jax
import jax.numpy as jnp
from jax import lax
from jax.experimental import pallas as pl
from jax.experimental.pallas import tpu as pltpu

F32 = jnp.float32
BF16 = jnp.bfloat16

HEAD_DIM = 64
N_FOX_HEADS = 8
N_DIFF_HEADS = 4
FOX_WIDTH = N_FOX_HEADS * HEAD_DIM
DIFF_WIDTH = N_DIFF_HEADS * 2 * HEAD_DIM
N_BUCKETS = 32
MAX_DISTANCE = 128
N_EXPERTS = 32
TOP_K = 4
SWIGLU_LIMIT = 7.0
SWIGLU_ALPHA = 1.702
RMS_EPS = 1e-6
ATTN_SCALE = HEAD_DIM ** -0.5
LANES = 128
VMEM_LIMIT = 56 * 1024 * 1024


def _cparams(sem):
    return pltpu.CompilerParams(dimension_semantics=sem, vmem_limit_bytes=VMEM_LIMIT)


def _split3(x):
    x1 = x.astype(BF16)
    r1 = x - x1.astype(F32)
    x2 = r1.astype(BF16)
    x3 = (r1 - x2.astype(F32)).astype(BF16)
    return x1, x2, x3


def _log_sigmoid(x):
    return -(jnp.maximum(-x, 0.0) + jnp.log1p(jnp.exp(-jnp.abs(x))))


def _proj_kernel(x_ref, g_ref, w_ref, wfl_ref, bfl_ref, ones_ref, gains_ref,
                 qf_ref, kf_ref, vf_ref, lf_ref, qd_ref, kd_ref, vd_ref, gf_ref, gd_ref):
    x = x_ref[...]
    ms = jnp.mean(x * x, axis=-1, keepdims=True)
    xn = (x * lax.rsqrt(ms + RMS_EPS) * g_ref[...]).astype(BF16)

    def seg(lo, width):
        return jnp.dot(xn, w_ref[:, lo:lo + width], preferred_element_type=F32)

    ones_bd = ones_ref[...]

    def head_norm(p, gain):
        sq = p * p
        hi = sq.astype(BF16)
        lo = (sq - hi.astype(F32)).astype(BF16)
        ss = (jnp.dot(hi, ones_bd, preferred_element_type=F32)
              + jnp.dot(lo, ones_bd, preferred_element_type=F32))
        return p * lax.rsqrt(ss * (1.0 / HEAD_DIM) + RMS_EPS) * gain

    w = FOX_WIDTH
    qf_ref[...] = (head_norm(seg(0, w), gains_ref[0:1, :]) * ATTN_SCALE).astype(BF16)
    kf_ref[...] = head_norm(seg(w, w), gains_ref[1:2, :])
    vf_ref[...] = seg(2 * w, w)
    qd_ref[...] = (head_norm(seg(3 * w, w), gains_ref[2:3, :]) * ATTN_SCALE).astype(BF16)
    kd_ref[...] = head_norm(seg(4 * w, w), gains_ref[3:4, :])
    vd_ref[...] = seg(5 * w, w)
    d = gf_ref.shape[1]
    gf_ref[...] = jax.nn.sigmoid(seg(6 * w, d))
    gd_ref[...] = jax.nn.sigmoid(seg(6 * w + d, d))
    fl = jnp.dot(xn, wfl_ref[...], preferred_element_type=F32) + bfl_ref[...]
    lf_ref[...] = _log_sigmoid(fl)


def _in_projection(x2d, g, w_main, w_fl, b_fl, ones_bd, gains, tm):
    n, d = x2d.shape
    w = FOX_WIDTH
    tm = min(tm, n)
    assert n % tm == 0
    row = lambda i: (i, 0)
    const = lambda i: (0, 0)
    out_shape = (
        jax.ShapeDtypeStruct((n, w), BF16), jax.ShapeDtypeStruct((n, w), F32),
        jax.ShapeDtypeStruct((n, w), F32), jax.ShapeDtypeStruct((n, LANES), F32),
        jax.ShapeDtypeStruct((n, w), BF16), jax.ShapeDtypeStruct((n, w), F32),
        jax.ShapeDtypeStruct((n, w), F32), jax.ShapeDtypeStruct((n, d), F32),
        jax.ShapeDtypeStruct((n, d), F32),
    )
    out_specs = tuple(pl.BlockSpec((tm, s.shape[1]), row) for s in out_shape)
    return pl.pallas_call(
        _proj_kernel,
        grid=(n // tm,),
        in_specs=[
            pl.BlockSpec((tm, d), row),
            pl.BlockSpec((1, d), const),
            pl.BlockSpec(w_main.shape, const),
            pl.BlockSpec(w_fl.shape, const),
            pl.BlockSpec((1, LANES), const),
            pl.BlockSpec(ones_bd.shape, const),
            pl.BlockSpec(gains.shape, const),
        ],
        out_specs=out_specs,
        out_shape=out_shape,
        compiler_params=_cparams(("arbitrary",)),
        name="in_projection",
    )(x2d, g, w_main, w_fl, b_fl, ones_bd, gains)


def _scan_kernel(x_ref, tri_ref, o_ref, carry_sc):
    @pl.when(pl.program_id(1) == 0)
    def _():
        carry_sc[...] = jnp.zeros_like(carry_sc)

    x = x_ref[0]
    h = x.shape[0]
    x3 = jnp.concatenate(_split3(x), axis=0)
    y3 = jnp.dot(x3, tri_ref[...], preferred_element_type=F32)
    y = y3[0:h] + y3[h:2 * h] + y3[2 * h:3 * h] + carry_sc[:, 0:1]
    o_ref[0] = y
    carry_sc[...] = jnp.broadcast_to(y[:, -1:], carry_sc.shape)


def _prefix_sum_time(xt, tl):
    b, h, t = xt.shape
    tl = min(tl, t)
    assert t % tl == 0
    tri = (jnp.arange(tl)[:, None] <= jnp.arange(tl)[None, :]).astype(BF16)
    return pl.pallas_call(
        _scan_kernel,
        grid=(b, t // tl),
        in_specs=[pl.BlockSpec((1, h, tl), lambda i, j: (i, 0, j)),
                  pl.BlockSpec((tl, tl), lambda i, j: (0, 0))],
        out_specs=pl.BlockSpec((1, h, tl), lambda i, j: (i, 0, j)),
        out_shape=jax.ShapeDtypeStruct((b, h, t), F32),
        scratch_shapes=[pltpu.VMEM((h, LANES), F32)],
        compiler_params=_cparams(("arbitrary", "arbitrary")),
        name="logf_prefix_sum",
    )(xt, tri)


def _lane_tile(x, n):
    return x if n == 1 else jnp.concatenate([x] * n, axis=1)


def _online_softmax(s, m_ref, l_ref, idx):
    tk = s.shape[1]
    m_prev = m_ref[idx]
    m_next = jnp.maximum(m_prev, jnp.max(s, axis=1, keepdims=True))
    p = jnp.exp(s - _lane_tile(m_next, tk // LANES))
    alpha = jnp.exp(m_prev - m_next)
    l_ref[idx] = alpha * l_ref[idx] + jnp.sum(p, axis=1, keepdims=True)
    m_ref[idx] = m_next
    return p, alpha


def _half_mask(shape_lanes):
    lane = lax.broadcasted_iota(jnp.int32, (1, shape_lanes), 1)
    return lane < HEAD_DIM


_NT = (((1,), (1,)), ((), ()))


def _fox_kernel(q_ref, k_ref, v_ref, cq_ref, ckt_ref, o_ref, m_sc, l_sc, acc_sc):
    i = pl.program_id(1)
    j = pl.program_id(2)
    tq = q_ref.shape[1]
    tk = k_ref.shape[1]

    @pl.when(j == 0)
    def _():
        m_sc[...] = jnp.full_like(m_sc, -jnp.inf)
        l_sc[...] = jnp.zeros_like(l_sc)
        acc_sc[...] = jnp.zeros_like(acc_sc)

    first_half = _half_mask(LANES)

    def step(masked):
        if masked:
            row = lax.broadcasted_iota(jnp.int32, (tq, tk), 0)
            col = lax.broadcasted_iota(jnp.int32, (tq, tk), 1)
            keep = col <= row
        for hp in range(N_FOX_HEADS // 2):
            cols = slice(hp * LANES, (hp + 1) * LANES)
            q2 = q_ref[0, :, cols]
            k2 = k_ref[0, :, cols].astype(BF16)
            v2 = v_ref[0, :, cols].astype(BF16)
            pv = []
            alphas = []
            for sub in range(2):
                h = 2 * hp + sub
                half = first_half if sub == 0 else jnp.logical_not(first_half)
                qm = jnp.where(half, q2, jnp.zeros_like(q2))
                s = lax.dot_general(qm, k2, _NT, preferred_element_type=F32)
                s = s + cq_ref[0, :, h:h + 1] - ckt_ref[0, h:h + 1, :]
                if masked:
                    s = jnp.where(keep, s, -jnp.inf)
                p, alpha = _online_softmax(s, m_sc, l_sc, h)
                pv.append(jnp.dot(p.astype(BF16), v2, preferred_element_type=F32))
                alphas.append(alpha)
            acc_sc[hp] = (jnp.where(first_half, alphas[0], alphas[1]) * acc_sc[hp]
                          + jnp.where(first_half, pv[0], pv[1]))

    @pl.when(j < i)
    def _():
        step(False)

    @pl.when(j == i)
    def _():
        step(True)
        for hp in range(N_FOX_HEADS // 2):
            l2 = jnp.where(first_half, l_sc[2 * hp], l_sc[2 * hp + 1])
            o_ref[0, :, hp * LANES:(hp + 1) * LANES] = (acc_sc[hp] / l2).astype(o_ref.dtype)


def _fox_attention(q, k, v, c, ct, tq):
    b, t, w = q.shape
    tq = min(tq, t)
    assert t % tq == 0 and tq % LANES == 0
    nq = t // tq
    qmap = lambda bi, i, j: (bi, i, 0)
    kmap = lambda bi, i, j: (bi, jnp.minimum(j, i), 0)
    return pl.pallas_call(
        _fox_kernel,
        grid=(b, nq, nq),
        in_specs=[
            pl.BlockSpec((1, tq, w), qmap),
            pl.BlockSpec((1, tq, w), kmap),
            pl.BlockSpec((1, tq, w), kmap),
            pl.BlockSpec((1, tq, N_FOX_HEADS), qmap),
            pl.BlockSpec((1, N_FOX_HEADS, tq), lambda bi, i, j: (bi, 0, jnp.minimum(j, i))),
        ],
        out_specs=pl.BlockSpec((1, tq, w), qmap),
        out_shape=jax.ShapeDtypeStruct((b, t, w), BF16),
        scratch_shapes=[pltpu.VMEM((N_FOX_HEADS, tq, LANES), F32),
                        pltpu.VMEM((N_FOX_HEADS, tq, LANES), F32),
                        pltpu.VMEM((N_FOX_HEADS // 2, tq, LANES), F32)],
        compiler_params=_cparams(("arbitrary", "arbitrary", "arbitrary")),
        name="fox_attention",
    )(q, k, v, c, ct)


def _diff_finalize(acc1, l1, acc2, l2, lam, g_out, out_scale):
    o = acc1 / l1 - lam * (acc2 / l2)
    ms = jnp.mean(o * o, axis=-1, keepdims=True)
    return o * lax.rsqrt(ms + RMS_EPS) * g_out * out_scale


def _diff_kernel(lam_ref, q_ref, k_ref, v_ref, bias_ref, go_ref, o_ref, m_sc, l_sc, acc_sc, *, out_scale):
    i = pl.program_id(1)
    j = pl.program_id(2)

    @pl.when(j == 0)
    def _():
        m_sc[...] = jnp.full_like(m_sc, -jnp.inf)
        l_sc[...] = jnp.zeros_like(l_sc)
        acc_sc[...] = jnp.zeros_like(acc_sc)

    first_half = _half_mask(LANES)

    def step(bias_idx):
        for h in range(N_DIFF_HEADS):
            cols = slice(h * LANES, (h + 1) * LANES)
            q2 = q_ref[0, :, cols]
            k2 = k_ref[0, :, cols].astype(BF16)
            v2 = v_ref[0, :, cols].astype(BF16)
            for c in range(2):
                idx = 2 * h + c
                half = first_half if c == 0 else jnp.logical_not(first_half)
                qm = jnp.where(half, q2, jnp.zeros_like(q2))
                s = lax.dot_general(qm, k2, _NT, preferred_element_type=F32)
                if bias_idx is not None:
                    s = s + bias_ref[bias_idx, h]
                p, alpha = _online_softmax(s, m_sc, l_sc, idx)
                acc_sc[idx] = alpha * acc_sc[idx] + jnp.dot(p.astype(BF16), v2, preferred_element_type=F32)

    @pl.when(j < i - 1)
    def _():
        step(None)

    @pl.when(j == i - 1)
    def _():
        step(1)

    @pl.when(j == i)
    def _():
        step(0)
        lam = lam_ref[0]
        for h in range(N_DIFF_HEADS):
            o = _diff_finalize(acc_sc[2 * h], l_sc[2 * h], acc_sc[2 * h + 1], l_sc[2 * h + 1],
                               lam, go_ref[...], out_scale)
            o_ref[0, :, h * LANES:(h + 1) * LANES] = o.astype(o_ref.dtype)


def _t5_bucket_1d(n):
    max_exact = N_BUCKETS // 2
    nf = jnp.maximum(n, 1).astype(F32)
    large = max_exact + (jnp.log(nf / max_exact) / math.log(MAX_DISTANCE / max_exact)
                         * (N_BUCKETS - max_exact)).astype(jnp.int32)
    large = jnp.minimum(large, N_BUCKETS - 1)
    return jnp.where(n < max_exact, n, large)


def _rel_bias_tile(rel_table, dist, valid):
    rel = rel_table.astype(F32) - rel_table[N_BUCKETS - 1].astype(F32)[None, :]
    tile = jnp.moveaxis(rel[_t5_bucket_1d(jnp.maximum(dist, 0))], -1, 0)
    return jnp.where(valid[None], tile, -jnp.inf)


def _diff_attention(q, k, v, rel_table, lam, g_out, out_scale, tq):
    b, t, w = q.shape
    tq = min(tq, t)
    assert t % tq == 0 and tq >= MAX_DISTANCE and tq % LANES == 0
    nq = t // tq
    r = jnp.arange(tq, dtype=jnp.int32)
    d0 = r[:, None] - r[None, :]
    bias = jnp.stack([_rel_bias_tile(rel_table, d0, d0 >= 0),
                      _rel_bias_tile(rel_table, d0 + tq, jnp.ones_like(d0, dtype=bool))])
    qmap = lambda bi, i, j: (bi, i, 0)
    kmap = lambda bi, i, j: (bi, jnp.minimum(j, i), 0)
    return pl.pallas_call(
        functools.partial(_diff_kernel, out_scale=out_scale),
        grid=(b, nq, nq),
        in_specs=[
            pl.BlockSpec(memory_space=pltpu.SMEM),
            pl.BlockSpec((1, tq, w), qmap),
            pl.BlockSpec((1, tq, w), kmap),
            pl.BlockSpec((1, tq, w), kmap),
            pl.BlockSpec(bias.shape, lambda bi, i, j: (0, 0, 0, 0)),
            pl.BlockSpec((1, LANES), lambda bi, i, j: (0, 0)),
        ],
        out_specs=pl.BlockSpec((1, tq, w), qmap),
        out_shape=jax.ShapeDtypeStruct((b, t, w), BF16),
        scratch_shapes=[pltpu.VMEM((2 * N_DIFF_HEADS, tq, LANES), F32),
                        pltpu.VMEM((2 * N_DIFF_HEADS, tq, LANES), F32),
                        pltpu.VMEM((2 * N_DIFF_HEADS, tq, LANES), F32)],
        compiler_params=_cparams(("arbitrary", "arbitrary", "arbitrary")),
        name="diff_attention",
    )(lam, q, k, v, bias, g_out)


def _out_kernel(x_ref, of_ref, od_ref, gf_ref, gd_ref, wf_ref, wd_ref, wo_ref, g2_ref, wr_ref, br_ref,
                y_ref, h2_ref, lg_ref):
    merged = (gf_ref[...] * jnp.dot(of_ref[...], wf_ref[...], preferred_element_type=F32)
              + gd_ref[...] * jnp.dot(od_ref[...], wd_ref[...], preferred_element_type=F32))
    y = x_ref[...] + jnp.dot(merged.astype(BF16), wo_ref[...], preferred_element_type=F32)
    y_ref[...] = y
    ms = jnp.mean(y * y, axis=-1, keepdims=True)
    h2 = y * lax.rsqrt(ms + RMS_EPS) * g2_ref[...]
    h2_ref[...] = h2
    lg_ref[...] = jnp.dot(h2.astype(BF16), wr_ref[...], preferred_element_type=F32) + br_ref[...]


def _out_projection(x2d, o_f, o_d, gate_f, gate_d, w_fu, w_du, w_o, g2, w_r, b_r, tm):
    n, d = x2d.shape
    tm = min(tm, n)
    assert n % tm == 0
    row = lambda i: (i, 0)
    const = lambda i: (0, 0)
    return pl.pallas_call(
        _out_kernel,
        grid=(n // tm,),
        in_specs=[
            pl.BlockSpec((tm, d), row), pl.BlockSpec((tm, o_f.shape[1]), row),
            pl.BlockSpec((tm, o_d.shape[1]), row), pl.BlockSpec((tm, d), row), pl.BlockSpec((tm, d), row),
            pl.BlockSpec(w_fu.shape, const), pl.BlockSpec(w_du.shape, const), pl.BlockSpec(w_o.shape, const),
            pl.BlockSpec((1, d), const), pl.BlockSpec(w_r.shape, const), pl.BlockSpec((1, LANES), const),
        ],
        out_specs=(pl.BlockSpec((tm, d), row), pl.BlockSpec((tm, d), row), pl.BlockSpec((tm, LANES), row)),
        out_shape=(jax.ShapeDtypeStruct((n, d), F32), jax.ShapeDtypeStruct((n, d), F32),
                   jax.ShapeDtypeStruct((n, LANES), F32)),
        compiler_params=_cparams(("arbitrary",)),
        name="out_projection",
    )(x2d, o_f, o_d, gate_f, gate_d, w_fu, w_du, w_o, g2, w_r, b_r)


def _deinterleave_kernel(w_ref, perm_ref, g_ref, u_ref):
    w = w_ref[0].astype(BF16)
    r = jnp.dot(w, perm_ref[...], preferred_element_type=F32).astype(BF16)
    half = r.shape[1] // 2
    g_ref[0] = r[:, :half]
    u_ref[0] = r[:, half:]


def _deinterleave_gate_up(w_gu, tn):
    e, d, f2 = w_gu.shape
    assert f2 % tn == 0
    half = tn // 2
    col = jnp.arange(tn)
    perm = (jnp.where(col % 2 == 0, col // 2, half + col // 2)[:, None] == col[None, :]).astype(BF16)
    out = jax.ShapeDtypeStruct((e, d, f2 // 2), BF16)
    return pl.pallas_call(
        _deinterleave_kernel,
        grid=(e, f2 // tn),
        in_specs=[pl.BlockSpec((1, d, tn), lambda i, j: (i, 0, j)),
                  pl.BlockSpec((tn, tn), lambda i, j: (0, 0))],
        out_specs=(pl.BlockSpec((1, d, half), lambda i, j: (i, 0, j)),
                   pl.BlockSpec((1, d, half), lambda i, j: (i, 0, j))),
        out_shape=(out, out),
        compiler_params=_cparams(("arbitrary", "arbitrary")),
        name="deinterleave_gate_up",
    )(w_gu, perm)


def _route_kernel(lg_ref, tril_ref, idx_ref, gate_ref, rank_ref, cnt_ref, carry_sc):
    @pl.when(pl.program_id(0) == 0)
    def _():
        carry_sc[...] = jnp.zeros_like(carry_sc)

    x = lg_ref[...]
    lane = lax.broadcasted_iota(jnp.int32, x.shape, 1)
    lane_f = lane.astype(F32)
    x = jnp.where(lane < N_EXPERTS, x, -jnp.inf)
    carry = carry_sc[0:1, :]
    vals, idxs, ranks = [], [], []
    for _ in range(TOP_K):
        m = jnp.max(x, axis=1, keepdims=True)
        idx = jnp.min(jnp.where(x == m, lane_f, float(LANES)), axis=1, keepdims=True)
        onehot = lane_f == idx
        oh = jnp.where(onehot, 1.0, 0.0)
        before = jnp.dot(tril_ref[...], oh.astype(BF16), preferred_element_type=F32)
        ranks.append(jnp.sum(jnp.where(onehot, before + carry, 0.0), axis=1, keepdims=True))
        carry = carry + jnp.sum(oh, axis=0, keepdims=True)
        x = jnp.where(onehot, -jnp.inf, x)
        vals.append(m)
        idxs.append(idx)
    carry_sc[0:1, :] = carry
    cnt_ref[...] = jnp.broadcast_to(carry, cnt_ref.shape)
    e = [jnp.exp(v - vals[0]) for v in vals]
    tot = e[0]
    for ek in e[1:]:
        tot = tot + ek
    gates = jnp.zeros(x.shape, F32)
    idx_out = jnp.zeros(x.shape, F32)
    rank_out = jnp.zeros(x.shape, F32)
    for k in range(TOP_K):
        gates = jnp.where(lane == k, e[k] / tot, gates)
        idx_out = jnp.where(lane == k, idxs[k], idx_out)
        rank_out = jnp.where(lane == k, ranks[k], rank_out)
    gate_ref[...] = gates
    idx_ref[...] = idx_out.astype(jnp.int32)
    rank_ref[...] = rank_out.astype(jnp.int32)


def _route(logits, tm):
    n = logits.shape[0]
    tm = min(tm, n)
    assert n % tm == 0
    tril = (jnp.arange(tm)[None, :] < jnp.arange(tm)[:, None]).astype(BF16)
    row = lambda i: (i, 0)
    idx, gates, rank, cnt = pl.pallas_call(
        _route_kernel,
        grid=(n // tm,),
        in_specs=[pl.BlockSpec((tm, LANES), row), pl.BlockSpec((tm, tm), lambda i: (0, 0))],
        out_specs=(pl.BlockSpec((tm, LANES), row), pl.BlockSpec((tm, LANES), row),
                   pl.BlockSpec((tm, LANES), row), pl.BlockSpec((8, LANES), lambda i: (0, 0))),
        out_shape=(jax.ShapeDtypeStruct((n, LANES), jnp.int32), jax.ShapeDtypeStruct((n, LANES), F32),
                   jax.ShapeDtypeStruct((n, LANES), jnp.int32), jax.ShapeDtypeStruct((8, LANES), F32)),
        scratch_shapes=[pltpu.VMEM((8, LANES), F32)],
        compiler_params=_cparams(("arbitrary",)),
        name="moe_route",
    )(logits, tril)
    return (idx[:, :TOP_K], gates[:, :TOP_K], rank[:, :TOP_K],
            cnt[0, :N_EXPERTS].astype(jnp.int32))


def _dispatch_kernel(slot_ref, h_ref, xs_in, xs_out, sem):
    del xs_in
    rows = h_ref.shape[0]

    def body(r, carry):
        for k in range(TOP_K):
            s = slot_ref[0, 0, r * TOP_K + k]
            pltpu.make_async_copy(h_ref.at[pl.ds(r, 1)], xs_out.at[pl.ds(s, 1)], sem).start()
        return carry

    lax.fori_loop(0, rows, body, 0, unroll=2)
    for _ in range(TOP_K):
        pltpu.make_async_copy(h_ref, xs_out.at[pl.ds(0, rows)], sem).wait()


def _dispatch(h2, slot, n_slots, tm):
    n, d = h2.shape
    tm = min(tm, n)
    assert n % tm == 0
    slot3 = slot.reshape(n // tm, 1, tm * TOP_K)
    return pl.pallas_call(
        _dispatch_kernel,
        grid=(n // tm,),
        in_specs=[pl.BlockSpec((1, 1, tm * TOP_K), lambda i: (i, 0, 0), memory_space=pltpu.SMEM),
                  pl.BlockSpec((tm, d), lambda i: (i, 0)),
                  pl.BlockSpec(memory_space=pl.ANY)],
        out_specs=pl.BlockSpec(memory_space=pl.ANY),
        out_shape=jax.ShapeDtypeStruct((n_slots, d), F32),
        scratch_shapes=[pltpu.SemaphoreType.DMA(())],
        input_output_aliases={2: 0},
        compiler_params=_cparams(("arbitrary",)),
        name="moe_dispatch",
    )(slot3, h2, jnp.zeros((n_slots, d), F32))


def _row_gather_start(src_hbm, idx_ref, n_rows, dst, sem):
    def body(r, carry):
        tok = idx_ref[0, 0, r]
        pltpu.make_async_copy(src_hbm.at[pl.ds(tok, 1)], dst.at[pl.ds(r, 1)], sem).start()
        return carry
    lax.fori_loop(0, n_rows, body, 0, unroll=8)


def _row_gather_wait(src_hbm, n_rows, dst, sem):
    pltpu.make_async_copy(src_hbm.at[pl.ds(0, n_rows)], dst, sem).wait()


def _moe_kernel(be_ref, nused_ref, x_ref, wg_ref, wu_ref, bg_ref, bu_ref, wd_ref, bd_ref, y_ref):
    i = pl.program_id(0)
    n_used = nused_ref[0]

    @pl.when(i < n_used)
    def _():
        x = x_ref[...].astype(BF16)
        gate = jnp.dot(x, wg_ref[0], preferred_element_type=F32) + bg_ref[0]
        up = jnp.dot(x, wu_ref[0], preferred_element_type=F32) + bu_ref[0]
        gate = jnp.minimum(gate, SWIGLU_LIMIT)
        up = jnp.clip(up, -SWIGLU_LIMIT, SWIGLU_LIMIT)
        glu = gate * jax.nn.sigmoid(SWIGLU_ALPHA * gate)
        act = ((up + 1.0) * glu).astype(BF16)
        y_ref[...] = jnp.dot(act, wd_ref[0], preferred_element_type=F32) + bd_ref[0]

    @pl.when(i >= n_used)
    def _():
        y_ref[...] = jnp.zeros_like(y_ref)


def _expert_ffn(xs, block_expert, n_used, w_g, w_u, b_g, b_u, w_d, b_d, bm):
    n_slots, d = xs.shape
    n_blocks = block_expert.shape[0]
    dff = w_g.shape[2]
    wmap = lambda i, be, nu: (be[i], 0, 0)
    grid_spec = pltpu.PrefetchScalarGridSpec(
        num_scalar_prefetch=2,
        grid=(n_blocks,),
        in_specs=[
            pl.BlockSpec((bm, d), lambda i, be, nu: (i, 0)),
            pl.BlockSpec((1, d, dff), wmap), pl.BlockSpec((1, d, dff), wmap),
            pl.BlockSpec((1, 1, dff), wmap), pl.BlockSpec((1, 1, dff), wmap),
            pl.BlockSpec((1, dff, d), wmap), pl.BlockSpec((1, 1, d), wmap),
        ],
        out_specs=pl.BlockSpec((bm, d), lambda i, be, nu: (i, 0)),
    )
    return pl.pallas_call(
        _moe_kernel,
        grid_spec=grid_spec,
        out_shape=jax.ShapeDtypeStruct((n_slots, d), F32),
        compiler_params=_cparams(("arbitrary",)),
        name="expert_ffn",
    )(block_expert, n_used, xs, w_g, w_u, b_g, b_u, w_d, b_d)


def _combine_kernel(idx_ref, idx_next_ref, yb_hbm, resid_ref, gates_ref, o_ref, buf, sem):
    i = pl.program_id(0)
    n = pl.num_programs(0)
    rows = buf.shape[1]
    tc = resid_ref.shape[0]
    slot = lax.rem(i, 2)

    @pl.when(i == 0)
    def _():
        _row_gather_start(yb_hbm, idx_ref, rows, buf.at[0], sem.at[0])

    @pl.when(i + 1 < n)
    def _():
        _row_gather_start(yb_hbm, idx_next_ref, rows, buf.at[1 - slot], sem.at[1 - slot])

    _row_gather_wait(yb_hbm, rows, buf.at[slot], sem.at[slot])
    acc = jnp.zeros(o_ref.shape, F32)
    for k in range(TOP_K):
        acc = acc + buf[slot, k * tc:(k + 1) * tc, :] * gates_ref[:, k:k + 1]
    o_ref[...] = resid_ref[...] + acc


def _combine(yb, slot_km, gates, resid, tc):
    n, d = resid.shape
    n_tiles = n // tc
    rows = TOP_K * tc
    return pl.pallas_call(
        _combine_kernel,
        grid=(n_tiles,),
        in_specs=[
            pl.BlockSpec((1, 1, rows), lambda i: (i, 0, 0), memory_space=pltpu.SMEM),
            pl.BlockSpec((1, 1, rows), lambda i: (jnp.minimum(i + 1, n_tiles - 1), 0, 0),
                         memory_space=pltpu.SMEM),
            pl.BlockSpec(memory_space=pl.ANY),
            pl.BlockSpec((tc, d), lambda i: (i, 0)),
            pl.BlockSpec((tc, TOP_K), lambda i: (i, 0)),
        ],
        out_specs=pl.BlockSpec((tc, d), lambda i: (i, 0)),
        out_shape=jax.ShapeDtypeStruct((n, d), F32),
        scratch_shapes=[pltpu.VMEM((2, rows, d), F32), pltpu.SemaphoreType.DMA((2,))],
        compiler_params=_cparams(("arbitrary",)),
        name="moe_combine",
    )(slot_km, slot_km, yb, resid, gates)


def _moe_layer(y1, h2, logits, moe_w, bm, tc):
    n, d = y1.shape
    w_g, w_u, b_g, b_u, w_d, b_d = moe_w
    bm = min(bm, n)
    tc = min(tc, n)
    idx, gates, rank, counts = _route(logits, ROUTE_ROWS)
    padded = (counts + bm - 1) // bm * bm
    pend = jnp.cumsum(padded)
    pstart = pend - padded
    experts = jnp.arange(N_EXPERTS, dtype=jnp.int32)
    slot = rank + jnp.sum(jnp.where(idx[..., None] == experts, pstart, 0), axis=-1).astype(jnp.int32)
    n_blocks = -(-(n * TOP_K) // bm) + N_EXPERTS
    block_expert = jnp.minimum(
        jnp.searchsorted(pend, jnp.arange(n_blocks) * bm, side='right'), N_EXPERTS - 1).astype(jnp.int32)
    n_used = (pend[-1] // bm).astype(jnp.int32).reshape(1)
    xs = _dispatch(h2, slot, n_blocks * bm, DISPATCH_ROWS)
    yb = _expert_ffn(xs, block_expert, n_used, w_g, w_u, b_g, b_u, w_d, b_d, bm)
    slot_km = slot.reshape(n // tc, tc, TOP_K).transpose(0, 2, 1).reshape(n // tc, 1, TOP_K * tc)
    return _combine(yb, slot_km, gates, y1, tc)


def _sample_attn_kernel(pt_ref, lam_ref, qsf_ref, qsd_ref, cncol_ref, cnrow_ref, snew_ref, blast_ref,
                        knf_ref, vnf_ref, knd_ref, vnd_ref, tri_ref, go_ref, *rest, pps, out_scale):
    page_refs = rest[:5 * pps]
    of_ref, od_ref, m_sc, l_sc, accf_sc, accd_sc, carry_sc = rest[5 * pps:]
    kf_refs = page_refs[0 * pps:1 * pps]
    vf_refs = page_refs[1 * pps:2 * pps]
    kd_refs = page_refs[2 * pps:3 * pps]
    vd_refs = page_refs[3 * pps:4 * pps]
    lft_refs = page_refs[4 * pps:5 * pps]
    g = pl.program_id(1)
    n_g = pl.num_programs(1)
    nh = N_FOX_HEADS
    nd_heads = N_DIFF_HEADS
    half_rows = cncol_ref.shape[1] // 2
    tok = half_rows // nh

    @pl.when(g == 0)
    def _():
        m_sc[...] = jnp.full_like(m_sc, -jnp.inf)
        l_sc[...] = jnp.zeros_like(l_sc)
        accf_sc[...] = jnp.zeros_like(accf_sc)
        accd_sc[...] = jnp.zeros_like(accd_sc)
        carry_sc[...] = jnp.zeros_like(carry_sc)

    _NN = (((1,), (0,)), ((), ()))

    def scores(kf_heads, kd_heads, fox_dims):
        rows = [lax.dot_general(qsf_ref[0, h], kf_heads[h], fox_dims, preferred_element_type=F32)
                for h in range(nh)]
        rows += [lax.dot_general(qsd_ref[0, h], kd_heads[h], _NT, preferred_element_type=F32)
                 for h in range(nd_heads)]
        return jnp.concatenate(rows, axis=0)

    def update(s, vf_heads, vd_heads, fox_dims):
        p, alpha = _online_softmax(s, m_sc, l_sc, slice(None))
        pv_f = jnp.concatenate(
            [lax.dot_general(p[h * tok:(h + 1) * tok].astype(BF16), vf_heads[h], fox_dims,
                             preferred_element_type=F32) for h in range(nh)], axis=0)
        pv_d = jnp.concatenate(
            [jnp.dot(p[half_rows + 2 * h * tok:half_rows + 2 * (h + 1) * tok].astype(BF16), vd_heads[h],
                     preferred_element_type=F32) for h in range(nd_heads)], axis=0)
        accf_sc[...] = alpha[:half_rows, :HEAD_DIM] * accf_sc[...] + pv_f
        accd_sc[...] = alpha[half_rows:] * accd_sc[...] + pv_d

    def fox_page_heads(refs):
        return [jnp.concatenate([ref[0, 0, h] for ref in refs], axis=1).astype(BF16) for h in range(nh)]

    def diff_page_heads(refs):
        page = refs[0].shape[2] // nd_heads
        return [jnp.concatenate([ref[0, 0, pl.ds(h, page, stride=nd_heads), :] for ref in refs],
                                axis=0).astype(BF16) for h in range(nd_heads)]

    def past_chunk(with_bias):
        carry = carry_sc[...]
        suf = [None] * pps
        for p in reversed(range(pps)):
            x = lft_refs[p][0]
            x3 = jnp.concatenate(_split3(x), axis=0)
            y3 = jnp.dot(x3, tri_ref[...], preferred_element_type=F32)
            within = y3[0:nh] + y3[nh:2 * nh] + y3[2 * nh:3 * nh]
            suf[p] = within + carry
            carry = carry + (within[:, 0:1] + x[:, 0:1])
        carry_sc[...] = carry
        sufc = jnp.concatenate(suf, axis=1)
        r = sufc.shape[1]
        suf_rows = jnp.concatenate(
            [jnp.broadcast_to(sufc[h:h + 1, :], (tok, r)) for h in range(nh)]
            + [jnp.zeros((half_rows, r), F32)], axis=0)
        s = scores(fox_page_heads(kf_refs), diff_page_heads(kd_refs), _NN)
        s = s + _lane_tile(cncol_ref[0], r // LANES) + suf_rows
        if with_bias:
            s = s + blast_ref[...]
        update(s, fox_page_heads(vf_refs), diff_page_heads(vd_refs), _NT)

    @pl.when(g == 0)
    def _():
        past_chunk(True)

    @pl.when(g > 0)
    def _():
        past_chunk(False)

    @pl.when(g == n_g - 1)
    def _():
        def new_heads(ref, n_heads):
            width = ref.shape[2] // n_heads
            return [ref[0, :, h * width:(h + 1) * width].astype(BF16) for h in range(n_heads)]

        s = scores(new_heads(knf_ref, nh), new_heads(knd_ref, nd_heads), _NT)
        s = s + (cncol_ref[0] - cnrow_ref[0]) + snew_ref[...]
        update(s, new_heads(vnf_ref, nh), new_heads(vnd_ref, nd_heads), _NN)

        l = l_sc[...]
        nf = accf_sc[...] / l[:half_rows, :HEAD_DIM]
        for h in range(nh):
            of_ref[0, :, h * HEAD_DIM:(h + 1) * HEAD_DIM] = nf[h * tok:(h + 1) * tok].astype(of_ref.dtype)

        nd = accd_sc[...] / l[half_rows:]
        lam = lam_ref[0]
        for h in range(N_DIFF_HEADS):
            cols = slice(h * LANES, (h + 1) * LANES)
            a1 = nd[(2 * h) * tok:(2 * h + 1) * tok]
            a2 = nd[(2 * h + 1) * tok:(2 * h + 2) * tok]
            o = a1 - lam * a2
            ms = jnp.mean(o * o, axis=-1, keepdims=True)
            od_ref[0, :, cols] = (o * lax.rsqrt(ms + RMS_EPS) * go_ref[...] * out_scale).astype(od_ref.dtype)


def _sample_attention(qf, qd, kf_new, vf_new, kd_new, vd_new, c_new, caches, page_table, rel_table, lam,
                      g_out, out_scale, pps, layer):
    db, s_new, w = qf.shape
    ck_f, cv_f, ck_d, cv_d, clf_t = caches
    page = ck_f.shape[2]
    ck_f, cv_f = (jnp.transpose(a, (0, 1, 3, 4, 2)) for a in (ck_f, cv_f))
    ck_d, cv_d = (a.reshape(a.shape[0], a.shape[1], page * N_DIFF_HEADS, 2 * HEAD_DIM) for a in (ck_d, cv_d))
    n_pages = page_table.shape[1]
    pps = min(pps, n_pages)
    assert n_pages % pps == 0 and page == LANES and s_new * N_FOX_HEADS * 2 == LANES
    n_g = n_pages // pps
    r = pps * page
    nh = N_FOX_HEADS
    half_rows = nh * s_new

    qsf = jnp.transpose(qf.reshape(db, s_new, nh, HEAD_DIM), (0, 2, 1, 3))
    qd_h = jnp.transpose(qd.reshape(db, s_new, N_DIFF_HEADS, 2 * HEAD_DIM), (0, 2, 1, 3))
    first = (jnp.arange(2 * HEAD_DIM) < HEAD_DIM)
    qsd = jnp.concatenate([jnp.where(first, qd_h, jnp.zeros((), qd.dtype)),
                           jnp.where(first, jnp.zeros((), qd.dtype), qd_h)], axis=2)

    c_rows = jnp.transpose(c_new, (0, 2, 1)).reshape(db, half_rows)
    cn_col = jnp.concatenate([c_rows, jnp.zeros_like(c_rows)], axis=1)[:, :, None]
    cn_col = jnp.broadcast_to(cn_col, (db, LANES, LANES))
    c_keys = jnp.repeat(jnp.transpose(c_new, (0, 2, 1)), s_new, axis=1)
    cn_row = jnp.zeros((db, LANES, LANES), F32).at[:, :half_rows, :s_new].set(c_keys)

    t_of_row = jnp.arange(LANES, dtype=jnp.int32) % s_new
    key = jnp.arange(LANES, dtype=jnp.int32)
    valid_new = (key[None, :] <= t_of_row[:, None]) & (key[None, :] < s_new)
    is_diff = (jnp.arange(LANES) >= half_rows)
    head_of_row = jnp.clip((jnp.arange(LANES) - half_rows) // (2 * s_new), 0, N_DIFF_HEADS - 1)
    rel = rel_table.astype(F32) - rel_table[N_BUCKETS - 1].astype(F32)[None, :]

    def rel_rows(dist):
        vals = rel[_t5_bucket_1d(jnp.maximum(dist, 0)), head_of_row[:, None]]
        return jnp.where(is_diff[:, None], vals, 0.0)

    s_new_tile = jnp.where(valid_new, rel_rows(t_of_row[:, None] - key[None, :]), -jnp.inf)
    key_r = jnp.arange(r, dtype=jnp.int32)
    b_last = rel_rows(r - key_r[None, :] + t_of_row[:, None])

    def pad_rows(x):
        return jnp.pad(x, ((0, 0), (0, LANES - s_new), (0, 0)))

    tri = (jnp.arange(page)[:, None] > jnp.arange(page)[None, :]).astype(BF16)

    bmap = lambda b, g, pt: (b, 0, 0)
    bmap4 = lambda b, g, pt: (b, 0, 0, 0)
    const2 = lambda b, g, pt: (0, 0)

    def page_map(p):
        return lambda b, g, pt: (pt[b, (n_g - 1 - g) * pps + p], 0, 0)

    def cache_map(p, ndim):
        return lambda b, g, pt: (layer, pt[b, (n_g - 1 - g) * pps + p]) + (0,) * (ndim - 2)

    in_specs = [
        pl.BlockSpec(memory_space=pltpu.SMEM),
        pl.BlockSpec((1,) + qsf.shape[1:], bmap4), pl.BlockSpec((1,) + qsd.shape[1:], bmap4),
        pl.BlockSpec((1, LANES, LANES), bmap), pl.BlockSpec((1, LANES, LANES), bmap),
        pl.BlockSpec((LANES, LANES), const2), pl.BlockSpec((LANES, r), const2),
        pl.BlockSpec((1, LANES, w), bmap), pl.BlockSpec((1, LANES, w), bmap),
        pl.BlockSpec((1, LANES, w), bmap), pl.BlockSpec((1, LANES, w), bmap),
        pl.BlockSpec((page, page), const2), pl.BlockSpec((1, LANES), const2),
    ]
    operands = [lam, qsf, qsd, cn_col, cn_row, s_new_tile, b_last,
                pad_rows(kf_new), pad_rows(vf_new), pad_rows(kd_new), pad_rows(vd_new), tri, g_out]
    for arr in (ck_f, cv_f, ck_d, cv_d):
        for p in range(pps):
            in_specs.append(pl.BlockSpec((1, 1) + arr.shape[2:], cache_map(p, arr.ndim)))
            operands.append(arr)
    for p in range(pps):
        in_specs.append(pl.BlockSpec((1, nh, page), page_map(p)))
        operands.append(clf_t)

    grid_spec = pltpu.PrefetchScalarGridSpec(
        num_scalar_prefetch=1,
        grid=(db, n_g),
        in_specs=in_specs,
        out_specs=(pl.BlockSpec((1, s_new, w), bmap), pl.BlockSpec((1, s_new, w), bmap)),
        scratch_shapes=[pltpu.VMEM((LANES, LANES), F32), pltpu.VMEM((LANES, LANES), F32),
                        pltpu.VMEM((half_rows, HEAD_DIM), F32), pltpu.VMEM((half_rows, 2 * HEAD_DIM), F32),
                        pltpu.VMEM((nh, LANES), F32)],
    )
    return pl.pallas_call(
        functools.partial(_sample_attn_kernel, pps=pps, out_scale=out_scale),
        grid_spec=grid_spec,
        out_shape=(jax.ShapeDtypeStruct((db, s_new, w), BF16), jax.ShapeDtypeStruct((db, s_new, w), BF16)),
        compiler_params=_cparams(("arbitrary", "arbitrary")),
        name="sample_attention",
    )(page_table, *operands)


PROJ_ROWS = 256
ATTN_BLOCK = 512
SCAN_BLOCK = 512
MOE_ROWS = 256
COMBINE_ROWS = 128
ROUTE_ROWS = 512
DISPATCH_ROWS = 512
DEINTERLEAVE_COLS = 512
PAGES_PER_STEP = 8


def _split_points(d_model):
    sizes = (FOX_WIDTH, FOX_WIDTH, FOX_WIDTH, N_FOX_HEADS, DIFF_WIDTH, DIFF_WIDTH, DIFF_WIDTH, d_model)
    pts, acc = [], 0
    for s in sizes:
        acc += s
        pts.append(acc)
    return pts


def _layer_weights(l, d_model, attn_norm_g, w_in, b_forget, fox_q_norm_g, fox_k_norm_g, diff_q_norm_g,
                   diff_k_norm_g, lambda_q1, lambda_k1, lambda_q2, lambda_k2, diff_out_norm_g, w_fox_up,
                   w_diff_up, w_out, ffn_norm_g, w_router, b_router, w_gate_up, b_gate_up, w_down, b_down):
    lambda_init = 0.8 - 0.6 * math.exp(-0.3 * l)
    wq_f, wk_f, wv_f, w_fl, wq_d, wk_d, wv_d, wg_f, wg_d = jnp.split(w_in[l], _split_points(d_model), axis=1)
    w_main = jnp.concatenate([wq_f, wk_f, wv_f, wq_d, wk_d, wv_d, wg_f, wg_d], axis=1).astype(BF16)
    w_fl = jnp.pad(w_fl, ((0, 0), (0, LANES - N_FOX_HEADS))).astype(BF16)
    b_fl = jnp.pad(b_forget[l].astype(F32), (0, LANES - N_FOX_HEADS)).reshape(1, LANES)
    grp = jnp.arange(FOX_WIDTH) // HEAD_DIM
    ones_bd = (grp[:, None] == grp[None, :]).astype(BF16)
    reps = FOX_WIDTH // HEAD_DIM
    gains = jnp.stack([jnp.tile(g[l].astype(F32), reps)
                       for g in (fox_q_norm_g, fox_k_norm_g, diff_q_norm_g, diff_k_norm_g)])
    lam = (jnp.exp(jnp.sum(lambda_q1[l].astype(F32) * lambda_k1[l].astype(F32)))
           - jnp.exp(jnp.sum(lambda_q2[l].astype(F32) * lambda_k2[l].astype(F32))) + lambda_init)
    w_r = jnp.pad(w_router[l], ((0, 0), (0, LANES - N_EXPERTS))).astype(BF16)
    b_r = jnp.pad(b_router[l].astype(F32), (0, LANES - N_EXPERTS)).reshape(1, LANES)
    w_g, w_u = _deinterleave_gate_up(w_gate_up[l], DEINTERLEAVE_COLS)
    bgu = b_gate_up[l].astype(F32)
    moe_w = (w_g, w_u, bgu[:, None, 0::2], bgu[:, None, 1::2],
             w_down[l].astype(BF16), b_down[l].astype(F32)[:, None, :])
    return dict(
        lambda_init=lambda_init, g1=attn_norm_g[l].astype(F32).reshape(1, d_model), w_main=w_main, w_fl=w_fl,
        b_fl=b_fl, ones_bd=ones_bd, gains=gains, lam=lam.reshape(1).astype(F32),
        g_out=diff_out_norm_g[l].astype(F32).reshape(1, LANES),
        w_fu=w_fox_up[l].astype(BF16), w_du=w_diff_up[l].astype(BF16), w_o=w_out[l].astype(BF16),
        g2=ffn_norm_g[l].astype(F32).reshape(1, d_model), w_r=w_r, b_r=b_r, moe_w=moe_w)


def _project(x, lw):
    b, t, d = x.shape
    outs = _in_projection(x.reshape(b * t, d), lw['g1'], lw['w_main'], lw['w_fl'], lw['b_fl'],
                          lw['ones_bd'], lw['gains'], PROJ_ROWS)
    qf, kf, vf, lf, qd, kd, vd, gf, gd = outs
    logf = lf[:, :N_FOX_HEADS].reshape(b, t, N_FOX_HEADS)
    r3 = lambda a: a.reshape(b, t, a.shape[-1])
    return r3(qf), r3(kf), r3(vf), logf, r3(qd), r3(kd), r3(vd), gf, gd


def _finish_layer(x, o_f, o_d, gf, gd, lw, moe_rows):
    b, t, d = x.shape
    n = b * t
    y1, h2, logits = _out_projection(x.reshape(n, d), o_f.reshape(n, -1), o_d.reshape(n, -1), gf, gd,
                                     lw['w_fu'], lw['w_du'], lw['w_o'], lw['g2'], lw['w_r'], lw['b_r'],
                                     PROJ_ROWS)
    return _moe_layer(y1, h2, logits, lw['moe_w'], moe_rows, COMBINE_ROWS).reshape(b, t, d)


def _new_rows(kf, vf, logf, kd, vd):
    b, t, _ = kf.shape
    return (kf.reshape(b, t, N_FOX_HEADS, HEAD_DIM), vf.reshape(b, t, N_FOX_HEADS, HEAD_DIM), logf,
            kd.reshape(b, t, N_DIFF_HEADS, 2 * HEAD_DIM), vd.reshape(b, t, N_DIFF_HEADS, 2 * HEAD_DIM))


def _prompt_layer(x, lw, rel_table):
    qf, kf, vf, logf, qd, kd, vd, gf, gd = _project(x, lw)
    ct = _prefix_sum_time(jnp.transpose(logf, (0, 2, 1)), SCAN_BLOCK)
    c = jnp.transpose(ct, (0, 2, 1))
    o_f = _fox_attention(qf, kf, vf, c, ct, ATTN_BLOCK)
    o_d = _diff_attention(qd, kd, vd, rel_table, lw['lam'], lw['g_out'], 1.0 - lw['lambda_init'], ATTN_BLOCK)
    return _finish_layer(x, o_f, o_d, gf, gd, lw, MOE_ROWS), _new_rows(kf, vf, logf, kd, vd)


def _sample_layer(x, lw, rel_table, caches, page_table, layer):
    qf, kf, vf, logf, qd, kd, vd, gf, gd = _project(x, lw)
    b, t, _ = logf.shape
    logf_t = jnp.pad(jnp.transpose(logf, (0, 2, 1)), ((0, 0), (0, 0), (0, LANES - t)))
    c = jnp.transpose(_prefix_sum_time(logf_t, LANES)[:, :, :t], (0, 2, 1))
    o_f, o_d = _sample_attention(qf, qd, kf, vf, kd, vd, c, caches, page_table, rel_table, lw['lam'],
                                 lw['g_out'], 1.0 - lw['lambda_init'], PAGES_PER_STEP, layer)
    return _finish_layer(x, o_f, o_d, gf, gd, lw, LANES), _new_rows(kf, vf, logf, kd, vd)


def kernel(x_prompt, x_sample, cache_fox_k, cache_fox_v, cache_fox_logf, cache_diff_k, cache_diff_v, page_table, rel_bias_table, attn_norm_g, w_in, b_forget, fox_q_norm_g, fox_k_norm_g, diff_q_norm_g, diff_k_norm_g, lambda_q1, lambda_k1, lambda_q2, lambda_k2, diff_out_norm_g, w_fox_up, w_diff_up, w_out, ffn_norm_g, w_router, b_router, w_gate_up, b_gate_up, w_down, b_down):
    depth = w_in.shape[0]
    d_model = x_prompt.shape[-1]
    yp, ys = x_prompt, x_sample
    rows_p, rows_s = [], []
    for l in range(depth):
        lw = _layer_weights(l, d_model, attn_norm_g, w_in, b_forget, fox_q_norm_g, fox_k_norm_g,
                            diff_q_norm_g, diff_k_norm_g, lambda_q1, lambda_k1, lambda_q2, lambda_k2,
                            diff_out_norm_g, w_fox_up, w_diff_up, w_out, ffn_norm_g, w_router, b_router,
                            w_gate_up, b_gate_up, w_down, b_down)
        yp, rp = _prompt_layer(yp, lw, rel_bias_table)
        caches = (cache_fox_k, cache_fox_v, cache_diff_k, cache_diff_v,
                  jnp.transpose(cache_fox_logf[l].astype(F32), (0, 2, 1)))
        ys, rs = _sample_layer(ys, lw, rel_bias_table, caches, page_table, l)
        rows_p.append(rp)
        rows_s.append(rs)
    stack = lambda rows, i: jnp.stack([r[i] for r in rows])
    return (yp, ys) + tuple(stack(rows_p, i) for i in range(5)) + tuple(stack(rows_s, i) for i in range(5))
```

```python
import functools
import math

import jax
import jax.numpy as jnp
from jax import lax
from jax.experimental import pallas as pl
from jax.experimental.pallas import tpu as pltpu

F32 = jnp.float32
BF16 = jnp.bfloat16

HEAD_DIM = 64
N_FOX_HEADS = 8
N_DIFF_HEADS = 4
FOX_WIDTH = N_FOX_HEADS * HEAD_DIM
DIFF_WIDTH = N_DIFF_HEADS * 2 * HEAD_DIM
N_BUCKETS = 32
MAX_DISTANCE = 128
N_EXPERTS = 32
TOP_K = 4
SWIGLU_LIMIT = 7.0
SWIGLU_ALPHA = 1.702
RMS_EPS = 1e-6
ATTN_SCALE = HEAD_DIM ** -0.5
LOG2E = math.log2(math.e)
Q_SCALE = ATTN_SCALE * LOG2E
LANES = 128
VMEM_LIMIT = 56 * 1024 * 1024


def _cparams(sem):
    return pltpu.CompilerParams(dimension_semantics=sem, vmem_limit_bytes=VMEM_LIMIT)


def _split3(x):
    x1 = x.astype(BF16)
    r1 = x - x1.astype(F32)
    x2 = r1.astype(BF16)
    x3 = (r1 - x2.astype(F32)).astype(BF16)
    return x1, x2, x3


def _log_sigmoid(x):
    return -(jnp.maximum(-x, 0.0) + jnp.log1p(jnp.exp(-jnp.abs(x))))


def _proj_kernel(x_ref, g_ref, w_ref, wfl_ref, bfl_ref, ones_ref, gains_ref,
                 qf_ref, kf_ref, vf_ref, lf_ref, qd_ref, kd_ref, vd_ref, gf_ref, gd_ref):
    x = x_ref[...]
    ms = jnp.mean(x * x, axis=-1, keepdims=True)
    xn = (x * lax.rsqrt(ms + RMS_EPS) * g_ref[...]).astype(BF16)

    def seg(lo, width):
        return jnp.dot(xn, w_ref[:, lo:lo + width], preferred_element_type=F32)

    ones_bd = ones_ref[...]

    def head_norm(p, gain):
        sq = p * p
        hi = sq.astype(BF16)
        lo = (sq - hi.astype(F32)).astype(BF16)
        ss = (jnp.dot(hi, ones_bd, preferred_element_type=F32)
              + jnp.dot(lo, ones_bd, preferred_element_type=F32))
        return p * lax.rsqrt(ss * (1.0 / HEAD_DIM) + RMS_EPS) * gain

    w = FOX_WIDTH
    qf_ref[...] = (head_norm(seg(0, w), gains_ref[0:1, :]) * Q_SCALE).astype(BF16)
    kf_ref[...] = head_norm(seg(w, w), gains_ref[1:2, :])
    vf_ref[...] = seg(2 * w, w)
    qd_ref[...] = (head_norm(seg(3 * w, w), gains_ref[2:3, :]) * Q_SCALE).astype(BF16)
    kd_ref[...] = head_norm(seg(4 * w, w), gains_ref[3:4, :])
    vd_ref[...] = seg(5 * w, w)
    d = gf_ref.shape[1]
    gf_ref[...] = jax.nn.sigmoid(seg(6 * w, d))
    gd_ref[...] = jax.nn.sigmoid(seg(6 * w + d, d))
    fl = jnp.dot(xn, wfl_ref[...], preferred_element_type=F32) + bfl_ref[...]
    lf_ref[...] = _log_sigmoid(fl)


def _in_projection(x2d, g, w_main, w_fl, b_fl, ones_bd, gains, tm):
    n, d = x2d.shape
    w = FOX_WIDTH
    tm = min(tm, n)
    assert n % tm == 0
    row = lambda i: (i, 0)
    const = lambda i: (0, 0)
    out_shape = (
        jax.ShapeDtypeStruct((n, w), BF16), jax.ShapeDtypeStruct((n, w), F32),
        jax.ShapeDtypeStruct((n, w), F32), jax.ShapeDtypeStruct((n, LANES), F32),
        jax.ShapeDtypeStruct((n, w), BF16), jax.ShapeDtypeStruct((n, w), F32),
        jax.ShapeDtypeStruct((n, w), F32), jax.ShapeDtypeStruct((n, d), F32),
        jax.ShapeDtypeStruct((n, d), F32),
    )
    out_specs = tuple(pl.BlockSpec((tm, s.shape[1]), row) for s in out_shape)
    return pl.pallas_call(
        _proj_kernel,
        grid=(n // tm,),
        in_specs=[
            pl.BlockSpec((tm, d), row),
            pl.BlockSpec((1, d), const),
            pl.BlockSpec(w_main.shape, const),
            pl.BlockSpec(w_fl.shape, const),
            pl.BlockSpec((1, LANES), const),
            pl.BlockSpec(ones_bd.shape, const),
            pl.BlockSpec(gains.shape, const),
        ],
        out_specs=out_specs,
        out_shape=out_shape,
        compiler_params=_cparams(("arbitrary",)),
        name="in_projection",
    )(x2d, g, w_main, w_fl, b_fl, ones_bd, gains)


def _scan_kernel(x_ref, tri_ref, o_ref, carry_sc):
    @pl.when(pl.program_id(1) == 0)
    def _():
        carry_sc[...] = jnp.zeros_like(carry_sc)

    x = x_ref[0]
    h = x.shape[0]
    x3 = jnp.concatenate(_split3(x), axis=0)
    y3 = jnp.dot(x3, tri_ref[...], preferred_element_type=F32)
    y = y3[0:h] + y3[h:2 * h] + y3[2 * h:3 * h] + carry_sc[:, 0:1]
    o_ref[0] = y
    carry_sc[...] = jnp.broadcast_to(y[:, -1:], carry_sc.shape)


def _prefix_sum_time(xt, tl):
    b, h, t = xt.shape
    tl = min(tl, t)
    assert t % tl == 0
    tri = (jnp.arange(tl)[:, None] <= jnp.arange(tl)[None, :]).astype(BF16)
    return pl.pallas_call(
        _scan_kernel,
        grid=(b, t // tl),
        in_specs=[pl.BlockSpec((1, h, tl), lambda i, j: (i, 0, j)),
                  pl.BlockSpec((tl, tl), lambda i, j: (0, 0))],
        out_specs=pl.BlockSpec((1, h, tl), lambda i, j: (i, 0, j)),
        out_shape=jax.ShapeDtypeStruct((b, h, t), F32),
        scratch_shapes=[pltpu.VMEM((h, LANES), F32)],
        compiler_params=_cparams(("arbitrary", "arbitrary")),
        name="logf_prefix_sum",
    )(xt, tri)


def _lane_tile(x, n):
    return x if n == 1 else jnp.concatenate([x] * n, axis=1)


def _online_softmax(s, m_ref, l_ref, idx):
    tk = s.shape[1]
    m_prev = m_ref[idx]
    m_next = jnp.maximum(m_prev, jnp.max(s, axis=1, keepdims=True))
    p = jnp.exp2(s - _lane_tile(m_next, tk // LANES))
    alpha = jnp.exp2(m_prev - m_next)
    l_ref[idx] = alpha * l_ref[idx] + jnp.sum(p, axis=1, keepdims=True)
    m_ref[idx] = m_next
    return p, alpha


def _half_mask(shape_lanes):
    lane = lax.broadcasted_iota(jnp.int32, (1, shape_lanes), 1)
    return lane < HEAD_DIM


_NT = (((1,), (1,)), ((), ()))


FOX_BIAS_LANES = 8


def _fox_bias_operands(c):
    b, t, h = c.shape
    to_bf16 = lambda x: lax.reduce_precision(x, exponent_bits=8, mantissa_bits=7)
    c1 = to_bf16(c)
    c2 = to_bf16(c - c1)
    c3 = to_bf16(c - c1 - c2)
    pieces = jnp.stack([c1, c2, c3], axis=-1).astype(BF16)
    ones = jnp.ones_like(pieces)
    pad = jnp.zeros(pieces.shape[:-1] + (FOX_BIAS_LANES - 6,), BF16)
    qx = jnp.concatenate([pieces, ones, pad], axis=-1).reshape(b, t, h * FOX_BIAS_LANES)
    kx = jnp.concatenate([ones, -pieces, pad], axis=-1).reshape(b, t, h * FOX_BIAS_LANES)
    fill = jnp.zeros((b, t, LANES - h * FOX_BIAS_LANES), BF16)
    return jnp.concatenate([qx, fill], axis=-1), jnp.concatenate([kx, fill], axis=-1)


def _fox_kernel(q_ref, k_ref, v_ref, qx_ref, kx_ref, o_ref, qaug_sc, m_sc, l_sc, acc_sc):
    i = pl.program_id(1)
    j = pl.program_id(2)
    tq = q_ref.shape[1]
    tk = k_ref.shape[1]
    first_half = _half_mask(LANES)

    @pl.when(j == 0)
    def _():
        m_sc[...] = jnp.full_like(m_sc, -jnp.inf)
        l_sc[...] = jnp.zeros_like(l_sc)
        acc_sc[...] = jnp.zeros_like(acc_sc)
        qx = qx_ref[0]
        lane_head = lax.broadcasted_iota(jnp.int32, (1, LANES), 1) // FOX_BIAS_LANES
        for h in range(N_FOX_HEADS):
            q2 = q_ref[0, :, (h // 2) * LANES:(h // 2 + 1) * LANES]
            half = first_half if h % 2 == 0 else jnp.logical_not(first_half)
            qaug_sc[h, :, :LANES] = jnp.where(half, q2, jnp.zeros_like(q2))
            qaug_sc[h, :, LANES:] = jnp.where(lane_head == h, qx, jnp.zeros_like(qx))

    def step(masked):
        if masked:
            row = lax.broadcasted_iota(jnp.int32, (tq, tk), 0)
            col = lax.broadcasted_iota(jnp.int32, (tq, tk), 1)
            keep = col <= row
        kx = kx_ref[0]
        for hp in range(N_FOX_HEADS // 2):
            cols = slice(hp * LANES, (hp + 1) * LANES)
            kaug = jnp.concatenate([k_ref[0, :, cols].astype(BF16), kx], axis=1)
            v2 = v_ref[0, :, cols].astype(BF16)
            pv = []
            alphas = []
            for sub in range(2):
                h = 2 * hp + sub
                s = lax.dot_general(qaug_sc[h], kaug, _NT, preferred_element_type=F32)
                if masked:
                    s = jnp.where(keep, s, -jnp.inf)
                p, alpha = _online_softmax(s, m_sc, l_sc, h)
                pv.append(jnp.dot(p.astype(BF16), v2, preferred_element_type=F32))
                alphas.append(alpha)
            acc_sc[hp] = (jnp.where(first_half, alphas[0], alphas[1]) * acc_sc[hp]
                          + jnp.where(first_half, pv[0], pv[1]))

    @pl.when(j < i)
    def _():
        step(False)

    @pl.when(j == i)
    def _():
        step(True)
        for hp in range(N_FOX_HEADS // 2):
            l2 = jnp.where(first_half, l_sc[2 * hp], l_sc[2 * hp + 1])
            o_ref[0, :, hp * LANES:(hp + 1) * LANES] = (acc_sc[hp] / l2).astype(o_ref.dtype)


def _fox_attention(q, k, v, c, tq):
    b, t, w = q.shape
    tq = min(tq, t)
    assert t % tq == 0 and tq % LANES == 0
    nq = t // tq
    qx, kx = _fox_bias_operands(c * LOG2E)
    qmap = lambda bi, i, j: (bi, i, 0)
    kmap = lambda bi, i, j: (bi, jnp.minimum(j, i), 0)
    return pl.pallas_call(
        _fox_kernel,
        grid=(b, nq, nq),
        in_specs=[
            pl.BlockSpec((1, tq, w), qmap),
            pl.BlockSpec((1, tq, w), kmap),
            pl.BlockSpec((1, tq, w), kmap),
            pl.BlockSpec((1, tq, LANES), qmap),
            pl.BlockSpec((1, tq, LANES), kmap),
        ],
        out_specs=pl.BlockSpec((1, tq, w), qmap),
        out_shape=jax.ShapeDtypeStruct((b, t, w), BF16),
        scratch_shapes=[pltpu.VMEM((N_FOX_HEADS, tq, 2 * LANES), BF16),
                        pltpu.VMEM((N_FOX_HEADS, tq, LANES), F32),
                        pltpu.VMEM((N_FOX_HEADS, tq, LANES), F32),
                        pltpu.VMEM((N_FOX_HEADS // 2, tq, LANES), F32)],
        compiler_params=_cparams(("arbitrary", "arbitrary", "arbitrary")),
        name="fox_attention",
    )(q, k, v, qx, kx)


def _diff_finalize(acc1, l1, acc2, l2, lam, g_out, out_scale):
    o = acc1 / l1 - lam * (acc2 / l2)
    ms = jnp.mean(o * o, axis=-1, keepdims=True)
    return o * lax.rsqrt(ms + RMS_EPS) * g_out * out_scale


def _diff_kernel(lam_ref, q_ref, k_ref, v_ref, bias_ref, go_ref, o_ref, qm_sc, m_sc, l_sc, acc_sc, *,
                 out_scale):
    i = pl.program_id(1)
    j = pl.program_id(2)

    @pl.when(j == 0)
    def _():
        m_sc[...] = jnp.full_like(m_sc, -jnp.inf)
        l_sc[...] = jnp.zeros_like(l_sc)
        acc_sc[...] = jnp.zeros_like(acc_sc)
        first_half = _half_mask(LANES)
        for idx in range(2 * N_DIFF_HEADS):
            q2 = q_ref[0, :, (idx // 2) * LANES:(idx // 2 + 1) * LANES]
            half = first_half if idx % 2 == 0 else jnp.logical_not(first_half)
            qm_sc[idx] = jnp.where(half, q2, jnp.zeros_like(q2))

    def step(bias_idx):
        for h in range(N_DIFF_HEADS):
            cols = slice(h * LANES, (h + 1) * LANES)
            k2 = k_ref[0, :, cols].astype(BF16)
            v2 = v_ref[0, :, cols].astype(BF16)
            for c in range(2):
                idx = 2 * h + c
                s = lax.dot_general(qm_sc[idx], k2, _NT, preferred_element_type=F32)
                if bias_idx is not None:
                    s = s + bias_ref[bias_idx, h]
                p, alpha = _online_softmax(s, m_sc, l_sc, idx)
                acc_sc[idx] = alpha * acc_sc[idx] + jnp.dot(p.astype(BF16), v2, preferred_element_type=F32)

    @pl.when(j < i - 1)
    def _():
        step(None)

    @pl.when(j == i - 1)
    def _():
        step(1)

    @pl.when(j == i)
    def _():
        step(0)
        lam = lam_ref[0]
        for h in range(N_DIFF_HEADS):
            o = _diff_finalize(acc_sc[2 * h], l_sc[2 * h], acc_sc[2 * h + 1], l_sc[2 * h + 1],
                               lam, go_ref[...], out_scale)
            o_ref[0, :, h * LANES:(h + 1) * LANES] = o.astype(o_ref.dtype)


def _t5_bucket_1d(n):
    max_exact = N_BUCKETS // 2
    nf = jnp.maximum(n, 1).astype(F32)
    large = max_exact + (jnp.log(nf / max_exact) / math.log(MAX_DISTANCE / max_exact)
                         * (N_BUCKETS - max_exact)).astype(jnp.int32)
    large = jnp.minimum(large, N_BUCKETS - 1)
    return jnp.where(n < max_exact, n, large)


def _rel_bias_lookup(rel_table, dist):
    rel = (rel_table.astype(F32) - rel_table[N_BUCKETS - 1].astype(F32)[None, :]) * LOG2E
    bucket = _t5_bucket_1d(jnp.maximum(dist, 0))
    out = jnp.zeros((rel.shape[1],) + dist.shape, F32)
    for b in range(N_BUCKETS - 1):
        out = jnp.where(bucket[None] == b, rel[b].reshape((-1,) + (1,) * dist.ndim), out)
    return out


def _rel_bias_tile(rel_table, dist, valid):
    return jnp.where(valid[None], _rel_bias_lookup(rel_table, dist), -jnp.inf)


def _diff_attention(q, k, v, rel_table, lam, g_out, out_scale, tq):
    b, t, w = q.shape
    tq = min(tq, t)
    assert t % tq == 0 and tq >= MAX_DISTANCE and tq % LANES == 0
    nq = t // tq
    r = jnp.arange(tq, dtype=jnp.int32)
    d0 = r[:, None] - r[None, :]
    bias = jnp.stack([_rel_bias_tile(rel_table, d0, d0 >= 0),
                      _rel_bias_tile(rel_table, d0 + tq, jnp.ones_like(d0, dtype=bool))])
    qmap = lambda bi, i, j: (bi, i, 0)
    kmap = lambda bi, i, j: (bi, jnp.minimum(j, i), 0)
    return pl.pallas_call(
        functools.partial(_diff_kernel, out_scale=out_scale),
        grid=(b, nq, nq),
        in_specs=[
            pl.BlockSpec(memory_space=pltpu.SMEM),
            pl.BlockSpec((1, tq, w), qmap),
            pl.BlockSpec((1, tq, w), kmap),
            pl.BlockSpec((1, tq, w), kmap),
            pl.BlockSpec(bias.shape, lambda bi, i, j: (0, 0, 0, 0)),
            pl.BlockSpec((1, LANES), lambda bi, i, j: (0, 0)),
        ],
        out_specs=pl.BlockSpec((1, tq, w), qmap),
        out_shape=jax.ShapeDtypeStruct((b, t, w), BF16),
        scratch_shapes=[pltpu.VMEM((2 * N_DIFF_HEADS, tq, LANES), BF16),
                        pltpu.VMEM((2 * N_DIFF_HEADS, tq, LANES), F32),
                        pltpu.VMEM((2 * N_DIFF_HEADS, tq, LANES), F32),
                        pltpu.VMEM((2 * N_DIFF_HEADS, tq, LANES), F32)],
        compiler_params=_cparams(("arbitrary", "arbitrary", "arbitrary")),
        name="diff_attention",
    )(lam, q, k, v, bias, g_out)


def _out_kernel(x_ref, of_ref, od_ref, gf_ref, gd_ref, wf_ref, wd_ref, wo_ref, g2_ref, wr_ref, br_ref,
                y_ref, h2_ref, lg_ref):
    merged = (gf_ref[...] * jnp.dot(of_ref[...], wf_ref[...], preferred_element_type=F32)
              + gd_ref[...] * jnp.dot(od_ref[...], wd_ref[...], preferred_element_type=F32))
    y = x_ref[...] + jnp.dot(merged.astype(BF16), wo_ref[...], preferred_element_type=F32)
    y_ref[...] = y
    ms = jnp.mean(y * y, axis=-1, keepdims=True)
    h2 = y * lax.rsqrt(ms + RMS_EPS) * g2_ref[...]
    h2_ref[...] = h2
    lg_ref[...] = jnp.dot(h2.astype(BF16), wr_ref[...], preferred_element_type=F32) + br_ref[...]


def _out_projection(x2d, o_f, o_d, gate_f, gate_d, w_fu, w_du, w_o, g2, w_r, b_r, tm):
    n, d = x2d.shape
    tm = min(tm, n)
    assert n % tm == 0
    row = lambda i: (i, 0)
    const = lambda i: (0, 0)
    return pl.pallas_call(
        _out_kernel,
        grid=(n // tm,),
        in_specs=[
            pl.BlockSpec((tm, d), row), pl.BlockSpec((tm, o_f.shape[1]), row),
            pl.BlockSpec((tm, o_d.shape[1]), row), pl.BlockSpec((tm, d), row), pl.BlockSpec((tm, d), row),
            pl.BlockSpec(w_fu.shape, const), pl.BlockSpec(w_du.shape, const), pl.BlockSpec(w_o.shape, const),
            pl.BlockSpec((1, d), const), pl.BlockSpec(w_r.shape, const), pl.BlockSpec((1, LANES), const),
        ],
        out_specs=(pl.BlockSpec((tm, d), row), pl.BlockSpec((tm, d), row), pl.BlockSpec((tm, LANES), row)),
        out_shape=(jax.ShapeDtypeStruct((n, d), F32), jax.ShapeDtypeStruct((n, d), F32),
                   jax.ShapeDtypeStruct((n, LANES), F32)),
        compiler_params=_cparams(("arbitrary",)),
        name="out_projection",
    )(x2d, o_f, o_d, gate_f, gate_d, w_fu, w_du, w_o, g2, w_r, b_r)


def _deinterleave_kernel(w_ref, perm_ref, g_ref, u_ref):
    w = w_ref[0].astype(BF16)
    r = jnp.dot(w, perm_ref[...], preferred_element_type=F32).astype(BF16)
    half = r.shape[1] // 2
    g_ref[0] = r[:, :half]
    u_ref[0] = r[:, half:]


def _deinterleave_gate_up(w_gu, tn):
    e, d, f2 = w_gu.shape
    assert f2 % tn == 0
    half = tn // 2
    col = jnp.arange(tn)
    perm = (jnp.where(col % 2 == 0, col // 2, half + col // 2)[:, None] == col[None, :]).astype(BF16)
    out = jax.ShapeDtypeStruct((e, d, f2 // 2), BF16)
    return pl.pallas_call(
        _deinterleave_kernel,
        grid=(e, f2 // tn),
        in_specs=[pl.BlockSpec((1, d, tn), lambda i, j: (i, 0, j)),
                  pl.BlockSpec((tn, tn), lambda i, j: (0, 0))],
        out_specs=(pl.BlockSpec((1, d, half), lambda i, j: (i, 0, j)),
                   pl.BlockSpec((1, d, half), lambda i, j: (i, 0, j))),
        out_shape=(out, out),
        compiler_params=_cparams(("arbitrary", "arbitrary")),
        name="deinterleave_gate_up",
    )(w_gu, perm)


def _route_kernel(lg_ref, tril_ref, idx_ref, gate_ref, rank_ref, cnt_ref, carry_sc):
    @pl.when(pl.program_id(0) == 0)
    def _():
        carry_sc[...] = jnp.zeros_like(carry_sc)

    x = lg_ref[...]
    lane = lax.broadcasted_iota(jnp.int32, x.shape, 1)
    lane_f = lane.astype(F32)
    x = jnp.where(lane < N_EXPERTS, x, -jnp.inf)
    carry = carry_sc[0:1, :]
    vals, idxs, ranks = [], [], []
    for _ in range(TOP_K):
        m = jnp.max(x, axis=1, keepdims=True)
        idx = jnp.min(jnp.where(x == m, lane_f, float(LANES)), axis=1, keepdims=True)
        onehot = lane_f == idx
        oh = jnp.where(onehot, 1.0, 0.0)
        before = jnp.dot(tril_ref[...], oh.astype(BF16), preferred_element_type=F32)
        ranks.append(jnp.sum(jnp.where(onehot, before + carry, 0.0), axis=1, keepdims=True))
        carry = carry + jnp.sum(oh, axis=0, keepdims=True)
        x = jnp.where(onehot, -jnp.inf, x)
        vals.append(m)
        idxs.append(idx)
    carry_sc[0:1, :] = carry
    cnt_ref[...] = jnp.broadcast_to(carry, cnt_ref.shape)
    e = [jnp.exp(v - vals[0]) for v in vals]
    tot = e[0]
    for ek in e[1:]:
        tot = tot + ek
    gates = jnp.zeros(x.shape, F32)
    idx_out = jnp.zeros(x.shape, F32)
    rank_out = jnp.zeros(x.shape, F32)
    for k in range(TOP_K):
        gates = jnp.where(lane == k, e[k] / tot, gates)
        idx_out = jnp.where(lane == k, idxs[k], idx_out)
        rank_out = jnp.where(lane == k, ranks[k], rank_out)
    gate_ref[...] = gates
    idx_ref[...] = idx_out.astype(jnp.int32)
    rank_ref[...] = rank_out.astype(jnp.int32)


def _route(logits, tm):
    n = logits.shape[0]
    tm = min(tm, n)
    assert n % tm == 0
    tril = (jnp.arange(tm)[None, :] < jnp.arange(tm)[:, None]).astype(BF16)
    row = lambda i: (i, 0)
    idx, gates, rank, cnt = pl.pallas_call(
        _route_kernel,
        grid=(n // tm,),
        in_specs=[pl.BlockSpec((tm, LANES), row), pl.BlockSpec((tm, tm), lambda i: (0, 0))],
        out_specs=(pl.BlockSpec((tm, LANES), row), pl.BlockSpec((tm, LANES), row),
                   pl.BlockSpec((tm, LANES), row), pl.BlockSpec((8, LANES), lambda i: (0, 0))),
        out_shape=(jax.ShapeDtypeStruct((n, LANES), jnp.int32), jax.ShapeDtypeStruct((n, LANES), F32),
                   jax.ShapeDtypeStruct((n, LANES), jnp.int32), jax.ShapeDtypeStruct((8, LANES), F32)),
        scratch_shapes=[pltpu.VMEM((8, LANES), F32)],
        compiler_params=_cparams(("arbitrary",)),
        name="moe_route",
    )(logits, tril)
    return (idx[:, :TOP_K], gates[:, :TOP_K], rank[:, :TOP_K],
            cnt[0, :N_EXPERTS].astype(jnp.int32))


def _dispatch_kernel(slot_ref, h_ref, xs_in, xs_out, sem):
    del xs_in
    rows = h_ref.shape[0]

    def body(r, carry):
        for k in range(TOP_K):
            s = slot_ref[0, 0, r * TOP_K + k]
            pltpu.make_async_copy(h_ref.at[pl.ds(r, 1)], xs_out.at[pl.ds(s, 1)], sem).start()
        return carry

    lax.fori_loop(0, rows, body, 0, unroll=2)
    for _ in range(TOP_K):
        pltpu.make_async_copy(h_ref, xs_out.at[pl.ds(0, rows)], sem).wait()


def _dispatch(h2, slot, n_slots, tm):
    n, d = h2.shape
    tm = min(tm, n)
    assert n % tm == 0
    slot3 = slot.reshape(n // tm, 1, tm * TOP_K)
    return pl.pallas_call(
        _dispatch_kernel,
        grid=(n // tm,),
        in_specs=[pl.BlockSpec((1, 1, tm * TOP_K), lambda i: (i, 0, 0), memory_space=pltpu.SMEM),
                  pl.BlockSpec((tm, d), lambda i: (i, 0)),
                  pl.BlockSpec(memory_space=pl.ANY)],
        out_specs=pl.BlockSpec(memory_space=pl.ANY),
        out_shape=jax.ShapeDtypeStruct((n_slots, d), F32),
        scratch_shapes=[pltpu.SemaphoreType.DMA(())],
        input_output_aliases={2: 0},
        compiler_params=_cparams(("arbitrary",)),
        name="moe_dispatch",
    )(slot3, h2, jnp.zeros((n_slots, d), F32))


def _row_gather_start(src_hbm, idx_ref, n_rows, dst, sem):
    def body(r, carry):
        tok = idx_ref[0, 0, r]
        pltpu.make_async_copy(src_hbm.at[pl.ds(tok, 1)], dst.at[pl.ds(r, 1)], sem).start()
        return carry
    lax.fori_loop(0, n_rows, body, 0, unroll=8)


def _row_gather_wait(src_hbm, n_rows, dst, sem):
    pltpu.make_async_copy(src_hbm.at[pl.ds(0, n_rows)], dst, sem).wait()


def _moe_kernel(be_ref, nused_ref, x_ref, wg_ref, wu_ref, bg_ref, bu_ref, wd_ref, bd_ref, y_ref):
    i = pl.program_id(0)
    n_used = nused_ref[0]

    @pl.when(i < n_used)
    def _():
        x = x_ref[...].astype(BF16)
        gate = jnp.dot(x, wg_ref[0], preferred_element_type=F32) + bg_ref[0]
        up = jnp.dot(x, wu_ref[0], preferred_element_type=F32) + bu_ref[0]
        gate = jnp.minimum(gate, SWIGLU_LIMIT)
        up = jnp.clip(up, -SWIGLU_LIMIT, SWIGLU_LIMIT)
        glu = gate * jax.nn.sigmoid(SWIGLU_ALPHA * gate)
        act = ((up + 1.0) * glu).astype(BF16)
        y_ref[...] = jnp.dot(act, wd_ref[0], preferred_element_type=F32) + bd_ref[0]

    @pl.when(i >= n_used)
    def _():
        y_ref[...] = jnp.zeros_like(y_ref)


def _expert_ffn(xs, block_expert, n_used, w_g, w_u, b_g, b_u, w_d, b_d, bm):
    n_slots, d = xs.shape
    n_blocks = block_expert.shape[0]
    dff = w_g.shape[2]
    wmap = lambda i, be, nu: (be[i], 0, 0)
    grid_spec = pltpu.PrefetchScalarGridSpec(
        num_scalar_prefetch=2,
        grid=(n_blocks,),
        in_specs=[
            pl.BlockSpec((bm, d), lambda i, be, nu: (i, 0)),
            pl.BlockSpec((1, d, dff), wmap), pl.BlockSpec((1, d, dff), wmap),
            pl.BlockSpec((1, 1, dff), wmap), pl.BlockSpec((1, 1, dff), wmap),
            pl.BlockSpec((1, dff, d), wmap), pl.BlockSpec((1, 1, d), wmap),
        ],
        out_specs=pl.BlockSpec((bm, d), lambda i, be, nu: (i, 0)),
    )
    return pl.pallas_call(
        _moe_kernel,
        grid_spec=grid_spec,
        out_shape=jax.ShapeDtypeStruct((n_slots, d), F32),
        compiler_params=_cparams(("arbitrary",)),
        name="expert_ffn",
    )(block_expert, n_used, xs, w_g, w_u, b_g, b_u, w_d, b_d)


def _combine_kernel(idx_ref, idx_next_ref, yb_hbm, resid_ref, gates_ref, o_ref, buf, sem):
    i = pl.program_id(0)
    n = pl.num_programs(0)
    rows = buf.shape[1]
    tc = resid_ref.shape[0]
    slot = lax.rem(i, 2)

    @pl.when(i == 0)
    def _():
        _row_gather_start(yb_hbm, idx_ref, rows, buf.at[0], sem.at[0])

    @pl.when(i + 1 < n)
    def _():
        _row_gather_start(yb_hbm, idx_next_ref, rows, buf.at[1 - slot], sem.at[1 - slot])

    _row_gather_wait(yb_hbm, rows, buf.at[slot], sem.at[slot])
    acc = jnp.zeros(o_ref.shape, F32)
    for k in range(TOP_K):
        acc = acc + buf[slot, k * tc:(k + 1) * tc, :] * gates_ref[:, k:k + 1]
    o_ref[...] = resid_ref[...] + acc


def _combine(yb, slot_km, gates, resid, tc):
    n, d = resid.shape
    n_tiles = n // tc
    rows = TOP_K * tc
    return pl.pallas_call(
        _combine_kernel,
        grid=(n_tiles,),
        in_specs=[
            pl.BlockSpec((1, 1, rows), lambda i: (i, 0, 0), memory_space=pltpu.SMEM),
            pl.BlockSpec((1, 1, rows), lambda i: (jnp.minimum(i + 1, n_tiles - 1), 0, 0),
                         memory_space=pltpu.SMEM),
            pl.BlockSpec(memory_space=pl.ANY),
            pl.BlockSpec((tc, d), lambda i: (i, 0)),
            pl.BlockSpec((tc, TOP_K), lambda i: (i, 0)),
        ],
        out_specs=pl.BlockSpec((tc, d), lambda i: (i, 0)),
        out_shape=jax.ShapeDtypeStruct((n, d), F32),
        scratch_shapes=[pltpu.VMEM((2, rows, d), F32), pltpu.SemaphoreType.DMA((2,))],
        compiler_params=_cparams(("arbitrary",)),
        name="moe_combine",
    )(slot_km, slot_km, yb, resid, gates)


def _moe_layer(y1, h2, logits, moe_w, bm, tc):
    n, d = y1.shape
    w_g, w_u, b_g, b_u, w_d, b_d = moe_w
    bm = min(bm, n)
    tc = min(tc, n)
    idx, gates, rank, counts = _route(logits, ROUTE_ROWS)
    padded = (counts + bm - 1) // bm * bm
    pend = jnp.cumsum(padded)
    pstart = pend - padded
    experts = jnp.arange(N_EXPERTS, dtype=jnp.int32)
    slot = rank + jnp.sum(jnp.where(idx[..., None] == experts, pstart, 0), axis=-1).astype(jnp.int32)
    n_blocks = -(-(n * TOP_K) // bm) + N_EXPERTS
    block_start = jnp.arange(n_blocks, dtype=jnp.int32) * bm
    block_expert = jnp.minimum(jnp.sum(pend[None, :] <= block_start[:, None], axis=1),
                               N_EXPERTS - 1).astype(jnp.int32)
    n_used = (pend[-1] // bm).astype(jnp.int32).reshape(1)
    xs = _dispatch(h2, slot, n_blocks * bm, DISPATCH_ROWS)
    yb = _expert_ffn(xs, block_expert, n_used, w_g, w_u, b_g, b_u, w_d, b_d, bm)
    slot_km = slot.reshape(n // tc, tc, TOP_K).transpose(0, 2, 1).reshape(n // tc, 1, TOP_K * tc)
    return _combine(yb, slot_km, gates, y1, tc)


def _sample_attn_kernel(pt_ref, lam_ref, qsf_ref, qsd_ref, cncol_ref, cnrow_ref, snew_ref, blast_ref,
                        knf_ref, vnf_ref, knd_ref, vnd_ref, tri_ref, go_ref, *rest, pps, out_scale):
    page_refs = rest[:5 * pps]
    of_ref, od_ref, m_sc, l_sc, accf_sc, accd_sc, carry_sc = rest[5 * pps:]
    kf_refs = page_refs[0 * pps:1 * pps]
    vf_refs = page_refs[1 * pps:2 * pps]
    kd_refs = page_refs[2 * pps:3 * pps]
    vd_refs = page_refs[3 * pps:4 * pps]
    lft_refs = page_refs[4 * pps:5 * pps]
    g = pl.program_id(1)
    n_g = pl.num_programs(1)
    nh = N_FOX_HEADS
    nd_heads = N_DIFF_HEADS
    half_rows = cncol_ref.shape[1] // 2
    tok = half_rows // nh

    @pl.when(g == 0)
    def _():
        m_sc[...] = jnp.full_like(m_sc, -jnp.inf)
        l_sc[...] = jnp.zeros_like(l_sc)
        accf_sc[...] = jnp.zeros_like(accf_sc)
        accd_sc[...] = jnp.zeros_like(accd_sc)
        carry_sc[...] = jnp.zeros_like(carry_sc)

    _NN = (((1,), (0,)), ((), ()))

    def scores(kf_heads, kd_heads, fox_dims):
        rows = [lax.dot_general(qsf_ref[0, h], kf_heads[h], fox_dims, preferred_element_type=F32)
                for h in range(nh)]
        rows += [lax.dot_general(qsd_ref[0, h], kd_heads[h], _NT, preferred_element_type=F32)
                 for h in range(nd_heads)]
        return jnp.concatenate(rows, axis=0)

    def update(s, vf_heads, vd_heads, fox_dims):
        p, alpha = _online_softmax(s, m_sc, l_sc, slice(None))
        pv_f = jnp.concatenate(
            [lax.dot_general(p[h * tok:(h + 1) * tok].astype(BF16), vf_heads[h], fox_dims,
                             preferred_element_type=F32) for h in range(nh)], axis=0)
        pv_d = jnp.concatenate(
            [jnp.dot(p[half_rows + 2 * h * tok:half_rows + 2 * (h + 1) * tok].astype(BF16), vd_heads[h],
                     preferred_element_type=F32) for h in range(nd_heads)], axis=0)
        accf_sc[...] = alpha[:half_rows, :HEAD_DIM] * accf_sc[...] + pv_f
        accd_sc[...] = alpha[half_rows:] * accd_sc[...] + pv_d

    def fox_page_heads(refs):
        return [jnp.concatenate([ref[0, 0, h] for ref in refs], axis=1).astype(BF16) for h in range(nh)]

    def diff_page_heads(refs):
        page = refs[0].shape[2] // nd_heads
        return [jnp.concatenate([ref[0, 0, pl.ds(h, page, stride=nd_heads), :] for ref in refs],
                                axis=0).astype(BF16) for h in range(nd_heads)]

    def past_chunk(with_bias):
        carry = carry_sc[...]
        suf = [None] * pps
        for p in reversed(range(pps)):
            x = lft_refs[p][0]
            x3 = jnp.concatenate(_split3(x), axis=0)
            y3 = jnp.dot(x3, tri_ref[...], preferred_element_type=F32)
            within = y3[0:nh] + y3[nh:2 * nh] + y3[2 * nh:3 * nh]
            suf[p] = within + carry
            carry = carry + (within[:, 0:1] + x[:, 0:1])
        carry_sc[...] = carry
        sufc = jnp.concatenate(suf, axis=1) * LOG2E
        r = sufc.shape[1]
        suf_rows = jnp.concatenate(
            [jnp.broadcast_to(sufc[h:h + 1, :], (tok, r)) for h in range(nh)]
            + [jnp.zeros((half_rows, r), F32)], axis=0)
        s = scores(fox_page_heads(kf_refs), diff_page_heads(kd_refs), _NN)
        s = s + _lane_tile(cncol_ref[0], r // LANES) + suf_rows
        if with_bias:
            s = s + blast_ref[...]
        update(s, fox_page_heads(vf_refs), diff_page_heads(vd_refs), _NT)

    @pl.when(g == 0)
    def _():
        past_chunk(True)

    @pl.when(g > 0)
    def _():
        past_chunk(False)

    @pl.when(g == n_g - 1)
    def _():
        def new_heads(ref, n_heads):
            width = ref.shape[2] // n_heads
            return [ref[0, :, h * width:(h + 1) * width].astype(BF16) for h in range(n_heads)]

        s = scores(new_heads(knf_ref, nh), new_heads(knd_ref, nd_heads), _NT)
        s = s + (cncol_ref[0] - cnrow_ref[0]) + snew_ref[...]
        update(s, new_heads(vnf_ref, nh), new_heads(vnd_ref, nd_heads), _NN)

        l = l_sc[...]
        nf = accf_sc[...] / l[:half_rows, :HEAD_DIM]
        for h in range(nh):
            of_ref[0, :, h * HEAD_DIM:(h + 1) * HEAD_DIM] = nf[h * tok:(h + 1) * tok].astype(of_ref.dtype)

        nd = accd_sc[...] / l[half_rows:]
        lam = lam_ref[0]
        for h in range(N_DIFF_HEADS):
            cols = slice(h * LANES, (h + 1) * LANES)
            a1 = nd[(2 * h) * tok:(2 * h + 1) * tok]
            a2 = nd[(2 * h + 1) * tok:(2 * h + 2) * tok]
            o = a1 - lam * a2
            ms = jnp.mean(o * o, axis=-1, keepdims=True)
            od_ref[0, :, cols] = (o * lax.rsqrt(ms + RMS_EPS) * go_ref[...] * out_scale).astype(od_ref.dtype)


def _sample_attention(qf, qd, kf_new, vf_new, kd_new, vd_new, c_new, caches, page_table, rel_table, lam,
                      g_out, out_scale, pps, layer):
    db, s_new, w = qf.shape
    ck_f, cv_f, ck_d, cv_d, clf_t = caches
    page = ck_f.shape[2]
    ck_f, cv_f = (jnp.transpose(a, (0, 1, 3, 4, 2)) for a in (ck_f, cv_f))
    ck_d, cv_d = (a.reshape(a.shape[0], a.shape[1], page * N_DIFF_HEADS, 2 * HEAD_DIM) for a in (ck_d, cv_d))
    n_pages = page_table.shape[1]
    pps = min(pps, n_pages)
    assert n_pages % pps == 0 and page == LANES and s_new * N_FOX_HEADS * 2 == LANES
    n_g = n_pages // pps
    r = pps * page
    nh = N_FOX_HEADS
    half_rows = nh * s_new

    qsf = jnp.transpose(qf.reshape(db, s_new, nh, HEAD_DIM), (0, 2, 1, 3))
    qd_h = jnp.transpose(qd.reshape(db, s_new, N_DIFF_HEADS, 2 * HEAD_DIM), (0, 2, 1, 3))
    first = (jnp.arange(2 * HEAD_DIM) < HEAD_DIM)
    qsd = jnp.concatenate([jnp.where(first, qd_h, jnp.zeros((), qd.dtype)),
                           jnp.where(first, jnp.zeros((), qd.dtype), qd_h)], axis=2)

    c_new = c_new * LOG2E
    c_rows =jnp.transpose(c_new, (0, 2, 1)).reshape(db, half_rows)
    cn_col = jnp.concatenate([c_rows, jnp.zeros_like(c_rows)], axis=1)[:, :, None]
    cn_col = jnp.broadcast_to(cn_col, (db, LANES, LANES))
    c_keys = jnp.repeat(jnp.transpose(c_new, (0, 2, 1)), s_new, axis=1)
    cn_row = jnp.zeros((db, LANES, LANES), F32).at[:, :half_rows, :s_new].set(c_keys)

    t_of_row = jnp.arange(LANES, dtype=jnp.int32) % s_new
    key = jnp.arange(LANES, dtype=jnp.int32)
    valid_new = (key[None, :] <= t_of_row[:, None]) & (key[None, :] < s_new)
    is_diff = (jnp.arange(LANES) >= half_rows)
    head_of_row = jnp.clip((jnp.arange(LANES) - half_rows) // (2 * s_new), 0, N_DIFF_HEADS - 1)

    def rel_rows(dist):
        per_head = _rel_bias_lookup(rel_table, dist)
        vals = jnp.zeros(dist.shape, F32)
        for h in range(N_DIFF_HEADS):
            vals = jnp.where(head_of_row[:, None] == h, per_head[h], vals)
        return jnp.where(is_diff[:, None], vals, 0.0)

    s_new_tile = jnp.where(valid_new, rel_rows(t_of_row[:, None] - key[None, :]), -jnp.inf)
    key_r = jnp.arange(r, dtype=jnp.int32)
    b_last = rel_rows(r - key_r[None, :] + t_of_row[:, None])

    def pad_rows(x):
        return jnp.pad(x, ((0, 0), (0, LANES - s_new), (0, 0)))

    tri = (jnp.arange(page)[:, None] > jnp.arange(page)[None, :]).astype(BF16)

    bmap = lambda b, g, pt: (b, 0, 0)
    bmap4 = lambda b, g, pt: (b, 0, 0, 0)
    const2 = lambda b, g, pt: (0, 0)

    def page_map(p):
        return lambda b, g, pt: (pt[b, (n_g - 1 - g) * pps + p], 0, 0)

    def cache_map(p, ndim):
        return lambda b, g, pt: (layer, pt[b, (n_g - 1 - g) * pps + p]) + (0,) * (ndim - 2)

    in_specs = [
        pl.BlockSpec(memory_space=pltpu.SMEM),
        pl.BlockSpec((1,) + qsf.shape[1:], bmap4), pl.BlockSpec((1,) + qsd.shape[1:], bmap4),
        pl.BlockSpec((1, LANES, LANES), bmap), pl.BlockSpec((1, LANES, LANES), bmap),
        pl.BlockSpec((LANES, LANES), const2), pl.BlockSpec((LANES, r), const2),
        pl.BlockSpec((1, LANES, w), bmap), pl.BlockSpec((1, LANES, w), bmap),
        pl.BlockSpec((1, LANES, w), bmap), pl.BlockSpec((1, LANES, w), bmap),
        pl.BlockSpec((page, page), const2), pl.BlockSpec((1, LANES), const2),
    ]
    operands = [lam, qsf, qsd, cn_col, cn_row, s_new_tile, b_last,
                pad_rows(kf_new), pad_rows(vf_new), pad_rows(kd_new), pad_rows(vd_new), tri, g_out]
    for arr in (ck_f, cv_f, ck_d, cv_d):
        for p in range(pps):
            in_specs.append(pl.BlockSpec((1, 1) + arr.shape[2:], cache_map(p, arr.ndim)))
            operands.append(arr)
    for p in range(pps):
        in_specs.append(pl.BlockSpec((1, nh, page), page_map(p)))
        operands.append(clf_t)

    grid_spec = pltpu.PrefetchScalarGridSpec(
        num_scalar_prefetch=1,
        grid=(db, n_g),
        in_specs=in_specs,
        out_specs=(pl.BlockSpec((1, s_new, w), bmap), pl.BlockSpec((1, s_new, w), bmap)),
        scratch_shapes=[pltpu.VMEM((LANES, LANES), F32), pltpu.VMEM((LANES, LANES), F32),
                        pltpu.VMEM((half_rows, HEAD_DIM), F32), pltpu.VMEM((half_rows, 2 * HEAD_DIM), F32),
                        pltpu.VMEM((nh, LANES), F32)],
    )
    return pl.pallas_call(
        functools.partial(_sample_attn_kernel, pps=pps, out_scale=out_scale),
        grid_spec=grid_spec,
        out_shape=(jax.ShapeDtypeStruct((db, s_new, w), BF16), jax.ShapeDtypeStruct((db, s_new, w), BF16)),
        compiler_params=_cparams(("arbitrary", "arbitrary")),
        name="sample_attention",
    )(page_table, *operands)


PROJ_ROWS = 256
ATTN_BLOCK = 512
SCAN_BLOCK = 512
MOE_ROWS = 256
COMBINE_ROWS = 128
ROUTE_ROWS = 512
DISPATCH_ROWS = 512
DEINTERLEAVE_COLS = 512
PAGES_PER_STEP = 8


def _split_points(d_model):
    sizes = (FOX_WIDTH, FOX_WIDTH, FOX_WIDTH, N_FOX_HEADS, DIFF_WIDTH, DIFF_WIDTH, DIFF_WIDTH, d_model)
    pts, acc = [], 0
    for s in sizes:
        acc += s
        pts.append(acc)
    return pts


def _layer_weights(l, d_model, attn_norm_g, w_in, b_forget, fox_q_norm_g, fox_k_norm_g, diff_q_norm_g,
                   diff_k_norm_g, lambda_q1, lambda_k1, lambda_q2, lambda_k2, diff_out_norm_g, w_fox_up,
                   w_diff_up, w_out, ffn_norm_g, w_router, b_router, w_gate_up, b_gate_up, w_down, b_down):
    lambda_init = 0.8 - 0.6 * math.exp(-0.3 * l)
    wq_f, wk_f, wv_f, w_fl, wq_d, wk_d, wv_d, wg_f, wg_d = jnp.split(w_in[l], _split_points(d_model), axis=1)
    w_main = jnp.concatenate([wq_f, wk_f, wv_f, wq_d, wk_d, wv_d, wg_f, wg_d], axis=1).astype(BF16)
    w_fl = jnp.pad(w_fl, ((0, 0), (0, LANES - N_FOX_HEADS))).astype(BF16)
    b_fl = jnp.pad(b_forget[l].astype(F32), (0, LANES - N_FOX_HEADS)).reshape(1, LANES)
    grp = jnp.arange(FOX_WIDTH) // HEAD_DIM
    ones_bd = (grp[:, None] == grp[None, :]).astype(BF16)
    reps = FOX_WIDTH // HEAD_DIM
    gains = jnp.stack([jnp.tile(g[l].astype(F32), reps)
                       for g in (fox_q_norm_g, fox_k_norm_g, diff_q_norm_g, diff_k_norm_g)])
    lam = (jnp.exp(jnp.sum(lambda_q1[l].astype(F32) * lambda_k1[l].astype(F32)))
           - jnp.exp(jnp.sum(lambda_q2[l].astype(F32) * lambda_k2[l].astype(F32))) + lambda_init)
    w_r = jnp.pad(w_router[l], ((0, 0), (0, LANES - N_EXPERTS))).astype(BF16)
    b_r = jnp.pad(b_router[l].astype(F32), (0, LANES - N_EXPERTS)).reshape(1, LANES)
    w_g, w_u = _deinterleave_gate_up(w_gate_up[l], DEINTERLEAVE_COLS)
    bgu = b_gate_up[l].astype(F32)
    moe_w = (w_g, w_u, bgu[:, None, 0::2], bgu[:, None, 1::2],
             w_down[l].astype(BF16), b_down[l].astype(F32)[:, None, :])
    return dict(
        lambda_init=lambda_init, g1=attn_norm_g[l].astype(F32).reshape(1, d_model), w_main=w_main, w_fl=w_fl,
        b_fl=b_fl, ones_bd=ones_bd, gains=gains, lam=lam.reshape(1).astype(F32),
        g_out=diff_out_norm_g[l].astype(F32).reshape(1, LANES),
        w_fu=w_fox_up[l].astype(BF16), w_du=w_diff_up[l].astype(BF16), w_o=w_out[l].astype(BF16),
        g2=ffn_norm_g[l].astype(F32).reshape(1, d_model), w_r=w_r, b_r=b_r, moe_w=moe_w)


def _project(x, lw):
    b, t, d = x.shape
    outs = _in_projection(x.reshape(b * t, d), lw['g1'], lw['w_main'], lw['w_fl'], lw['b_fl'],
                          lw['ones_bd'], lw['gains'], PROJ_ROWS)
    qf, kf, vf, lf, qd, kd, vd, gf, gd = outs
    logf = lf[:, :N_FOX_HEADS].reshape(b, t, N_FOX_HEADS)
    r3 = lambda a: a.reshape(b, t, a.shape[-1])
    return r3(qf), r3(kf), r3(vf), logf, r3(qd), r3(kd), r3(vd), gf, gd


def _finish_layer(x, o_f, o_d, gf, gd, lw, moe_rows):
    b, t, d = x.shape
    n = b * t
    y1, h2, logits = _out_projection(x.reshape(n, d), o_f.reshape(n, -1), o_d.reshape(n, -1), gf, gd,
                                     lw['w_fu'], lw['w_du'], lw['w_o'], lw['g2'], lw['w_r'], lw['b_r'],
                                     PROJ_ROWS)
    return _moe_layer(y1, h2, logits, lw['moe_w'], moe_rows, COMBINE_ROWS).reshape(b, t, d)


def _new_rows(kf, vf, logf, kd, vd):
    b, t, _ = kf.shape
    return (kf.reshape(b, t, N_FOX_HEADS, HEAD_DIM), vf.reshape(b, t, N_FOX_HEADS, HEAD_DIM), logf,
            kd.reshape(b, t, N_DIFF_HEADS, 2 * HEAD_DIM), vd.reshape(b, t, N_DIFF_HEADS, 2 * HEAD_DIM))


def _prompt_layer(x, lw, rel_table):
    qf, kf, vf, logf, qd, kd, vd, gf, gd = _project(x, lw)
    ct = _prefix_sum_time(jnp.transpose(logf, (0, 2, 1)), SCAN_BLOCK)
    c = jnp.transpose(ct, (0, 2, 1))
    o_f = _fox_attention(qf, kf, vf, c, ATTN_BLOCK)
    o_d = _diff_attention(qd, kd, vd, rel_table, lw['lam'], lw['g_out'], 1.0 - lw['lambda_init'], ATTN_BLOCK)
    return _finish_layer(x, o_f, o_d, gf, gd, lw, MOE_ROWS), _new_rows(kf, vf, logf, kd, vd)


def _sample_layer(x, lw, rel_table, caches, page_table, layer):
    qf, kf, vf, logf, qd, kd, vd, gf, gd = _project(x, lw)
    b, t, _ = logf.shape
    logf_t = jnp.pad(jnp.transpose(logf, (0, 2, 1)), ((0, 0), (0, 0), (0, LANES - t)))
    c = jnp.transpose(_prefix_sum_time(logf_t, LANES)[:, :, :t], (0, 2, 1))
    o_f, o_d = _sample_attention(qf, qd, kf, vf, kd, vd, c, caches, page_table, rel_table, lw['lam'],
                                 lw['g_out'], 1.0 - lw['lambda_init'], PAGES_PER_STEP, layer)
    return _finish_layer(x, o_f, o_d, gf, gd, lw, LANES), _new_rows(kf, vf, logf, kd, vd)


def kernel(x_prompt, x_sample, cache_fox_k, cache_fox_v, cache_fox_logf, cache_diff_k, cache_diff_v, page_table, rel_bias_table, attn_norm_g, w_in, b_forget, fox_q_norm_g, fox_k_norm_g, diff_q_norm_g, diff_k_norm_g, lambda_q1, lambda_k1, lambda_q2, lambda_k2, diff_out_norm_g, w_fox_up, w_diff_up, w_out, ffn_norm_g, w_router, b_router, w_gate_up, b_gate_up, w_down, b_down):
    depth = w_in.shape[0]
    d_model = x_prompt.shape[-1]
    yp, ys = x_prompt, x_sample
    rows_p, rows_s = [], []
    for l in range(depth):
        lw = _layer_weights(l, d_model, attn_norm_g, w_in, b_forget, fox_q_norm_g, fox_k_norm_g,
                            diff_q_norm_g, diff_k_norm_g, lambda_q1, lambda_k1, lambda_q2, lambda_k2,
                            diff_out_norm_g, w_fox_up, w_diff_up, w_out, ffn_norm_g, w_router, b_router,
                            w_gate_up, b_gate_up, w_down, b_down)
        yp, rp = _prompt_layer(yp, lw, rel_bias_table)
        caches = (cache_fox_k, cache_fox_v, cache_diff_k, cache_diff_v,
                  jnp.transpose(cache_fox_logf[l].astype(F32), (0, 2, 1)))
        ys, rs = _sample_layer(ys, lw, rel_bias_table, caches, page_table, l)
        rows_p.append(rp)
        rows_s.append(rs)
    stack = lambda rows, i: jnp.stack([r[i] for r in rows])
    return (yp, ys) + tuple(stack(rows_p, i) for i in range(5)) + tuple(stack(rows_s, i) for i in range(5))
```

```python
import functools
import math

import jax
import jax.numpy as jnp
from jax import lax
from jax.experimental import pallas as pl
from jax.experimental.pallas import tpu as pltpu

F32 = jnp.float32
BF16 = jnp.bfloat16

HEAD_DIM = 64
N_FOX_HEADS = 8
N_DIFF_HEADS = 4
FOX_WIDTH = N_FOX_HEADS * HEAD_DIM
DIFF_WIDTH = N_DIFF_HEADS * 2 * HEAD_DIM
N_BUCKETS = 32
MAX_DISTANCE = 128
N_EXPERTS = 32
TOP_K = 4
SWIGLU_LIMIT = 7.0
SWIGLU_ALPHA = 1.702
RMS_EPS = 1e-6
ATTN_SCALE = HEAD_DIM ** -0.5
LOG2E = math.log2(math.e)
Q_SCALE = ATTN_SCALE * LOG2E
LANES = 128
VMEM_LIMIT = 56 * 1024 * 1024


def _cparams(sem):
    return pltpu.CompilerParams(dimension_semantics=sem, vmem_limit_bytes=VMEM_LIMIT)


def _split3(x):
    x1 = x.astype(BF16)
    r1 = x - x1.astype(F32)
    x2 = r1.astype(BF16)
    x3 = (r1 - x2.astype(F32)).astype(BF16)
    return x1, x2, x3


def _log_sigmoid(x):
    return -(jnp.maximum(-x, 0.0) + jnp.log1p(jnp.exp(-jnp.abs(x))))


def _proj_kernel(x_ref, g_ref, w_ref, wfl_ref, bfl_ref, ones_ref, gains_ref,
                 qf_ref, kf_ref, vf_ref, lf_ref, qd_ref, kd_ref, vd_ref, gf_ref, gd_ref, *t_refs):
    x = x_ref[...]
    ms = jnp.mean(x * x, axis=-1, keepdims=True)
    xn = (x * lax.rsqrt(ms + RMS_EPS) * g_ref[...]).astype(BF16)

    def seg(lo, width):
        return jnp.dot(xn, w_ref[:, lo:lo + width], preferred_element_type=F32)

    ones_bd = ones_ref[...]

    def head_norm(p, gain):
        sq = p * p
        hi = sq.astype(BF16)
        lo = (sq - hi.astype(F32)).astype(BF16)
        ss = (jnp.dot(hi, ones_bd, preferred_element_type=F32)
              + jnp.dot(lo, ones_bd, preferred_element_type=F32))
        return p * lax.rsqrt(ss * (1.0 / HEAD_DIM) + RMS_EPS) * gain

    w = FOX_WIDTH
    qf_ref[...] = (head_norm(seg(0, w), gains_ref[0:1, :]) * Q_SCALE).astype(BF16)
    kf = head_norm(seg(w, w), gains_ref[1:2, :])
    vf = seg(2 * w, w)
    kf_ref[...] = kf
    vf_ref[...] = vf
    if t_refs:
        t_refs[0][0] = kf.T
        t_refs[1][0] = vf.T
    qd_ref[...] = (head_norm(seg(3 * w, w), gains_ref[2:3, :]) * Q_SCALE).astype(BF16)
    kd_ref[...] = head_norm(seg(4 * w, w), gains_ref[3:4, :])
    vd_ref[...] = seg(5 * w, w)
    d = gf_ref.shape[1]
    gf_ref[...] = jax.nn.sigmoid(seg(6 * w, d))
    gd_ref[...] = jax.nn.sigmoid(seg(6 * w + d, d))
    fl = jnp.dot(xn, wfl_ref[...], preferred_element_type=F32) + bfl_ref[...]
    lf_ref[...] = _log_sigmoid(fl)


def _in_projection(x2d, g, w_main, w_fl, b_fl, ones_bd, gains, tm, seq):
    n, d = x2d.shape
    w = FOX_WIDTH
    tm = min(tm, n)
    assert n % tm == 0
    row = lambda i: (i, 0)
    const = lambda i: (0, 0)
    out_shape = (
        jax.ShapeDtypeStruct((n, w), BF16), jax.ShapeDtypeStruct((n, w), F32),
        jax.ShapeDtypeStruct((n, w), F32), jax.ShapeDtypeStruct((n, LANES), F32),
        jax.ShapeDtypeStruct((n, w), BF16), jax.ShapeDtypeStruct((n, w), F32),
        jax.ShapeDtypeStruct((n, w), F32), jax.ShapeDtypeStruct((n, d), F32),
        jax.ShapeDtypeStruct((n, d), F32),
    )
    out_specs = tuple(pl.BlockSpec((tm, s.shape[1]), row) for s in out_shape)
    if seq % tm == 0:
        tiles = seq // tm
        t_shape = jax.ShapeDtypeStruct((n // seq, w, seq), F32)
        t_spec = pl.BlockSpec((1, w, tm), lambda i: (i // tiles, 0, i % tiles))
        out_shape += (t_shape, t_shape)
        out_specs += (t_spec, t_spec)
    return pl.pallas_call(
        _proj_kernel,
        grid=(n // tm,),
        in_specs=[
            pl.BlockSpec((tm, d), row),
            pl.BlockSpec((1, d), const),
            pl.BlockSpec(w_main.shape, const),
            pl.BlockSpec(w_fl.shape, const),
            pl.BlockSpec((1, LANES), const),
            pl.BlockSpec(ones_bd.shape, const),
            pl.BlockSpec(gains.shape, const),
        ],
        out_specs=out_specs,
        out_shape=out_shape,
        compiler_params=_cparams(("arbitrary",)),
        name="in_projection",
    )(x2d, g, w_main, w_fl, b_fl, ones_bd, gains)


def _scan_kernel(x_ref, tri_ref, o_ref, carry_sc):
    @pl.when(pl.program_id(1) == 0)
    def _():
        carry_sc[...] = jnp.zeros_like(carry_sc)

    x = x_ref[0]
    h = x.shape[0]
    x3 = jnp.concatenate(_split3(x), axis=0)
    y3 = jnp.dot(x3, tri_ref[...], preferred_element_type=F32)
    y = y3[0:h] + y3[h:2 * h] + y3[2 * h:3 * h] + carry_sc[:, 0:1]
    o_ref[0] = y
    carry_sc[...] = jnp.broadcast_to(y[:, -1:], carry_sc.shape)


def _prefix_sum_time(xt, tl):
    b, h, t = xt.shape
    tl = min(tl, t)
    assert t % tl == 0
    tri = (jnp.arange(tl)[:, None] <= jnp.arange(tl)[None, :]).astype(BF16)
    return pl.pallas_call(
        _scan_kernel,
        grid=(b, t // tl),
        in_specs=[pl.BlockSpec((1, h, tl), lambda i, j: (i, 0, j)),
                  pl.BlockSpec((tl, tl), lambda i, j: (0, 0))],
        out_specs=pl.BlockSpec((1, h, tl), lambda i, j: (i, 0, j)),
        out_shape=jax.ShapeDtypeStruct((b, h, t), F32),
        scratch_shapes=[pltpu.VMEM((h, LANES), F32)],
        compiler_params=_cparams(("arbitrary", "arbitrary")),
        name="logf_prefix_sum",
    )(xt, tri)


def _lane_tile(x, n):
    return x if n == 1 else jnp.concatenate([x] * n, axis=1)


def _online_softmax(s, m_ref, l_ref, idx):
    tk = s.shape[1]
    m_prev = m_ref[idx]
    m_next = jnp.maximum(m_prev, jnp.max(s, axis=1, keepdims=True))
    p = jnp.exp2(s - _lane_tile(m_next, tk // LANES))
    alpha = jnp.exp2(m_prev - m_next)
    l_ref[idx] = alpha * l_ref[idx] + jnp.sum(p, axis=1, keepdims=True)
    m_ref[idx] = m_next
    return p, alpha


def _half_mask(shape_lanes):
    lane = lax.broadcasted_iota(jnp.int32, (1, shape_lanes), 1)
    return lane < HEAD_DIM


_NT = (((1,), (1,)), ((), ()))


FOX_BIAS_LANES = 8


def _fox_bias_operands(c):
    b, t, h = c.shape
    to_bf16 = lambda x: lax.reduce_precision(x, exponent_bits=8, mantissa_bits=7)
    c1 = to_bf16(c)
    c2 = to_bf16(c - c1)
    c3 = to_bf16(c - c1 - c2)
    pieces = jnp.stack([c1, c2, c3], axis=-1).astype(BF16)
    ones = jnp.ones_like(pieces)
    pad = jnp.zeros(pieces.shape[:-1] + (FOX_BIAS_LANES - 6,), BF16)
    qx = jnp.concatenate([pieces, ones, pad], axis=-1).reshape(b, t, h * FOX_BIAS_LANES)
    kx = jnp.concatenate([ones, -pieces, pad], axis=-1).reshape(b, t, h * FOX_BIAS_LANES)
    fill = jnp.zeros((b, t, LANES - h * FOX_BIAS_LANES), BF16)
    return jnp.concatenate([qx, fill], axis=-1), jnp.concatenate([kx, fill], axis=-1)


def _causal_pairs(nq):
    pairs = [(i, j) for i in range(nq) for j in range(i + 1)]
    return (jnp.asarray([p[0] for p in pairs], jnp.int32), jnp.asarray([p[1] for p in pairs], jnp.int32))


def _fox_kernel(it_ref, jt_ref, q_ref, k_ref, v_ref, qx_ref, kx_ref, o_ref, qaug_sc, m_sc, l_sc, acc_sc):
    i = it_ref[pl.program_id(1)]
    j = jt_ref[pl.program_id(1)]
    tq = q_ref.shape[1]
    tk = k_ref.shape[1]
    first_half = _half_mask(LANES)

    @pl.when(j == 0)
    def _():
        m_sc[...] = jnp.full_like(m_sc, -jnp.inf)
        l_sc[...] = jnp.zeros_like(l_sc)
        acc_sc[...] = jnp.zeros_like(acc_sc)
        qx = qx_ref[0]
        lane_head = lax.broadcasted_iota(jnp.int32, (1, LANES), 1) // FOX_BIAS_LANES
        for h in range(N_FOX_HEADS):
            q2 = q_ref[0, :, (h // 2) * LANES:(h // 2 + 1) * LANES]
            half = first_half if h % 2 == 0 else jnp.logical_not(first_half)
            qaug_sc[h, :, :LANES] = jnp.where(half, q2, jnp.zeros_like(q2))
            qaug_sc[h, :, LANES:] = jnp.where(lane_head == h, qx, jnp.zeros_like(qx))

    def step(masked):
        if masked:
            row = lax.broadcasted_iota(jnp.int32, (tq, tk), 0)
            col = lax.broadcasted_iota(jnp.int32, (tq, tk), 1)
            keep = col <= row
        kx = kx_ref[0]
        for hp in range(N_FOX_HEADS // 2):
            cols = slice(hp * LANES, (hp + 1) * LANES)
            kaug = jnp.concatenate([k_ref[0, :, cols].astype(BF16), kx], axis=1)
            v2 = v_ref[0, :, cols].astype(BF16)
            pv = []
            alphas = []
            for sub in range(2):
                h = 2 * hp + sub
                s = lax.dot_general(qaug_sc[h], kaug, _NT, preferred_element_type=F32)
                if masked:
                    s = jnp.where(keep, s, -jnp.inf)
                p, alpha = _online_softmax(s, m_sc, l_sc, h)
                pv.append(jnp.dot(p.astype(BF16), v2, preferred_element_type=F32))
                alphas.append(alpha)
            acc_sc[hp] = (jnp.where(first_half, alphas[0], alphas[1]) * acc_sc[hp]
                          + jnp.where(first_half, pv[0], pv[1]))

    @pl.when(j < i)
    def _():
        step(False)

    @pl.when(j == i)
    def _():
        step(True)
        for hp in range(N_FOX_HEADS // 2):
            l2 = jnp.where(first_half, l_sc[2 * hp], l_sc[2 * hp + 1])
            o_ref[0, :, hp * LANES:(hp + 1) * LANES] = (acc_sc[hp] / l2).astype(o_ref.dtype)


def _fox_attention(q, k, v, c, tq):
    b, t, w = q.shape
    tq = min(tq, t)
    assert t % tq == 0 and tq % LANES == 0
    nq = t // tq
    qx, kx = _fox_bias_operands(c * LOG2E)
    itab, jtab = _causal_pairs(nq)
    qmap = lambda bi, p, it, jt: (bi, it[p], 0)
    kmap = lambda bi, p, it, jt: (bi, jt[p], 0)
    grid_spec = pltpu.PrefetchScalarGridSpec(
        num_scalar_prefetch=2,
        grid=(b, itab.shape[0]),
        in_specs=[
            pl.BlockSpec((1, tq, w), qmap),
            pl.BlockSpec((1, tq, w), kmap),
            pl.BlockSpec((1, tq, w), kmap),
            pl.BlockSpec((1, tq, LANES), qmap),
            pl.BlockSpec((1, tq, LANES), kmap),
        ],
        out_specs=pl.BlockSpec((1, tq, w), qmap),
        scratch_shapes=[pltpu.VMEM((N_FOX_HEADS, tq, 2 * LANES), BF16),
                        pltpu.VMEM((N_FOX_HEADS, tq, LANES), F32),
                        pltpu.VMEM((N_FOX_HEADS, tq, LANES), F32),
                        pltpu.VMEM((N_FOX_HEADS // 2, tq, LANES), F32)],
    )
    return pl.pallas_call(
        _fox_kernel,
        grid_spec=grid_spec,
        out_shape=jax.ShapeDtypeStruct((b, t, w), BF16),
        compiler_params=_cparams(("arbitrary", "arbitrary")),
        name="fox_attention",
    )(itab, jtab, q, k, v, qx, kx)


def _diff_finalize(acc1, l1, acc2, l2, lam, g_out, out_scale):
    o = acc1 / l1 - lam * (acc2 / l2)
    ms = jnp.mean(o * o, axis=-1, keepdims=True)
    return o * lax.rsqrt(ms + RMS_EPS) * g_out * out_scale


def _diff_kernel(it_ref, jt_ref, lam_ref, q_ref, k_ref, v_ref, bias_ref, go_ref, o_ref, qm_sc, m_sc, l_sc, acc_sc, *,
                 out_scale):
    i = it_ref[pl.program_id(1)]
    j = jt_ref[pl.program_id(1)]

    @pl.when(j == 0)
    def _():
        m_sc[...] = jnp.full_like(m_sc, -jnp.inf)
        l_sc[...] = jnp.zeros_like(l_sc)
        acc_sc[...] = jnp.zeros_like(acc_sc)
        first_half = _half_mask(LANES)
        for idx in range(2 * N_DIFF_HEADS):
            q2 = q_ref[0, :, (idx // 2) * LANES:(idx // 2 + 1) * LANES]
            half = first_half if idx % 2 == 0 else jnp.logical_not(first_half)
            qm_sc[idx] = jnp.where(half, q2, jnp.zeros_like(q2))

    def step(bias_idx):
        for h in range(N_DIFF_HEADS):
            cols = slice(h * LANES, (h + 1) * LANES)
            k2 = k_ref[0, :, cols].astype(BF16)
            v2 = v_ref[0, :, cols].astype(BF16)
            for c in range(2):
                idx = 2 * h + c
                s = lax.dot_general(qm_sc[idx], k2, _NT, preferred_element_type=F32)
                if bias_idx is not None:
                    s = s + bias_ref[bias_idx, h]
                p, alpha = _online_softmax(s, m_sc, l_sc, idx)
                acc_sc[idx] = alpha * acc_sc[idx] + jnp.dot(p.astype(BF16), v2, preferred_element_type=F32)

    @pl.when(j < i - 1)
    def _():
        step(None)

    @pl.when(j == i - 1)
    def _():
        step(1)

    @pl.when(j == i)
    def _():
        step(0)
        lam = lam_ref[0]
        for h in range(N_DIFF_HEADS):
            o = _diff_finalize(acc_sc[2 * h], l_sc[2 * h], acc_sc[2 * h + 1], l_sc[2 * h + 1],
                               lam, go_ref[...], out_scale)
            o_ref[0, :, h * LANES:(h + 1) * LANES] = o.astype(o_ref.dtype)


def _t5_bucket_1d(n):
    max_exact = N_BUCKETS // 2
    nf = jnp.maximum(n, 1).astype(F32)
    large = max_exact + (jnp.log(nf / max_exact) / math.log(MAX_DISTANCE / max_exact)
                         * (N_BUCKETS - max_exact)).astype(jnp.int32)
    large = jnp.minimum(large, N_BUCKETS - 1)
    return jnp.where(n < max_exact, n, large)


def _rel_bias_lookup(rel_table, dist):
    rel = (rel_table.astype(F32) - rel_table[N_BUCKETS - 1].astype(F32)[None, :]) * LOG2E
    bucket = _t5_bucket_1d(jnp.maximum(dist, 0))
    out = jnp.zeros((rel.shape[1],) + dist.shape, F32)
    for b in range(N_BUCKETS - 1):
        out = jnp.where(bucket[None] == b, rel[b].reshape((-1,) + (1,) * dist.ndim), out)
    return out


def _rel_bias_tile(rel_table, dist, valid):
    return jnp.where(valid[None], _rel_bias_lookup(rel_table, dist), -jnp.inf)


def _diff_attention(q, k, v, rel_table, lam, g_out, out_scale, tq):
    b, t, w = q.shape
    tq = min(tq, t)
    assert t % tq == 0 and tq >= MAX_DISTANCE and tq % LANES == 0
    nq = t // tq
    r = jnp.arange(tq, dtype=jnp.int32)
    d0 = r[:, None] - r[None, :]
    bias = jnp.stack([_rel_bias_tile(rel_table, d0, d0 >= 0),
                      _rel_bias_tile(rel_table, d0 + tq, jnp.ones_like(d0, dtype=bool))])
    itab, jtab = _causal_pairs(nq)
    qmap = lambda bi, p, it, jt: (bi, it[p], 0)
    kmap = lambda bi, p, it, jt: (bi, jt[p], 0)
    grid_spec = pltpu.PrefetchScalarGridSpec(
        num_scalar_prefetch=2,
        grid=(b, itab.shape[0]),
        in_specs=[
            pl.BlockSpec(memory_space=pltpu.SMEM),
            pl.BlockSpec((1, tq, w), qmap),
            pl.BlockSpec((1, tq, w), kmap),
            pl.BlockSpec((1, tq, w), kmap),
            pl.BlockSpec(bias.shape, lambda bi, p, it, jt: (0, 0, 0, 0)),
            pl.BlockSpec((1, LANES), lambda bi, p, it, jt: (0, 0)),
        ],
        out_specs=pl.BlockSpec((1, tq, w), qmap),
        scratch_shapes=[pltpu.VMEM((2 * N_DIFF_HEADS, tq, LANES), BF16),
                        pltpu.VMEM((2 * N_DIFF_HEADS, tq, LANES), F32),
                        pltpu.VMEM((2 * N_DIFF_HEADS, tq, LANES), F32),
                        pltpu.VMEM((2 * N_DIFF_HEADS, tq, LANES), F32)],
    )
    return pl.pallas_call(
        functools.partial(_diff_kernel, out_scale=out_scale),
        grid_spec=grid_spec,
        out_shape=jax.ShapeDtypeStruct((b, t, w), BF16),
        compiler_params=_cparams(("arbitrary", "arbitrary")),
        name="diff_attention",
    )(itab, jtab, lam, q, k, v, bias, g_out)


def _out_kernel(x_ref, of_ref, od_ref, gf_ref, gd_ref, wf_ref, wd_ref, wo_ref, g2_ref, wr_ref, br_ref,
                y_ref, h2_ref, lg_ref):
    merged = (gf_ref[...] * jnp.dot(of_ref[...], wf_ref[...], preferred_element_type=F32)
              + gd_ref[...] * jnp.dot(od_ref[...], wd_ref[...], preferred_element_type=F32))
    y = x_ref[...] + jnp.dot(merged.astype(BF16), wo_ref[...], preferred_element_type=F32)
    y_ref[...] = y
    ms = jnp.mean(y * y, axis=-1, keepdims=True)
    h2 = y * lax.rsqrt(ms + RMS_EPS) * g2_ref[...]
    h2_ref[...] = h2
    lg_ref[...] = jnp.dot(h2.astype(BF16), wr_ref[...], preferred_element_type=F32) + br_ref[...]


def _out_projection(x2d, o_f, o_d, gate_f, gate_d, w_fu, w_du, w_o, g2, w_r, b_r, tm):
    n, d = x2d.shape
    tm = min(tm, n)
    assert n % tm == 0
    row = lambda i: (i, 0)
    const = lambda i: (0, 0)
    return pl.pallas_call(
        _out_kernel,
        grid=(n // tm,),
        in_specs=[
            pl.BlockSpec((tm, d), row), pl.BlockSpec((tm, o_f.shape[1]), row),
            pl.BlockSpec((tm, o_d.shape[1]), row), pl.BlockSpec((tm, d), row), pl.BlockSpec((tm, d), row),
            pl.BlockSpec(w_fu.shape, const), pl.BlockSpec(w_du.shape, const), pl.BlockSpec(w_o.shape, const),
            pl.BlockSpec((1, d), const), pl.BlockSpec(w_r.shape, const), pl.BlockSpec((1, LANES), const),
        ],
        out_specs=(pl.BlockSpec((tm, d), row), pl.BlockSpec((tm, d), row), pl.BlockSpec((tm, LANES), row)),
        out_shape=(jax.ShapeDtypeStruct((n, d), F32), jax.ShapeDtypeStruct((n, d), F32),
                   jax.ShapeDtypeStruct((n, LANES), F32)),
        compiler_params=_cparams(("arbitrary",)),
        name="out_projection",
    )(x2d, o_f, o_d, gate_f, gate_d, w_fu, w_du, w_o, g2, w_r, b_r)


def _deinterleave_kernel(w_ref, perm_ref, g_ref, u_ref):
    w = w_ref[0].astype(BF16)
    r = jnp.dot(w, perm_ref[...], preferred_element_type=F32).astype(BF16)
    half = r.shape[1] // 2
    g_ref[0] = r[:, :half]
    u_ref[0] = r[:, half:]


def _deinterleave_gate_up(w_gu, tn):
    e, d, f2 = w_gu.shape
    assert f2 % tn == 0
    half = tn // 2
    col = jnp.arange(tn)
    perm = (jnp.where(col % 2 == 0, col // 2, half + col // 2)[:, None] == col[None, :]).astype(BF16)
    out = jax.ShapeDtypeStruct((e, d, f2 // 2), BF16)
    return pl.pallas_call(
        _deinterleave_kernel,
        grid=(e, f2 // tn),
        in_specs=[pl.BlockSpec((1, d, tn), lambda i, j: (i, 0, j)),
                  pl.BlockSpec((tn, tn), lambda i, j: (0, 0))],
        out_specs=(pl.BlockSpec((1, d, half), lambda i, j: (i, 0, j)),
                   pl.BlockSpec((1, d, half), lambda i, j: (i, 0, j))),
        out_shape=(out, out),
        compiler_params=_cparams(("arbitrary", "arbitrary")),
        name="deinterleave_gate_up",
    )(w_gu, perm)


def _route_kernel(lg_ref, tril_ref, idx_ref, gate_ref, rank_ref, cnt_ref, carry_sc):
    @pl.when(pl.program_id(0) == 0)
    def _():
        carry_sc[...] = jnp.zeros_like(carry_sc)

    x = lg_ref[...]
    lane = lax.broadcasted_iota(jnp.int32, x.shape, 1)
    lane_f = lane.astype(F32)
    x = jnp.where(lane < N_EXPERTS, x, -jnp.inf)
    carry = carry_sc[0:1, :]
    vals, idxs, ranks = [], [], []
    for _ in range(TOP_K):
        m = jnp.max(x, axis=1, keepdims=True)
        idx = jnp.min(jnp.where(x == m, lane_f, float(LANES)), axis=1, keepdims=True)
        onehot = lane_f == idx
        oh = jnp.where(onehot, 1.0, 0.0)
        before = jnp.dot(tril_ref[...], oh.astype(BF16), preferred_element_type=F32)
        ranks.append(jnp.sum(jnp.where(onehot, before + carry, 0.0), axis=1, keepdims=True))
        carry = carry + jnp.sum(oh, axis=0, keepdims=True)
        x = jnp.where(onehot, -jnp.inf, x)
        vals.append(m)
        idxs.append(idx)
    carry_sc[0:1, :] = carry
    cnt_ref[...] = jnp.broadcast_to(carry, cnt_ref.shape)
    e = [jnp.exp(v - vals[0]) for v in vals]
    tot = e[0]
    for ek in e[1:]:
        tot = tot + ek
    gates = jnp.zeros(x.shape, F32)
    idx_out = jnp.zeros(x.shape, F32)
    rank_out = jnp.zeros(x.shape, F32)
    for k in range(TOP_K):
        gates = jnp.where(lane == k, e[k] / tot, gates)
        idx_out = jnp.where(lane == k, idxs[k], idx_out)
        rank_out = jnp.where(lane == k, ranks[k], rank_out)
    gate_ref[...] = gates
    idx_ref[...] = idx_out.astype(jnp.int32)
    rank_ref[...] = rank_out.astype(jnp.int32)


def _route(logits, tm):
    n = logits.shape[0]
    tm = min(tm, n)
    assert n % tm == 0
    tril = (jnp.arange(tm)[None, :] < jnp.arange(tm)[:, None]).astype(BF16)
    row = lambda i: (i, 0)
    idx, gates, rank, cnt = pl.pallas_call(
        _route_kernel,
        grid=(n // tm,),
        in_specs=[pl.BlockSpec((tm, LANES), row), pl.BlockSpec((tm, tm), lambda i: (0, 0))],
        out_specs=(pl.BlockSpec((tm, LANES), row), pl.BlockSpec((tm, LANES), row),
                   pl.BlockSpec((tm, LANES), row), pl.BlockSpec((8, LANES), lambda i: (0, 0))),
        out_shape=(jax.ShapeDtypeStruct((n, LANES), jnp.int32), jax.ShapeDtypeStruct((n, LANES), F32),
                   jax.ShapeDtypeStruct((n, LANES), jnp.int32), jax.ShapeDtypeStruct((8, LANES), F32)),
        scratch_shapes=[pltpu.VMEM((8, LANES), F32)],
        compiler_params=_cparams(("arbitrary",)),
        name="moe_route",
    )(logits, tril)
    return (idx[:, :TOP_K], gates[:, :TOP_K], rank[:, :TOP_K],
            cnt[0, :N_EXPERTS].astype(jnp.int32))


def _dispatch_kernel(slot_ref, pad_ref, h_ref, xs_out, sem):
    rows = h_ref.shape[0]
    n_pad = pad_ref.shape[0]

    @pl.when(pl.program_id(0) == 0)
    def _():
        def pad_body(r, carry):
            pltpu.make_async_copy(h_ref.at[pl.ds(0, 1)], xs_out.at[pl.ds(pad_ref[r], 1)], sem).start()
            return carry
        lax.fori_loop(0, n_pad, pad_body, 0, unroll=8)
        for _ in range(n_pad // rows):
            pltpu.make_async_copy(h_ref, xs_out.at[pl.ds(0, rows)], sem).wait()

    def body(r, carry):
        for k in range(TOP_K):
            s = slot_ref[0, 0, r * TOP_K + k]
            pltpu.make_async_copy(h_ref.at[pl.ds(r, 1)], xs_out.at[pl.ds(s, 1)], sem).start()
        return carry

    lax.fori_loop(0, rows, body, 0, unroll=2)
    for _ in range(TOP_K):
        pltpu.make_async_copy(h_ref, xs_out.at[pl.ds(0, rows)], sem).wait()


def _dispatch(h2, slot, pad_slots, n_slots, tm):
    n, d = h2.shape
    tm = min(tm, n)
    assert n % tm == 0 and pad_slots.shape[0] % tm == 0
    slot3 = slot.reshape(n // tm, 1, tm * TOP_K)
    return pl.pallas_call(
        _dispatch_kernel,
        grid=(n // tm,),
        in_specs=[pl.BlockSpec((1, 1, tm * TOP_K), lambda i: (i, 0, 0), memory_space=pltpu.SMEM),
                  pl.BlockSpec(memory_space=pltpu.SMEM),
                  pl.BlockSpec((tm, d), lambda i: (i, 0))],
        out_specs=pl.BlockSpec(memory_space=pl.ANY),
        out_shape=jax.ShapeDtypeStruct((n_slots, d), F32),
        scratch_shapes=[pltpu.SemaphoreType.DMA(())],
        compiler_params=_cparams(("arbitrary",)),
        name="moe_dispatch",
    )(slot3, pad_slots, h2)


def _row_gather_wait(src_hbm, n_rows, dst, sem):
    pltpu.make_async_copy(src_hbm.at[pl.ds(0, n_rows)], dst, sem).wait()


def _moe_kernel(be_ref, nused_ref, x_ref, wg_ref, wu_ref, bg_ref, bu_ref, wd_ref, bd_ref, y_ref):
    i = pl.program_id(0)
    n_used = nused_ref[0]

    @pl.when(i < n_used)
    def _():
        x = x_ref[...].astype(BF16)
        gate = jnp.dot(x, wg_ref[0], preferred_element_type=F32) + bg_ref[0]
        up = jnp.dot(x, wu_ref[0], preferred_element_type=F32) + bu_ref[0]
        gate = jnp.minimum(gate, SWIGLU_LIMIT)
        up = jnp.clip(up, -SWIGLU_LIMIT, SWIGLU_LIMIT)
        glu = gate * jax.nn.sigmoid(SWIGLU_ALPHA * gate)
        act = ((up + 1.0) * glu).astype(BF16)
        y_ref[...] = jnp.dot(act, wd_ref[0], preferred_element_type=F32) + bd_ref[0]

    @pl.when(i >= n_used)
    def _():
        y_ref[...] = jnp.zeros_like(y_ref)


def _expert_ffn(xs, block_expert, n_used, w_g, w_u, b_g, b_u, w_d, b_d, bm):
    n_slots, d = xs.shape
    n_blocks = block_expert.shape[0]
    dff = w_g.shape[2]
    wmap = lambda i, be, nu: (be[i], 0, 0)
    grid_spec = pltpu.PrefetchScalarGridSpec(
        num_scalar_prefetch=2,
        grid=(n_blocks,),
        in_specs=[
            pl.BlockSpec((bm, d), lambda i, be, nu: (i, 0)),
            pl.BlockSpec((1, d, dff), wmap), pl.BlockSpec((1, d, dff), wmap),
            pl.BlockSpec((1, 1, dff), wmap), pl.BlockSpec((1, 1, dff), wmap),
            pl.BlockSpec((1, dff, d), wmap), pl.BlockSpec((1, 1, d), wmap),
        ],
        out_specs=pl.BlockSpec((bm, d), lambda i, be, nu: (i, 0)),
    )
    return pl.pallas_call(
        _moe_kernel,
        grid_spec=grid_spec,
        out_shape=jax.ShapeDtypeStruct((n_slots, d), F32),
        compiler_params=_cparams(("arbitrary",)),
        name="expert_ffn",
    )(block_expert, n_used, xs, w_g, w_u, b_g, b_u, w_d, b_d)


def _combine_kernel(idx_ref, idx_next_ref, yb_hbm, resid_ref, gates_ref, o_ref, buf, sem):
    i = pl.program_id(0)
    n = pl.num_programs(0)
    tc = resid_ref.shape[0]
    slot = lax.rem(i, 2)

    def start(idx, par):
        def body(r, carry):
            for k in range(TOP_K):
                s = idx[0, 0, r * TOP_K + k]
                pltpu.make_async_copy(yb_hbm.at[pl.ds(s, 1)], buf.at[par, pl.ds(k * tc + r, 1)],
                                      sem.at[par]).start()
            return carry
        lax.fori_loop(0, tc, body, 0, unroll=2)

    @pl.when(i == 0)
    def _():
        start(idx_ref, 0)

    @pl.when(i + 1 < n)
    def _():
        start(idx_next_ref, 1 - slot)

    _row_gather_wait(yb_hbm, TOP_K * tc, buf.at[slot], sem.at[slot])
    acc = jnp.zeros(o_ref.shape, F32)
    for k in range(TOP_K):
        acc = acc + buf[slot, k * tc:(k + 1) * tc, :] * gates_ref[:, k:k + 1]
    o_ref[...] = resid_ref[...] + acc


def _combine(yb, slot_tm, gates, resid, tc):
    n, d = resid.shape
    n_tiles = n // tc
    rows = TOP_K * tc
    return pl.pallas_call(
        _combine_kernel,
        grid=(n_tiles,),
        in_specs=[
            pl.BlockSpec((1, 1, rows), lambda i: (i, 0, 0), memory_space=pltpu.SMEM),
            pl.BlockSpec((1, 1, rows), lambda i: (jnp.minimum(i + 1, n_tiles - 1), 0, 0),
                         memory_space=pltpu.SMEM),
            pl.BlockSpec(memory_space=pl.ANY),
            pl.BlockSpec((tc, d), lambda i: (i, 0)),
            pl.BlockSpec((tc, TOP_K), lambda i: (i, 0)),
        ],
        out_specs=pl.BlockSpec((tc, d), lambda i: (i, 0)),
        out_shape=jax.ShapeDtypeStruct((n, d), F32),
        scratch_shapes=[pltpu.VMEM((2, rows, d), F32), pltpu.SemaphoreType.DMA((2,))],
        compiler_params=_cparams(("arbitrary",)),
        name="moe_combine",
    )(slot_tm, slot_tm, yb, resid, gates)


def _moe_layer(y1, h2, logits, moe_w, bm, tc):
    n, d = y1.shape
    w_g, w_u, b_g, b_u, w_d, b_d = moe_w
    bm = min(bm, n)
    tc = min(tc, n)
    idx, gates, rank, counts = _route(logits, ROUTE_ROWS)
    padded = (counts + bm - 1) // bm * bm
    pend = jnp.cumsum(padded)
    pstart = pend - padded
    experts = jnp.arange(N_EXPERTS, dtype=jnp.int32)
    slot = rank + jnp.sum(jnp.where(idx[..., None] == experts, pstart, 0), axis=-1).astype(jnp.int32)
    n_blocks = -(-(n * TOP_K) // bm) + N_EXPERTS
    block_start = jnp.arange(n_blocks, dtype=jnp.int32) * bm
    block_expert = jnp.minimum(jnp.sum(pend[None, :] <= block_start[:, None], axis=1),
                               N_EXPERTS - 1).astype(jnp.int32)
    n_used = (pend[-1] // bm).astype(jnp.int32).reshape(1)
    assert (n * TOP_K) % bm == 0
    r = jnp.arange(bm, dtype=jnp.int32)
    is_pad = r[None, :] < (padded - counts)[:, None]
    tail_rank = (jnp.cumsum(jnp.logical_not(is_pad).reshape(-1)) - 1).reshape(N_EXPERTS, bm)
    pad_slots = jnp.where(is_pad, (pstart + counts)[:, None] + r[None, :], pend[-1] + tail_rank)
    xs = _dispatch(h2, slot, pad_slots.reshape(-1).astype(jnp.int32), n_blocks * bm, DISPATCH_ROWS)
    yb = _expert_ffn(xs, block_expert, n_used, w_g, w_u, b_g, b_u, w_d, b_d, bm)
    return _combine(yb, slot.reshape(n // tc, 1, tc * TOP_K), gates, y1, tc)


def _sample_attn_kernel(pt_ref, lam_ref, qsf_ref, qsd_ref, cncol_ref, cnrow_ref, snew_ref, blast_ref,
                        knf_ref, vnf_ref, knd_ref, vnd_ref, tri_ref, go_ref, *rest, pps, out_scale):
    page_refs = rest[:5 * pps]
    of_ref, od_ref, m_sc, l_sc, accf_sc, accd_sc, carry_sc = rest[5 * pps:]
    kf_refs = page_refs[0 * pps:1 * pps]
    vf_refs = page_refs[1 * pps:2 * pps]
    kd_refs = page_refs[2 * pps:3 * pps]
    vd_refs = page_refs[3 * pps:4 * pps]
    lft_refs = page_refs[4 * pps:5 * pps]
    g = pl.program_id(1)
    n_g = pl.num_programs(1)
    nh = N_FOX_HEADS
    nd_heads = N_DIFF_HEADS
    half_rows = cncol_ref.shape[1] // 2
    tok = half_rows // nh

    @pl.when(g == 0)
    def _():
        m_sc[...] = jnp.full_like(m_sc, -jnp.inf)
        l_sc[...] = jnp.zeros_like(l_sc)
        accf_sc[...] = jnp.zeros_like(accf_sc)
        accd_sc[...] = jnp.zeros_like(accd_sc)
        carry_sc[...] = jnp.zeros_like(carry_sc)

    _NN = (((1,), (0,)), ((), ()))

    def scores(kf_heads, kd_heads, fox_dims):
        rows = [lax.dot_general(qsf_ref[0, h], kf_heads[h], fox_dims, preferred_element_type=F32)
                for h in range(nh)]
        rows += [lax.dot_general(qsd_ref[0, h], kd_heads[h], _NT, preferred_element_type=F32)
                 for h in range(nd_heads)]
        return jnp.concatenate(rows, axis=0)

    def update(s, vf_heads, vd_heads, fox_dims):
        p, alpha = _online_softmax(s, m_sc, l_sc, slice(None))
        pv_f = jnp.concatenate(
            [lax.dot_general(p[h * tok:(h + 1) * tok].astype(BF16), vf_heads[h], fox_dims,
                             preferred_element_type=F32) for h in range(nh)], axis=0)
        pv_d = jnp.concatenate(
            [jnp.dot(p[half_rows + 2 * h * tok:half_rows + 2 * (h + 1) * tok].astype(BF16), vd_heads[h],
                     preferred_element_type=F32) for h in range(nd_heads)], axis=0)
        accf_sc[...] = alpha[:half_rows, :HEAD_DIM] * accf_sc[...] + pv_f
        accd_sc[...] = alpha[half_rows:] * accd_sc[...] + pv_d

    def fox_page_heads(refs):
        return [jnp.concatenate([ref[0, 0, h] for ref in refs], axis=1).astype(BF16) for h in range(nh)]

    def diff_page_heads(refs):
        page = refs[0].shape[2] // nd_heads
        return [jnp.concatenate([ref[0, 0, pl.ds(h, page, stride=nd_heads), :] for ref in refs],
                                axis=0).astype(BF16) for h in range(nd_heads)]

    def past_chunk(with_bias):
        carry = carry_sc[...]
        suf = [None] * pps
        for p in reversed(range(pps)):
            x = lft_refs[p][0]
            x3 = jnp.concatenate(_split3(x), axis=0)
            y3 = jnp.dot(x3, tri_ref[...], preferred_element_type=F32)
            within = y3[0:nh] + y3[nh:2 * nh] + y3[2 * nh:3 * nh]
            suf[p] = within + carry
            carry = carry + (within[:, 0:1] + x[:, 0:1])
        carry_sc[...] = carry
        sufc = jnp.concatenate(suf, axis=1) * LOG2E
        r = sufc.shape[1]
        suf_rows = jnp.concatenate(
            [jnp.broadcast_to(sufc[h:h + 1, :], (tok, r)) for h in range(nh)]
            + [jnp.zeros((half_rows, r), F32)], axis=0)
        s = scores(fox_page_heads(kf_refs), diff_page_heads(kd_refs), _NN)
        s = s + _lane_tile(cncol_ref[0], r // LANES) + suf_rows
        if with_bias:
            s = s + blast_ref[...]
        update(s, fox_page_heads(vf_refs), diff_page_heads(vd_refs), _NT)

    @pl.when(g == 0)
    def _():
        past_chunk(True)

    @pl.when(g > 0)
    def _():
        past_chunk(False)

    @pl.when(g == n_g - 1)
    def _():
        def new_heads(ref, n_heads):
            width = ref.shape[2] // n_heads
            return [ref[0, :, h * width:(h + 1) * width].astype(BF16) for h in range(n_heads)]

        s = scores(new_heads(knf_ref, nh), new_heads(knd_ref, nd_heads), _NT)
        s = s + (cncol_ref[0] - cnrow_ref[0]) + snew_ref[...]
        update(s, new_heads(vnf_ref, nh), new_heads(vnd_ref, nd_heads), _NN)

        l = l_sc[...]
        nf = accf_sc[...] / l[:half_rows, :HEAD_DIM]
        for h in range(nh):
            of_ref[0, :, h * HEAD_DIM:(h + 1) * HEAD_DIM] = nf[h * tok:(h + 1) * tok].astype(of_ref.dtype)

        nd = accd_sc[...] / l[half_rows:]
        lam = lam_ref[0]
        for h in range(N_DIFF_HEADS):
            cols = slice(h * LANES, (h + 1) * LANES)
            a1 = nd[(2 * h) * tok:(2 * h + 1) * tok]
            a2 = nd[(2 * h + 1) * tok:(2 * h + 2) * tok]
            o = a1 - lam * a2
            ms = jnp.mean(o * o, axis=-1, keepdims=True)
            od_ref[0, :, cols] = (o * lax.rsqrt(ms + RMS_EPS) * go_ref[...] * out_scale).astype(od_ref.dtype)


def _sample_attention(qf, qd, kf_new, vf_new, kd_new, vd_new, c_new, caches, page_table, rel_table, lam,
                      g_out, out_scale, pps, layer):
    db, s_new, w = qf.shape
    ck_f, cv_f, ck_d, cv_d, clf_t = caches
    page = ck_f.shape[2]
    ck_f, cv_f = (jnp.transpose(a, (0, 1, 3, 4, 2)) for a in (ck_f, cv_f))
    ck_d, cv_d = (a.reshape(a.shape[0], a.shape[1], page * N_DIFF_HEADS, 2 * HEAD_DIM) for a in (ck_d, cv_d))
    n_pages = page_table.shape[1]
    pps = min(pps, n_pages)
    assert n_pages % pps == 0 and page == LANES and s_new * N_FOX_HEADS * 2 == LANES
    n_g = n_pages // pps
    r = pps * page
    nh = N_FOX_HEADS
    half_rows = nh * s_new

    qsf = jnp.transpose(qf.reshape(db, s_new, nh, HEAD_DIM), (0, 2, 1, 3))
    qd_h = jnp.transpose(qd.reshape(db, s_new, N_DIFF_HEADS, 2 * HEAD_DIM), (0, 2, 1, 3))
    first = (jnp.arange(2 * HEAD_DIM) < HEAD_DIM)
    qsd = jnp.concatenate([jnp.where(first, qd_h, jnp.zeros((), qd.dtype)),
                           jnp.where(first, jnp.zeros((), qd.dtype), qd_h)], axis=2)

    c_new = c_new * LOG2E
    c_rows =jnp.transpose(c_new, (0, 2, 1)).reshape(db, half_rows)
    cn_col = jnp.concatenate([c_rows, jnp.zeros_like(c_rows)], axis=1)[:, :, None]
    cn_col = jnp.broadcast_to(cn_col, (db, LANES, LANES))
    c_keys = jnp.repeat(jnp.transpose(c_new, (0, 2, 1)), s_new, axis=1)
    cn_row = jnp.zeros((db, LANES, LANES), F32).at[:, :half_rows, :s_new].set(c_keys)

    t_of_row = jnp.arange(LANES, dtype=jnp.int32) % s_new
    key = jnp.arange(LANES, dtype=jnp.int32)
    valid_new = (key[None, :] <= t_of_row[:, None]) & (key[None, :] < s_new)
    is_diff = (jnp.arange(LANES) >= half_rows)
    head_of_row = jnp.clip((jnp.arange(LANES) - half_rows) // (2 * s_new), 0, N_DIFF_HEADS - 1)

    def rel_rows(dist):
        per_head = _rel_bias_lookup(rel_table, dist)
        vals = jnp.zeros(dist.shape, F32)
        for h in range(N_DIFF_HEADS):
            vals = jnp.where(head_of_row[:, None] == h, per_head[h], vals)
        return jnp.where(is_diff[:, None], vals, 0.0)

    s_new_tile = jnp.where(valid_new, rel_rows(t_of_row[:, None] - key[None, :]), -jnp.inf)
    key_r = jnp.arange(r, dtype=jnp.int32)
    b_last = rel_rows(r - key_r[None, :] + t_of_row[:, None])

    def pad_rows(x):
        return jnp.pad(x, ((0, 0), (0, LANES - s_new), (0, 0)))

    tri = (jnp.arange(page)[:, None] > jnp.arange(page)[None, :]).astype(BF16)

    bmap = lambda b, g, pt: (b, 0, 0)
    bmap4 = lambda b, g, pt: (b, 0, 0, 0)
    const2 = lambda b, g, pt: (0, 0)

    def page_map(p):
        return lambda b, g, pt: (pt[b, (n_g - 1 - g) * pps + p], 0, 0)

    def cache_map(p, ndim):
        return lambda b, g, pt: (layer, pt[b, (n_g - 1 - g) * pps + p]) + (0,) * (ndim - 2)

    in_specs = [
        pl.BlockSpec(memory_space=pltpu.SMEM),
        pl.BlockSpec((1,) + qsf.shape[1:], bmap4), pl.BlockSpec((1,) + qsd.shape[1:], bmap4),
        pl.BlockSpec((1, LANES, LANES), bmap), pl.BlockSpec((1, LANES, LANES), bmap),
        pl.BlockSpec((LANES, LANES), const2), pl.BlockSpec((LANES, r), const2),
        pl.BlockSpec((1, LANES, w), bmap), pl.BlockSpec((1, LANES, w), bmap),
        pl.BlockSpec((1, LANES, w), bmap), pl.BlockSpec((1, LANES, w), bmap),
        pl.BlockSpec((page, page), const2), pl.BlockSpec((1, LANES), const2),
    ]
    operands = [lam, qsf, qsd, cn_col, cn_row, s_new_tile, b_last,
                pad_rows(kf_new), pad_rows(vf_new), pad_rows(kd_new), pad_rows(vd_new), tri, g_out]
    for arr in (ck_f, cv_f, ck_d, cv_d):
        for p in range(pps):
            in_specs.append(pl.BlockSpec((1, 1) + arr.shape[2:], cache_map(p, arr.ndim)))
            operands.append(arr)
    for p in range(pps):
        in_specs.append(pl.BlockSpec((1, nh, page), page_map(p)))
        operands.append(clf_t)

    grid_spec = pltpu.PrefetchScalarGridSpec(
        num_scalar_prefetch=1,
        grid=(db, n_g),
        in_specs=in_specs,
        out_specs=(pl.BlockSpec((1, s_new, w), bmap), pl.BlockSpec((1, s_new, w), bmap)),
        scratch_shapes=[pltpu.VMEM((LANES, LANES), F32), pltpu.VMEM((LANES, LANES), F32),
                        pltpu.VMEM((half_rows, HEAD_DIM), F32), pltpu.VMEM((half_rows, 2 * HEAD_DIM), F32),
                        pltpu.VMEM((nh, LANES), F32)],
    )
    return pl.pallas_call(
        functools.partial(_sample_attn_kernel, pps=pps, out_scale=out_scale),
        grid_spec=grid_spec,
        out_shape=(jax.ShapeDtypeStruct((db, s_new, w), BF16), jax.ShapeDtypeStruct((db, s_new, w), BF16)),
        compiler_params=_cparams(("arbitrary", "arbitrary")),
        name="sample_attention",
    )(page_table, *operands)


PROJ_ROWS = 256
ATTN_BLOCK = 512
SCAN_BLOCK = 512
MOE_ROWS = 256
COMBINE_ROWS = 128
ROUTE_ROWS = 512
DISPATCH_ROWS = 512
DEINTERLEAVE_COLS = 512
PAGES_PER_STEP = 8


def _split_points(d_model):
    sizes = (FOX_WIDTH, FOX_WIDTH, FOX_WIDTH, N_FOX_HEADS, DIFF_WIDTH, DIFF_WIDTH, DIFF_WIDTH, d_model)
    pts, acc = [], 0
    for s in sizes:
        acc += s
        pts.append(acc)
    return pts


def _layer_weights(l, d_model, attn_norm_g, w_in, b_forget, fox_q_norm_g, fox_k_norm_g, diff_q_norm_g,
                   diff_k_norm_g, lambda_q1, lambda_k1, lambda_q2, lambda_k2, diff_out_norm_g, w_fox_up,
                   w_diff_up, w_out, ffn_norm_g, w_router, b_router, w_gate_up, b_gate_up, w_down, b_down):
    lambda_init = 0.8 - 0.6 * math.exp(-0.3 * l)
    wq_f, wk_f, wv_f, w_fl, wq_d, wk_d, wv_d, wg_f, wg_d = jnp.split(w_in[l], _split_points(d_model), axis=1)
    w_main = jnp.concatenate([wq_f, wk_f, wv_f, wq_d, wk_d, wv_d, wg_f, wg_d], axis=1).astype(BF16)
    w_fl = jnp.pad(w_fl, ((0, 0), (0, LANES - N_FOX_HEADS))).astype(BF16)
    b_fl = jnp.pad(b_forget[l].astype(F32), (0, LANES - N_FOX_HEADS)).reshape(1, LANES)
    grp = jnp.arange(FOX_WIDTH) // HEAD_DIM
    ones_bd = (grp[:, None] == grp[None, :]).astype(BF16)
    reps = FOX_WIDTH // HEAD_DIM
    gains = jnp.stack([jnp.tile(g[l].astype(F32), reps)
                       for g in (fox_q_norm_g, fox_k_norm_g, diff_q_norm_g, diff_k_norm_g)])
    lam = (jnp.exp(jnp.sum(lambda_q1[l].astype(F32) * lambda_k1[l].astype(F32)))
           - jnp.exp(jnp.sum(lambda_q2[l].astype(F32) * lambda_k2[l].astype(F32))) + lambda_init)
    w_r = jnp.pad(w_router[l], ((0, 0), (0, LANES - N_EXPERTS))).astype(BF16)
    b_r = jnp.pad(b_router[l].astype(F32), (0, LANES - N_EXPERTS)).reshape(1, LANES)
    w_g, w_u = _deinterleave_gate_up(w_gate_up[l], DEINTERLEAVE_COLS)
    bgu = b_gate_up[l].astype(F32)
    moe_w = (w_g, w_u, bgu[:, None, 0::2], bgu[:, None, 1::2],
             w_down[l].astype(BF16), b_down[l].astype(F32)[:, None, :])
    return dict(
        lambda_init=lambda_init, g1=attn_norm_g[l].astype(F32).reshape(1, d_model), w_main=w_main, w_fl=w_fl,
        b_fl=b_fl, ones_bd=ones_bd, gains=gains, lam=lam.reshape(1).astype(F32),
        g_out=diff_out_norm_g[l].astype(F32).reshape(1, LANES),
        w_fu=w_fox_up[l].astype(BF16), w_du=w_diff_up[l].astype(BF16), w_o=w_out[l].astype(BF16),
        g2=ffn_norm_g[l].astype(F32).reshape(1, d_model), w_r=w_r, b_r=b_r, moe_w=moe_w)


def _project(x, lw):
    b, t, d = x.shape
    outs = _in_projection(x.reshape(b * t, d), lw['g1'], lw['w_main'], lw['w_fl'], lw['b_fl'],
                          lw['ones_bd'], lw['gains'], PROJ_ROWS, t)
    qf, kf, vf, lf, qd, kd, vd, gf, gd = outs[:9]
    logf = lf[:, :N_FOX_HEADS].reshape(b, t, N_FOX_HEADS)
    r3 = lambda a: a.reshape(b, t, a.shape[-1])
    if len(outs) > 9:
        rows4 = lambda a: jnp.transpose(a.reshape(b, N_FOX_HEADS, HEAD_DIM, t), (0, 3, 1, 2))
        fox_rows = (rows4(outs[9]), rows4(outs[10]))
    else:
        fox_rows = (kf.reshape(b, t, N_FOX_HEADS, HEAD_DIM), vf.reshape(b, t, N_FOX_HEADS, HEAD_DIM))
    return r3(qf), r3(kf), r3(vf), logf, r3(qd), r3(kd), r3(vd), gf, gd, fox_rows


def _finish_layer(x, o_f, o_d, gf, gd, lw, moe_rows):
    b, t, d = x.shape
    n = b * t
    y1, h2, logits = _out_projection(x.reshape(n, d), o_f.reshape(n, -1), o_d.reshape(n, -1), gf, gd,
                                     lw['w_fu'], lw['w_du'], lw['w_o'], lw['g2'], lw['w_r'], lw['b_r'],
                                     PROJ_ROWS)
    return _moe_layer(y1, h2, logits, lw['moe_w'], moe_rows, COMBINE_ROWS).reshape(b, t, d)


def _new_rows(fox_rows, logf, kd, vd):
    b, t, _ = kd.shape
    return (fox_rows[0], fox_rows[1], logf,
            kd.reshape(b, t, N_DIFF_HEADS, 2 * HEAD_DIM), vd.reshape(b, t, N_DIFF_HEADS, 2 * HEAD_DIM))


def _prompt_layer(x, lw, rel_table):
    qf, kf, vf, logf, qd, kd, vd, gf, gd, fox_rows = _project(x, lw)
    ct = _prefix_sum_time(jnp.transpose(logf, (0, 2, 1)), SCAN_BLOCK)
    c = jnp.transpose(ct, (0, 2, 1))
    o_f = _fox_attention(qf, kf, vf, c, ATTN_BLOCK)
    o_d = _diff_attention(qd, kd, vd, rel_table, lw['lam'], lw['g_out'], 1.0 - lw['lambda_init'], ATTN_BLOCK)
    return _finish_layer(x, o_f, o_d, gf, gd, lw, MOE_ROWS), _new_rows(fox_rows, logf, kd, vd)


def _sample_layer(x, lw, rel_table, caches, page_table, layer):
    qf, kf, vf, logf, qd, kd, vd, gf, gd, fox_rows = _project(x, lw)
    b, t, _ = logf.shape
    logf_t = jnp.pad(jnp.transpose(logf, (0, 2, 1)), ((0, 0), (0, 0), (0, LANES - t)))
    c = jnp.transpose(_prefix_sum_time(logf_t, LANES)[:, :, :t], (0, 2, 1))
    o_f, o_d = _sample_attention(qf, qd, kf, vf, kd, vd, c, caches, page_table, rel_table, lw['lam'],
                                 lw['g_out'], 1.0 - lw['lambda_init'], PAGES_PER_STEP, layer)
    return _finish_layer(x, o_f, o_d, gf, gd, lw, LANES), _new_rows(fox_rows, logf, kd, vd)


def kernel(x_prompt, x_sample, cache_fox_k, cache_fox_v, cache_fox_logf, cache_diff_k, cache_diff_v, page_table, rel_bias_table, attn_norm_g, w_in, b_forget, fox_q_norm_g, fox_k_norm_g, diff_q_norm_g, diff_k_norm_g, lambda_q1, lambda_k1, lambda_q2, lambda_k2, diff_out_norm_g, w_fox_up, w_diff_up, w_out, ffn_norm_g, w_router, b_router, w_gate_up, b_gate_up, w_down, b_down):
    depth = w_in.shape[0]
    d_model = x_prompt.shape[-1]
    yp, ys = x_prompt, x_sample
    rows_p, rows_s = [], []
    for l in range(depth):
        lw = _layer_weights(l, d_model, attn_norm_g, w_in, b_forget, fox_q_norm_g, fox_k_norm_g,
                            diff_q_norm_g, diff_k_norm_g, lambda_q1, lambda_k1, lambda_q2, lambda_k2,
                            diff_out_norm_g, w_fox_up, w_diff_up, w_out, ffn_norm_g, w_router, b_router,
                            w_gate_up, b_gate_up, w_down, b_down)
        yp, rp = _prompt_layer(yp, lw, rel_bias_table)
        caches = (cache_fox_k, cache_fox_v, cache_diff_k, cache_diff_v,
                  jnp.transpose(cache_fox_logf[l].astype(F32), (0, 2, 1)))
        ys, rs = _sample_layer(ys, lw, rel_bias_table, caches, page_table, l)
        rows_p.append(rp)
        rows_s.append(rs)
    stack = lambda rows, i: jnp.stack([r[i] for r in rows])
    return (yp, ys) + tuple(stack(rows_p, i) for i in range(5)) + tuple(stack(rows_s, i) for i in range(5))
```

```python
import functools
import math

import jax
import jax.numpy as jnp
from jax import lax
from jax.experimental import pallas as pl
from jax.experimental.pallas import tpu as pltpu

F32 = jnp.float32
BF16 = jnp.bfloat16

HEAD_DIM = 64
N_FOX_HEADS = 8
N_DIFF_HEADS = 4
FOX_WIDTH = N_FOX_HEADS * HEAD_DIM
DIFF_WIDTH = N_DIFF_HEADS * 2 * HEAD_DIM
N_BUCKETS = 32
MAX_DISTANCE = 128
N_EXPERTS = 32
TOP_K = 4
SWIGLU_LIMIT = 7.0
SWIGLU_ALPHA = 1.702
RMS_EPS = 1e-6
ATTN_SCALE = HEAD_DIM ** -0.5
LOG2E = math.log2(math.e)
Q_SCALE = ATTN_SCALE * LOG2E
LANES = 128
VMEM_LIMIT = 56 * 1024 * 1024


def _cparams(sem):
    return pltpu.CompilerParams(dimension_semantics=sem, vmem_limit_bytes=VMEM_LIMIT)


def _split3(x):
    x1 = x.astype(BF16)
    r1 = x - x1.astype(F32)
    x2 = r1.astype(BF16)
    x3 = (r1 - x2.astype(F32)).astype(BF16)
    return x1, x2, x3


def _log_sigmoid(x):
    return -(jnp.maximum(-x, 0.0) + jnp.log1p(jnp.exp(-jnp.abs(x))))


def _proj_kernel(x_ref, g_ref, w_ref, wfl_ref, bfl_ref, ones_ref, gains_ref,
                 qf_ref, kf_ref, vf_ref, lf_ref, qd_ref, kd_ref, vd_ref, gf_ref, gd_ref, *t_refs):
    x = x_ref[...]
    ms = jnp.mean(x * x, axis=-1, keepdims=True)
    xn = (x * lax.rsqrt(ms + RMS_EPS) * g_ref[...]).astype(BF16)

    def seg(lo, width):
        return jnp.dot(xn, w_ref[:, lo:lo + width], preferred_element_type=F32)

    ones_bd = ones_ref[...]

    def head_norm(p, gain):
        sq = p * p
        hi = sq.astype(BF16)
        lo = (sq - hi.astype(F32)).astype(BF16)
        ss = (jnp.dot(hi, ones_bd, preferred_element_type=F32)
              + jnp.dot(lo, ones_bd, preferred_element_type=F32))
        return p * lax.rsqrt(ss * (1.0 / HEAD_DIM) + RMS_EPS) * gain

    w = FOX_WIDTH
    qf_ref[...] = (head_norm(seg(0, w), gains_ref[0:1, :]) * Q_SCALE).astype(BF16)
    kf = head_norm(seg(w, w), gains_ref[1:2, :])
    vf = seg(2 * w, w)
    kf_ref[...] = kf
    vf_ref[...] = vf
    if t_refs:
        t_refs[0][0] = kf.T
        t_refs[1][0] = vf.T
    qd_ref[...] = (head_norm(seg(3 * w, w), gains_ref[2:3, :]) * Q_SCALE).astype(BF16)
    kd_ref[...] = head_norm(seg(4 * w, w), gains_ref[3:4, :])
    vd_ref[...] = seg(5 * w, w)
    d = gf_ref.shape[1]
    gf_ref[...] = jax.nn.sigmoid(seg(6 * w, d))
    gd_ref[...] = jax.nn.sigmoid(seg(6 * w + d, d))
    fl = jnp.dot(xn, wfl_ref[...], preferred_element_type=F32) + bfl_ref[...]
    lf_ref[...] = _log_sigmoid(fl)


def _in_projection(x2d, g, w_main, w_fl, b_fl, ones_bd, gains, tm, seq):
    n, d = x2d.shape
    w = FOX_WIDTH
    tm = min(tm, n)
    assert n % tm == 0
    row = lambda i: (i, 0)
    const = lambda i: (0, 0)
    out_shape = (
        jax.ShapeDtypeStruct((n, w), BF16), jax.ShapeDtypeStruct((n, w), F32),
        jax.ShapeDtypeStruct((n, w), F32), jax.ShapeDtypeStruct((n, LANES), F32),
        jax.ShapeDtypeStruct((n, w), BF16), jax.ShapeDtypeStruct((n, w), F32),
        jax.ShapeDtypeStruct((n, w), F32), jax.ShapeDtypeStruct((n, d), F32),
        jax.ShapeDtypeStruct((n, d), F32),
    )
    out_specs = tuple(pl.BlockSpec((tm, s.shape[1]), row) for s in out_shape)
    if seq % tm == 0:
        tiles = seq // tm
        t_shape = jax.ShapeDtypeStruct((n // seq, w, seq), F32)
        t_spec = pl.BlockSpec((1, w, tm), lambda i: (i // tiles, 0, i % tiles))
        out_shape += (t_shape, t_shape)
        out_specs += (t_spec, t_spec)
    return pl.pallas_call(
        _proj_kernel,
        grid=(n // tm,),
        in_specs=[
            pl.BlockSpec((tm, d), row),
            pl.BlockSpec((1, d), const),
            pl.BlockSpec(w_main.shape, const),
            pl.BlockSpec(w_fl.shape, const),
            pl.BlockSpec((1, LANES), const),
            pl.BlockSpec(ones_bd.shape, const),
            pl.BlockSpec(gains.shape, const),
        ],
        out_specs=out_specs,
        out_shape=out_shape,
        compiler_params=_cparams(("arbitrary",)),
        name="in_projection",
    )(x2d, g, w_main, w_fl, b_fl, ones_bd, gains)


def _scan_kernel(x_ref, tri_ref, o_ref, carry_sc):
    @pl.when(pl.program_id(1) == 0)
    def _():
        carry_sc[...] = jnp.zeros_like(carry_sc)

    x = x_ref[0]
    h = x.shape[0]
    x3 = jnp.concatenate(_split3(x), axis=0)
    y3 = jnp.dot(x3, tri_ref[...], preferred_element_type=F32)
    y = y3[0:h] + y3[h:2 * h] + y3[2 * h:3 * h] + carry_sc[:, 0:1]
    o_ref[0] = y
    carry_sc[...] = jnp.broadcast_to(y[:, -1:], carry_sc.shape)


def _prefix_sum_time(xt, tl):
    b, h, t = xt.shape
    tl = min(tl, t)
    assert t % tl == 0
    tri = (jnp.arange(tl)[:, None] <= jnp.arange(tl)[None, :]).astype(BF16)
    return pl.pallas_call(
        _scan_kernel,
        grid=(b, t // tl),
        in_specs=[pl.BlockSpec((1, h, tl), lambda i, j: (i, 0, j)),
                  pl.BlockSpec((tl, tl), lambda i, j: (0, 0))],
        out_specs=pl.BlockSpec((1, h, tl), lambda i, j: (i, 0, j)),
        out_shape=jax.ShapeDtypeStruct((b, h, t), F32),
        scratch_shapes=[pltpu.VMEM((h, LANES), F32)],
        compiler_params=_cparams(("arbitrary", "arbitrary")),
        name="logf_prefix_sum",
    )(xt, tri)


def _lane_tile(x, n):
    return x if n == 1 else jnp.concatenate([x] * n, axis=1)


def _online_softmax(s, m_ref, l_ref, idx):
    tk = s.shape[1]
    m_prev = m_ref[idx]
    m_next = jnp.maximum(m_prev, jnp.max(s, axis=1, keepdims=True))
    p = jnp.exp2(s - _lane_tile(m_next, tk // LANES))
    alpha = jnp.exp2(m_prev - m_next)
    l_ref[idx] = alpha * l_ref[idx] + jnp.sum(p, axis=1, keepdims=True)
    m_ref[idx] = m_next
    return p, alpha


def _half_mask(shape_lanes):
    lane = lax.broadcasted_iota(jnp.int32, (1, shape_lanes), 1)
    return lane < HEAD_DIM


_NT = (((1,), (1,)), ((), ()))


FOX_BIAS_LANES = 8


def _fox_bias_operands(c):
    b, t, h = c.shape
    to_bf16 = lambda x: lax.reduce_precision(x, exponent_bits=8, mantissa_bits=7)
    c1 = to_bf16(c)
    c2 = to_bf16(c - c1)
    c3 = to_bf16(c - c1 - c2)
    pieces = jnp.stack([c1, c2, c3], axis=-1).astype(BF16)
    ones = jnp.ones_like(pieces)
    pad = jnp.zeros(pieces.shape[:-1] + (FOX_BIAS_LANES - 6,), BF16)
    qx = jnp.concatenate([pieces, ones, pad], axis=-1).reshape(b, t, h * FOX_BIAS_LANES)
    kx = jnp.concatenate([ones, -pieces, pad], axis=-1).reshape(b, t, h * FOX_BIAS_LANES)
    fill = jnp.zeros((b, t, LANES - h * FOX_BIAS_LANES), BF16)
    return jnp.concatenate([qx, fill], axis=-1), jnp.concatenate([kx, fill], axis=-1)


def _causal_pairs(nq):
    pairs = [(i, j) for i in range(nq) for j in range(i + 1)]
    return (jnp.asarray([p[0] for p in pairs], jnp.int32), jnp.asarray([p[1] for p in pairs], jnp.int32))


def _fox_kernel(it_ref, jt_ref, q_ref, k_ref, v_ref, qx_ref, kx_ref, o_ref, qaug_sc, m_sc, l_sc, acc_sc):
    i = it_ref[pl.program_id(1)]
    j = jt_ref[pl.program_id(1)]
    tq = q_ref.shape[1]
    tk = k_ref.shape[1]
    first_half = _half_mask(LANES)

    @pl.when(j == 0)
    def _():
        m_sc[...] = jnp.full_like(m_sc, -jnp.inf)
        l_sc[...] = jnp.zeros_like(l_sc)
        acc_sc[...] = jnp.zeros_like(acc_sc)
        qx = qx_ref[0]
        lane_head = lax.broadcasted_iota(jnp.int32, (1, LANES), 1) // FOX_BIAS_LANES
        for h in range(N_FOX_HEADS):
            q2 = q_ref[0, :, (h // 2) * LANES:(h // 2 + 1) * LANES]
            half = first_half if h % 2 == 0 else jnp.logical_not(first_half)
            qaug_sc[h, :, :LANES] = jnp.where(half, q2, jnp.zeros_like(q2))
            qaug_sc[h, :, LANES:] = jnp.where(lane_head == h, qx, jnp.zeros_like(qx))

    def step(masked):
        if masked:
            row = lax.broadcasted_iota(jnp.int32, (tq, tk), 0)
            col = lax.broadcasted_iota(jnp.int32, (tq, tk), 1)
            keep = col <= row
        kx = kx_ref[0]
        for hp in range(N_FOX_HEADS // 2):
            cols = slice(hp * LANES, (hp + 1) * LANES)
            kaug = jnp.concatenate([k_ref[0, :, cols].astype(BF16), kx], axis=1)
            v2 = v_ref[0, :, cols].astype(BF16)
            pv = []
            alphas = []
            for sub in range(2):
                h = 2 * hp + sub
                s = lax.dot_general(qaug_sc[h], kaug, _NT, preferred_element_type=F32)
                if masked:
                    s = jnp.where(keep, s, -jnp.inf)
                p, alpha = _online_softmax(s, m_sc, l_sc, h)
                pv.append(jnp.dot(p.astype(BF16), v2, preferred_element_type=F32))
                alphas.append(alpha)
            acc_sc[hp] = (jnp.where(first_half, alphas[0], alphas[1]) * acc_sc[hp]
                          + jnp.where(first_half, pv[0], pv[1]))

    @pl.when(j < i)
    def _():
        step(False)

    @pl.when(j == i)
    def _():
        step(True)
        for hp in range(N_FOX_HEADS // 2):
            l2 = jnp.where(first_half, l_sc[2 * hp], l_sc[2 * hp + 1])
            o_ref[0, :, hp * LANES:(hp + 1) * LANES] = (acc_sc[hp] / l2).astype(o_ref.dtype)


def _fox_attention(q, k, v, c, tq):
    b, t, w = q.shape
    tq = min(tq, t)
    assert t % tq == 0 and tq % LANES == 0
    nq = t // tq
    qx, kx = _fox_bias_operands(c * LOG2E)
    itab, jtab = _causal_pairs(nq)
    qmap = lambda bi, p, it, jt: (bi, it[p], 0)
    kmap = lambda bi, p, it, jt: (bi, jt[p], 0)
    grid_spec = pltpu.PrefetchScalarGridSpec(
        num_scalar_prefetch=2,
        grid=(b, itab.shape[0]),
        in_specs=[
            pl.BlockSpec((1, tq, w), qmap),
            pl.BlockSpec((1, tq, w), kmap),
            pl.BlockSpec((1, tq, w), kmap),
            pl.BlockSpec((1, tq, LANES), qmap),
            pl.BlockSpec((1, tq, LANES), kmap),
        ],
        out_specs=pl.BlockSpec((1, tq, w), qmap),
        scratch_shapes=[pltpu.VMEM((N_FOX_HEADS, tq, 2 * LANES), BF16),
                        pltpu.VMEM((N_FOX_HEADS, tq, LANES), F32),
                        pltpu.VMEM((N_FOX_HEADS, tq, LANES), F32),
                        pltpu.VMEM((N_FOX_HEADS // 2, tq, LANES), F32)],
    )
    return pl.pallas_call(
        _fox_kernel,
        grid_spec=grid_spec,
        out_shape=jax.ShapeDtypeStruct((b, t, w), BF16),
        compiler_params=_cparams(("arbitrary", "arbitrary")),
        name="fox_attention",
    )(itab, jtab, q, k, v, qx, kx)


def _diff_finalize(acc1, l1, acc2, l2, lam, g_out, out_scale):
    o = acc1 / l1 - lam * (acc2 / l2)
    ms = jnp.mean(o * o, axis=-1, keepdims=True)
    return o * lax.rsqrt(ms + RMS_EPS) * g_out * out_scale


def _diff_kernel(it_ref, jt_ref, lam_ref, q_ref, k_ref, v_ref, bias_ref, go_ref, o_ref, qm_sc, m_sc, l_sc, acc_sc, *,
                 out_scale):
    i = it_ref[pl.program_id(1)]
    j = jt_ref[pl.program_id(1)]

    @pl.when(j == 0)
    def _():
        m_sc[...] = jnp.full_like(m_sc, -jnp.inf)
        l_sc[...] = jnp.zeros_like(l_sc)
        acc_sc[...] = jnp.zeros_like(acc_sc)
        first_half = _half_mask(LANES)
        for idx in range(2 * N_DIFF_HEADS):
            q2 = q_ref[0, :, (idx // 2) * LANES:(idx // 2 + 1) * LANES]
            half = first_half if idx % 2 == 0 else jnp.logical_not(first_half)
            qm_sc[idx] = jnp.where(half, q2, jnp.zeros_like(q2))

    def step(bias_idx):
        for h in range(N_DIFF_HEADS):
            cols = slice(h * LANES, (h + 1) * LANES)
            k2 = k_ref[0, :, cols].astype(BF16)
            v2 = v_ref[0, :, cols].astype(BF16)
            for c in range(2):
                idx = 2 * h + c
                s = lax.dot_general(qm_sc[idx], k2, _NT, preferred_element_type=F32)
                if bias_idx is not None:
                    s = s + bias_ref[bias_idx, h]
                p, alpha = _online_softmax(s, m_sc, l_sc, idx)
                acc_sc[idx] = alpha * acc_sc[idx] + jnp.dot(p.astype(BF16), v2, preferred_element_type=F32)

    @pl.when(j < i - 1)
    def _():
        step(None)

    @pl.when(j == i - 1)
    def _():
        step(1)

    @pl.when(j == i)
    def _():
        step(0)
        lam = lam_ref[0]
        for h in range(N_DIFF_HEADS):
            o = _diff_finalize(acc_sc[2 * h], l_sc[2 * h], acc_sc[2 * h + 1], l_sc[2 * h + 1],
                               lam, go_ref[...], out_scale)
            o_ref[0, :, h * LANES:(h + 1) * LANES] = o.astype(o_ref.dtype)


def _t5_bucket_1d(n):
    max_exact = N_BUCKETS // 2
    nf = jnp.maximum(n, 1).astype(F32)
    large = max_exact + (jnp.log(nf / max_exact) / math.log(MAX_DISTANCE / max_exact)
                         * (N_BUCKETS - max_exact)).astype(jnp.int32)
    large = jnp.minimum(large, N_BUCKETS - 1)
    return jnp.where(n < max_exact, n, large)


def _rel_bias_lookup(rel_table, dist):
    rel = (rel_table.astype(F32) - rel_table[N_BUCKETS - 1].astype(F32)[None, :]) * LOG2E
    bucket = _t5_bucket_1d(jnp.maximum(dist, 0))
    hit = bucket[None, ..., None] == jnp.arange(N_BUCKETS, dtype=jnp.int32)
    return jnp.sum(jnp.where(hit, rel.T.reshape((rel.shape[1],) + (1,) * dist.ndim + (N_BUCKETS,)), 0.0), axis=-1)


def _rel_bias_tile(rel_table, dist, valid):
    return jnp.where(valid[None], _rel_bias_lookup(rel_table, dist), -jnp.inf)


def _diff_attention(q, k, v, rel_table, lam, g_out, out_scale, tq):
    b, t, w = q.shape
    tq = min(tq, t)
    assert t % tq == 0 and tq >= MAX_DISTANCE and tq % LANES == 0
    nq = t // tq
    r = jnp.arange(tq, dtype=jnp.int32)
    d0 = r[:, None] - r[None, :]
    bias = jnp.stack([_rel_bias_tile(rel_table, d0, d0 >= 0),
                      _rel_bias_tile(rel_table, d0 + tq, jnp.ones_like(d0, dtype=bool))])
    itab, jtab = _causal_pairs(nq)
    qmap = lambda bi, p, it, jt: (bi, it[p], 0)
    kmap = lambda bi, p, it, jt: (bi, jt[p], 0)
    grid_spec = pltpu.PrefetchScalarGridSpec(
        num_scalar_prefetch=2,
        grid=(b, itab.shape[0]),
        in_specs=[
            pl.BlockSpec(memory_space=pltpu.SMEM),
            pl.BlockSpec((1, tq, w), qmap),
            pl.BlockSpec((1, tq, w), kmap),
            pl.BlockSpec((1, tq, w), kmap),
            pl.BlockSpec(bias.shape, lambda bi, p, it, jt: (0, 0, 0, 0)),
            pl.BlockSpec((1, LANES), lambda bi, p, it, jt: (0, 0)),
        ],
        out_specs=pl.BlockSpec((1, tq, w), qmap),
        scratch_shapes=[pltpu.VMEM((2 * N_DIFF_HEADS, tq, LANES), BF16),
                        pltpu.VMEM((2 * N_DIFF_HEADS, tq, LANES), F32),
                        pltpu.VMEM((2 * N_DIFF_HEADS, tq, LANES), F32),
                        pltpu.VMEM((2 * N_DIFF_HEADS, tq, LANES), F32)],
    )
    return pl.pallas_call(
        functools.partial(_diff_kernel, out_scale=out_scale),
        grid_spec=grid_spec,
        out_shape=jax.ShapeDtypeStruct((b, t, w), BF16),
        compiler_params=_cparams(("arbitrary", "arbitrary")),
        name="diff_attention",
    )(itab, jtab, lam, q, k, v, bias, g_out)


def _out_kernel(x_ref, of_ref, od_ref, gf_ref, gd_ref, wf_ref, wd_ref, wo_ref, g2_ref, wr_ref, br_ref,
                y_ref, h2_ref, lg_ref):
    merged = (gf_ref[...] * jnp.dot(of_ref[...], wf_ref[...], preferred_element_type=F32)
              + gd_ref[...] * jnp.dot(od_ref[...], wd_ref[...], preferred_element_type=F32))
    y = x_ref[...] + jnp.dot(merged.astype(BF16), wo_ref[...], preferred_element_type=F32)
    y_ref[...] = y
    ms = jnp.mean(y * y, axis=-1, keepdims=True)
    h2 = y * lax.rsqrt(ms + RMS_EPS) * g2_ref[...]
    h2_ref[...] = h2
    lg_ref[...] = jnp.dot(h2.astype(BF16), wr_ref[...], preferred_element_type=F32) + br_ref[...]


def _out_projection(x2d, o_f, o_d, gate_f, gate_d, w_fu, w_du, w_o, g2, w_r, b_r, tm):
    n, d = x2d.shape
    tm = min(tm, n)
    assert n % tm == 0
    row = lambda i: (i, 0)
    const = lambda i: (0, 0)
    return pl.pallas_call(
        _out_kernel,
        grid=(n // tm,),
        in_specs=[
            pl.BlockSpec((tm, d), row), pl.BlockSpec((tm, o_f.shape[1]), row),
            pl.BlockSpec((tm, o_d.shape[1]), row), pl.BlockSpec((tm, d), row), pl.BlockSpec((tm, d), row),
            pl.BlockSpec(w_fu.shape, const), pl.BlockSpec(w_du.shape, const), pl.BlockSpec(w_o.shape, const),
            pl.BlockSpec((1, d), const), pl.BlockSpec(w_r.shape, const), pl.BlockSpec((1, LANES), const),
        ],
        out_specs=(pl.BlockSpec((tm, d), row), pl.BlockSpec((tm, d), row), pl.BlockSpec((tm, LANES), row)),
        out_shape=(jax.ShapeDtypeStruct((n, d), F32), jax.ShapeDtypeStruct((n, d), F32),
                   jax.ShapeDtypeStruct((n, LANES), F32)),
        compiler_params=_cparams(("arbitrary",)),
        name="out_projection",
    )(x2d, o_f, o_d, gate_f, gate_d, w_fu, w_du, w_o, g2, w_r, b_r)


def _deinterleave_kernel(w_ref, perm_ref, g_ref, u_ref):
    w = w_ref[0].astype(BF16)
    r = jnp.dot(w, perm_ref[...], preferred_element_type=F32).astype(BF16)
    half = r.shape[1] // 2
    g_ref[0] = r[:, :half]
    u_ref[0] = r[:, half:]


def _deinterleave_gate_up(w_gu, tn):
    e, d, f2 = w_gu.shape
    assert f2 % tn == 0
    half = tn // 2
    col = jnp.arange(tn)
    perm = (jnp.where(col % 2 == 0, col // 2, half + col // 2)[:, None] == col[None, :]).astype(BF16)
    out = jax.ShapeDtypeStruct((e, d, f2 // 2), BF16)
    return pl.pallas_call(
        _deinterleave_kernel,
        grid=(e, f2 // tn),
        in_specs=[pl.BlockSpec((1, d, tn), lambda i, j: (i, 0, j)),
                  pl.BlockSpec((tn, tn), lambda i, j: (0, 0))],
        out_specs=(pl.BlockSpec((1, d, half), lambda i, j: (i, 0, j)),
                   pl.BlockSpec((1, d, half), lambda i, j: (i, 0, j))),
        out_shape=(out, out),
        compiler_params=_cparams(("arbitrary", "arbitrary")),
        name="deinterleave_gate_up",
    )(w_gu, perm)


def _route_kernel(lg_ref, tril_ref, idx_ref, gate_ref, rank_ref, cnt_ref, carry_sc):
    @pl.when(pl.program_id(0) == 0)
    def _():
        carry_sc[...] = jnp.zeros_like(carry_sc)

    x = lg_ref[...]
    lane = lax.broadcasted_iota(jnp.int32, x.shape, 1)
    lane_f = lane.astype(F32)
    x = jnp.where(lane < N_EXPERTS, x, -jnp.inf)
    carry = carry_sc[0:1, :]
    vals, idxs, ranks = [], [], []
    for _ in range(TOP_K):
        m = jnp.max(x, axis=1, keepdims=True)
        idx = jnp.min(jnp.where(x == m, lane_f, float(LANES)), axis=1, keepdims=True)
        onehot = lane_f == idx
        oh = jnp.where(onehot, 1.0, 0.0)
        before = jnp.dot(tril_ref[...], oh.astype(BF16), preferred_element_type=F32)
        ranks.append(jnp.sum(jnp.where(onehot, before + carry, 0.0), axis=1, keepdims=True))
        carry = carry + jnp.sum(oh, axis=0, keepdims=True)
        x = jnp.where(onehot, -jnp.inf, x)
        vals.append(m)
        idxs.append(idx)
    carry_sc[0:1, :] = carry
    cnt_ref[...] = jnp.broadcast_to(carry, cnt_ref.shape)
    e = [jnp.exp(v - vals[0]) for v in vals]
    tot = e[0]
    for ek in e[1:]:
        tot = tot + ek
    gates = jnp.zeros(x.shape, F32)
    idx_out = jnp.zeros(x.shape, F32)
    rank_out = jnp.zeros(x.shape, F32)
    for k in range(TOP_K):
        gates = jnp.where(lane == k, e[k] / tot, gates)
        idx_out = jnp.where(lane == k, idxs[k], idx_out)
        rank_out = jnp.where(lane == k, ranks[k], rank_out)
    gate_ref[...] = gates
    idx_ref[...] = idx_out.astype(jnp.int32)
    rank_ref[...] = rank_out.astype(jnp.int32)


def _route(logits, tm):
    n = logits.shape[0]
    tm = min(tm, n)
    assert n % tm == 0
    tril = (jnp.arange(tm)[None, :] < jnp.arange(tm)[:, None]).astype(BF16)
    row = lambda i: (i, 0)
    idx, gates, rank, cnt = pl.pallas_call(
        _route_kernel,
        grid=(n // tm,),
        in_specs=[pl.BlockSpec((tm, LANES), row), pl.BlockSpec((tm, tm), lambda i: (0, 0))],
        out_specs=(pl.BlockSpec((tm, LANES), row), pl.BlockSpec((tm, LANES), row),
                   pl.BlockSpec((tm, LANES), row), pl.BlockSpec((8, LANES), lambda i: (0, 0))),
        out_shape=(jax.ShapeDtypeStruct((n, LANES), jnp.int32), jax.ShapeDtypeStruct((n, LANES), F32),
                   jax.ShapeDtypeStruct((n, LANES), jnp.int32), jax.ShapeDtypeStruct((8, LANES), F32)),
        scratch_shapes=[pltpu.VMEM((8, LANES), F32)],
        compiler_params=_cparams(("arbitrary",)),
        name="moe_route",
    )(logits, tril)
    return (idx[:, :TOP_K], gates[:, :TOP_K], rank[:, :TOP_K],
            cnt[0, :N_EXPERTS].astype(jnp.int32))


def _dispatch_kernel(slot_ref, pad_ref, h_ref, xs_out, sem):
    rows = h_ref.shape[0]
    n_pad = pad_ref.shape[0]

    @pl.when(pl.program_id(0) == 0)
    def _():
        def pad_body(r, carry):
            pltpu.make_async_copy(h_ref.at[pl.ds(0, 1)], xs_out.at[pl.ds(pad_ref[r], 1)], sem).start()
            return carry
        lax.fori_loop(0, n_pad, pad_body, 0, unroll=8)
        for _ in range(n_pad // rows):
            pltpu.make_async_copy(h_ref, xs_out.at[pl.ds(0, rows)], sem).wait()

    def body(r, carry):
        for k in range(TOP_K):
            s = slot_ref[0, 0, r * TOP_K + k]
            pltpu.make_async_copy(h_ref.at[pl.ds(r, 1)], xs_out.at[pl.ds(s, 1)], sem).start()
        return carry

    lax.fori_loop(0, rows, body, 0, unroll=2)
    for _ in range(TOP_K):
        pltpu.make_async_copy(h_ref, xs_out.at[pl.ds(0, rows)], sem).wait()


def _dispatch(h2, slot, pad_slots, n_slots, tm):
    n, d = h2.shape
    tm = min(tm, n)
    assert n % tm == 0 and pad_slots.shape[0] % tm == 0
    slot3 = slot.reshape(n // tm, 1, tm * TOP_K)
    return pl.pallas_call(
        _dispatch_kernel,
        grid=(n // tm,),
        in_specs=[pl.BlockSpec((1, 1, tm * TOP_K), lambda i: (i, 0, 0), memory_space=pltpu.SMEM),
                  pl.BlockSpec(memory_space=pltpu.SMEM),
                  pl.BlockSpec((tm, d), lambda i: (i, 0))],
        out_specs=pl.BlockSpec(memory_space=pl.ANY),
        out_shape=jax.ShapeDtypeStruct((n_slots, d), F32),
        scratch_shapes=[pltpu.SemaphoreType.DMA(())],
        compiler_params=_cparams(("arbitrary",)),
        name="moe_dispatch",
    )(slot3, pad_slots, h2)


def _row_gather_wait(src_hbm, n_rows, dst, sem):
    pltpu.make_async_copy(src_hbm.at[pl.ds(0, n_rows)], dst, sem).wait()


def _moe_kernel(be_ref, nused_ref, x_ref, wg_ref, wu_ref, bg_ref, bu_ref, wd_ref, bd_ref, y_ref):
    i = pl.program_id(0)
    n_used = nused_ref[0]

    @pl.when(i < n_used)
    def _():
        x = x_ref[...].astype(BF16)
        gate = jnp.dot(x, wg_ref[0], preferred_element_type=F32) + bg_ref[0]
        up = jnp.dot(x, wu_ref[0], preferred_element_type=F32) + bu_ref[0]
        gate = jnp.minimum(gate, SWIGLU_LIMIT)
        up = jnp.clip(up, -SWIGLU_LIMIT, SWIGLU_LIMIT)
        glu = gate * jax.nn.sigmoid(SWIGLU_ALPHA * gate)
        act = ((up + 1.0) * glu).astype(BF16)
        y_ref[...] = jnp.dot(act, wd_ref[0], preferred_element_type=F32) + bd_ref[0]

    @pl.when(i >= n_used)
    def _():
        y_ref[...] = jnp.zeros_like(y_ref)


def _expert_ffn(xs, block_expert, n_used, w_g, w_u, b_g, b_u, w_d, b_d, bm):
    n_slots, d = xs.shape
    n_blocks = block_expert.shape[0]
    dff = w_g.shape[2]
    wmap = lambda i, be, nu: (be[i], 0, 0)
    grid_spec = pltpu.PrefetchScalarGridSpec(
        num_scalar_prefetch=2,
        grid=(n_blocks,),
        in_specs=[
            pl.BlockSpec((bm, d), lambda i, be, nu: (i, 0)),
            pl.BlockSpec((1, d, dff), wmap), pl.BlockSpec((1, d, dff), wmap),
            pl.BlockSpec((1, 1, dff), wmap), pl.BlockSpec((1, 1, dff), wmap),
            pl.BlockSpec((1, dff, d), wmap), pl.BlockSpec((1, 1, d), wmap),
        ],
        out_specs=pl.BlockSpec((bm, d), lambda i, be, nu: (i, 0)),
    )
    return pl.pallas_call(
        _moe_kernel,
        grid_spec=grid_spec,
        out_shape=jax.ShapeDtypeStruct((n_slots, d), F32),
        compiler_params=_cparams(("arbitrary",)),
        name="expert_ffn",
    )(block_expert, n_used, xs, w_g, w_u, b_g, b_u, w_d, b_d)


def _combine_kernel(idx_ref, idx_next_ref, yb_hbm, resid_ref, gates_ref, o_ref, buf, sem):
    i = pl.program_id(0)
    n = pl.num_programs(0)
    tc = resid_ref.shape[0]
    slot = lax.rem(i, 2)

    def start(idx, par):
        def body(r, carry):
            pltpu.make_async_copy(yb_hbm.at[pl.ds(idx[0, 0, r], 1)], buf.at[par, pl.ds(r, 1)],
                                  sem.at[par]).start()
            return carry
        lax.fori_loop(0, TOP_K * tc, body, 0, unroll=8)

    @pl.when(i == 0)
    def _():
        start(idx_ref, 0)

    @pl.when(i + 1 < n)
    def _():
        start(idx_next_ref, 1 - slot)

    _row_gather_wait(yb_hbm, TOP_K * tc, buf.at[slot], sem.at[slot])
    acc = jnp.zeros(o_ref.shape, F32)
    for k in range(TOP_K):
        acc = acc + buf[slot, k * tc:(k + 1) * tc, :] * gates_ref[:, k:k + 1]
    o_ref[...] = resid_ref[...] + acc


def _combine(yb, slot_tm, gates, resid, tc):
    n, d = resid.shape
    n_tiles = n // tc
    rows = TOP_K * tc
    return pl.pallas_call(
        _combine_kernel,
        grid=(n_tiles,),
        in_specs=[
            pl.BlockSpec((1, 1, rows), lambda i: (i, 0, 0), memory_space=pltpu.SMEM),
            pl.BlockSpec((1, 1, rows), lambda i: (jnp.minimum(i + 1, n_tiles - 1), 0, 0),
                         memory_space=pltpu.SMEM),
            pl.BlockSpec(memory_space=pl.ANY),
            pl.BlockSpec((tc, d), lambda i: (i, 0)),
            pl.BlockSpec((tc, TOP_K), lambda i: (i, 0)),
        ],
        out_specs=pl.BlockSpec((tc, d), lambda i: (i, 0)),
        out_shape=jax.ShapeDtypeStruct((n, d), F32),
        scratch_shapes=[pltpu.VMEM((2, rows, d), F32), pltpu.SemaphoreType.DMA((2,))],
        compiler_params=_cparams(("arbitrary",)),
        name="moe_combine",
    )(slot_tm, slot_tm, yb, resid, gates)


def _moe_layer(y1, h2, logits, moe_w, bm, tc):
    n, d = y1.shape
    w_g, w_u, b_g, b_u, w_d, b_d = moe_w
    bm = min(bm, n)
    tc = min(tc, n)
    idx, gates, rank, counts = _route(logits, ROUTE_ROWS)
    padded = (counts + bm - 1) // bm * bm
    pend = jnp.cumsum(padded)
    pstart = pend - padded
    experts = jnp.arange(N_EXPERTS, dtype=jnp.int32)
    slot = rank + jnp.sum(jnp.where(idx[..., None] == experts, pstart, 0), axis=-1).astype(jnp.int32)
    n_blocks = -(-(n * TOP_K) // bm) + N_EXPERTS
    block_start = jnp.arange(n_blocks, dtype=jnp.int32) * bm
    block_expert = jnp.minimum(jnp.sum(pend[None, :] <= block_start[:, None], axis=1),
                               N_EXPERTS - 1).astype(jnp.int32)
    n_used = (pend[-1] // bm).astype(jnp.int32).reshape(1)
    assert (n * TOP_K) % bm == 0
    r = jnp.arange(bm, dtype=jnp.int32)
    is_pad = r[None, :] < (padded - counts)[:, None]
    tail_rank = (jnp.cumsum(jnp.logical_not(is_pad).reshape(-1)) - 1).reshape(N_EXPERTS, bm)
    pad_slots = jnp.where(is_pad, (pstart + counts)[:, None] + r[None, :], pend[-1] + tail_rank)
    xs = _dispatch(h2, slot, pad_slots.reshape(-1).astype(jnp.int32), n_blocks * bm, DISPATCH_ROWS)
    yb = _expert_ffn(xs, block_expert, n_used, w_g, w_u, b_g, b_u, w_d, b_d, bm)
    slot_km = slot.reshape(n // tc, tc, TOP_K).transpose(0, 2, 1).reshape(n // tc, 1, TOP_K * tc)
    return _combine(yb, slot_km, gates, y1, tc)


def _sample_attn_kernel(pt_ref, lam_ref, qsf_ref, qsd_ref, cncol_ref, cnrow_ref, snew_ref, blast_ref,
                        knf_ref, vnf_ref, knd_ref, vnd_ref, tri_ref, go_ref, *rest, pps, out_scale):
    page_refs = rest[:5 * pps]
    of_ref, od_ref, m_sc, l_sc, accf_sc, accd_sc, carry_sc = rest[5 * pps:]
    kf_refs = page_refs[0 * pps:1 * pps]
    vf_refs = page_refs[1 * pps:2 * pps]
    kd_refs = page_refs[2 * pps:3 * pps]
    vd_refs = page_refs[3 * pps:4 * pps]
    lft_refs = page_refs[4 * pps:5 * pps]
    g = pl.program_id(1)
    n_g = pl.num_programs(1)
    nh = N_FOX_HEADS
    nd_heads = N_DIFF_HEADS
    half_rows = cncol_ref.shape[1] // 2
    tok = half_rows // nh

    @pl.when(g == 0)
    def _():
        m_sc[...] = jnp.full_like(m_sc, -jnp.inf)
        l_sc[...] = jnp.zeros_like(l_sc)
        accf_sc[...] = jnp.zeros_like(accf_sc)
        accd_sc[...] = jnp.zeros_like(accd_sc)
        carry_sc[...] = jnp.zeros_like(carry_sc)

    _NN = (((1,), (0,)), ((), ()))

    def scores(kf_heads, kd_heads, fox_dims):
        rows = [lax.dot_general(qsf_ref[0, h], kf_heads[h], fox_dims, preferred_element_type=F32)
                for h in range(nh)]
        rows += [lax.dot_general(qsd_ref[0, h], kd_heads[h], _NT, preferred_element_type=F32)
                 for h in range(nd_heads)]
        return jnp.concatenate(rows, axis=0)

    def update(s, vf_heads, vd_heads, fox_dims):
        p, alpha = _online_softmax(s, m_sc, l_sc, slice(None))
        pv_f = jnp.concatenate(
            [lax.dot_general(p[h * tok:(h + 1) * tok].astype(BF16), vf_heads[h], fox_dims,
                             preferred_element_type=F32) for h in range(nh)], axis=0)
        pv_d = jnp.concatenate(
            [jnp.dot(p[half_rows + 2 * h * tok:half_rows + 2 * (h + 1) * tok].astype(BF16), vd_heads[h],
                     preferred_element_type=F32) for h in range(nd_heads)], axis=0)
        accf_sc[...] = alpha[:half_rows, :HEAD_DIM] * accf_sc[...] + pv_f
        accd_sc[...] = alpha[half_rows:] * accd_sc[...] + pv_d

    def fox_page_heads(refs):
        return [jnp.concatenate([ref[0, 0, h] for ref in refs], axis=1).astype(BF16) for h in range(nh)]

    def diff_page_heads(refs):
        page = refs[0].shape[2] // nd_heads
        return [jnp.concatenate([ref[0, 0, pl.ds(h, page, stride=nd_heads), :] for ref in refs],
                                axis=0).astype(BF16) for h in range(nd_heads)]

    def past_chunk(with_bias):
        carry = carry_sc[...]
        suf = [None] * pps
        for p in reversed(range(pps)):
            x = lft_refs[p][0]
            x3 = jnp.concatenate(_split3(x), axis=0)
            y3 = jnp.dot(x3, tri_ref[...], preferred_element_type=F32)
            within = y3[0:nh] + y3[nh:2 * nh] + y3[2 * nh:3 * nh]
            suf[p] = within + carry
            carry = carry + (within[:, 0:1] + x[:, 0:1])
        carry_sc[...] = carry
        sufc = jnp.concatenate(suf, axis=1) * LOG2E
        r = sufc.shape[1]
        suf_rows = jnp.concatenate(
            [jnp.broadcast_to(sufc[h:h + 1, :], (tok, r)) for h in range(nh)]
            + [jnp.zeros((half_rows, r), F32)], axis=0)
        s = scores(fox_page_heads(kf_refs), diff_page_heads(kd_refs), _NN)
        s = s + _lane_tile(cncol_ref[0], r // LANES) + suf_rows
        if with_bias:
            s = s + blast_ref[...]
        update(s, fox_page_heads(vf_refs), diff_page_heads(vd_refs), _NT)

    @pl.when(g == 0)
    def _():
        past_chunk(True)

    @pl.when(g > 0)
    def _():
        past_chunk(False)

    @pl.when(g == n_g - 1)
    def _():
        def new_heads(ref, n_heads):
            width = ref.shape[2] // n_heads
            return [ref[0, :, h * width:(h + 1) * width].astype(BF16) for h in range(n_heads)]

        s = scores(new_heads(knf_ref, nh), new_heads(knd_ref, nd_heads), _NT)
        s = s + (cncol_ref[0] - cnrow_ref[0]) + snew_ref[...]
        update(s, new_heads(vnf_ref, nh), new_heads(vnd_ref, nd_heads), _NN)

        l = l_sc[...]
        nf = accf_sc[...] / l[:half_rows, :HEAD_DIM]
        for h in range(nh):
            of_ref[0, :, h * HEAD_DIM:(h + 1) * HEAD_DIM] = nf[h * tok:(h + 1) * tok].astype(of_ref.dtype)

        nd = accd_sc[...] / l[half_rows:]
        lam = lam_ref[0]
        for h in range(N_DIFF_HEADS):
            cols = slice(h * LANES, (h + 1) * LANES)
            a1 = nd[(2 * h) * tok:(2 * h + 1) * tok]
            a2 = nd[(2 * h + 1) * tok:(2 * h + 2) * tok]
            o = a1 - lam * a2
            ms = jnp.mean(o * o, axis=-1, keepdims=True)
            od_ref[0, :, cols] = (o * lax.rsqrt(ms + RMS_EPS) * go_ref[...] * out_scale).astype(od_ref.dtype)


def _sample_attention(qf, qd, kf_new, vf_new, kd_new, vd_new, c_new, caches, page_table, rel_table, lam,
                      g_out, out_scale, pps, layer):
    db, s_new, w = qf.shape
    ck_f, cv_f, ck_d, cv_d, clf_t = caches
    page = ck_f.shape[2]
    ck_f, cv_f = (jnp.transpose(a, (0, 1, 3, 4, 2)) for a in (ck_f, cv_f))
    ck_d, cv_d = (a.reshape(a.shape[0], a.shape[1], page * N_DIFF_HEADS, 2 * HEAD_DIM) for a in (ck_d, cv_d))
    n_pages = page_table.shape[1]
    pps = min(pps, n_pages)
    assert n_pages % pps == 0 and page == LANES and s_new * N_FOX_HEADS * 2 == LANES
    n_g = n_pages // pps
    r = pps * page
    nh = N_FOX_HEADS
    half_rows = nh * s_new

    qsf = jnp.transpose(qf.reshape(db, s_new, nh, HEAD_DIM), (0, 2, 1, 3))
    qd_h = jnp.transpose(qd.reshape(db, s_new, N_DIFF_HEADS, 2 * HEAD_DIM), (0, 2, 1, 3))
    first = (jnp.arange(2 * HEAD_DIM) < HEAD_DIM)
    qsd = jnp.concatenate([jnp.where(first, qd_h, jnp.zeros((), qd.dtype)),
                           jnp.where(first, jnp.zeros((), qd.dtype), qd_h)], axis=2)

    c_new = c_new * LOG2E
    c_rows =jnp.transpose(c_new, (0, 2, 1)).reshape(db, half_rows)
    cn_col = jnp.concatenate([c_rows, jnp.zeros_like(c_rows)], axis=1)[:, :, None]
    cn_col = jnp.broadcast_to(cn_col, (db, LANES, LANES))
    c_keys = jnp.repeat(jnp.transpose(c_new, (0, 2, 1)), s_new, axis=1)
    cn_row = jnp.zeros((db, LANES, LANES), F32).at[:, :half_rows, :s_new].set(c_keys)

    t_of_row = jnp.arange(LANES, dtype=jnp.int32) % s_new
    key = jnp.arange(LANES, dtype=jnp.int32)
    valid_new = (key[None, :] <= t_of_row[:, None]) & (key[None, :] < s_new)
    is_diff = (jnp.arange(LANES) >= half_rows)
    head_of_row = jnp.clip((jnp.arange(LANES) - half_rows) // (2 * s_new), 0, N_DIFF_HEADS - 1)

    def rel_rows(dist):
        per_head = _rel_bias_lookup(rel_table, dist)
        vals = jnp.zeros(dist.shape, F32)
        for h in range(N_DIFF_HEADS):
            vals = jnp.where(head_of_row[:, None] == h, per_head[h], vals)
        return jnp.where(is_diff[:, None], vals, 0.0)

    s_new_tile = jnp.where(valid_new, rel_rows(t_of_row[:, None] - key[None, :]), -jnp.inf)
    key_r = jnp.arange(r, dtype=jnp.int32)
    b_last = rel_rows(r - key_r[None, :] + t_of_row[:, None])

    def pad_rows(x):
        return jnp.pad(x, ((0, 0), (0, LANES - s_new), (0, 0)))

    tri = (jnp.arange(page)[:, None] > jnp.arange(page)[None, :]).astype(BF16)

    bmap = lambda b, g, pt: (b, 0, 0)
    bmap4 = lambda b, g, pt: (b, 0, 0, 0)
    const2 = lambda b, g, pt: (0, 0)

    def page_map(p):
        return lambda b, g, pt: (pt[b, (n_g - 1 - g) * pps + p], 0, 0)

    def cache_map(p, ndim):
        return lambda b, g, pt: (layer, pt[b, (n_g - 1 - g) * pps + p]) + (0,) * (ndim - 2)

    in_specs = [
        pl.BlockSpec(memory_space=pltpu.SMEM),
        pl.BlockSpec((1,) + qsf.shape[1:], bmap4), pl.BlockSpec((1,) + qsd.shape[1:], bmap4),
        pl.BlockSpec((1, LANES, LANES), bmap), pl.BlockSpec((1, LANES, LANES), bmap),
        pl.BlockSpec((LANES, LANES), const2), pl.BlockSpec((LANES, r), const2),
        pl.BlockSpec((1, LANES, w), bmap), pl.BlockSpec((1, LANES, w), bmap),
        pl.BlockSpec((1, LANES, w), bmap), pl.BlockSpec((1, LANES, w), bmap),
        pl.BlockSpec((page, page), const2), pl.BlockSpec((1, LANES), const2),
    ]
    operands = [lam, qsf, qsd, cn_col, cn_row, s_new_tile, b_last,
                pad_rows(kf_new), pad_rows(vf_new), pad_rows(kd_new), pad_rows(vd_new), tri, g_out]
    for arr in (ck_f, cv_f, ck_d, cv_d):
        for p in range(pps):
            in_specs.append(pl.BlockSpec((1, 1) + arr.shape[2:], cache_map(p, arr.ndim)))
            operands.append(arr)
    for p in range(pps):
        in_specs.append(pl.BlockSpec((1, nh, page), page_map(p)))
        operands.append(clf_t)

    grid_spec = pltpu.PrefetchScalarGridSpec(
        num_scalar_prefetch=1,
        grid=(db, n_g),
        in_specs=in_specs,
        out_specs=(pl.BlockSpec((1, s_new, w), bmap), pl.BlockSpec((1, s_new, w), bmap)),
        scratch_shapes=[pltpu.VMEM((LANES, LANES), F32), pltpu.VMEM((LANES, LANES), F32),
                        pltpu.VMEM((half_rows, HEAD_DIM), F32), pltpu.VMEM((half_rows, 2 * HEAD_DIM), F32),
                        pltpu.VMEM((nh, LANES), F32)],
    )
    return pl.pallas_call(
        functools.partial(_sample_attn_kernel, pps=pps, out_scale=out_scale),
        grid_spec=grid_spec,
        out_shape=(jax.ShapeDtypeStruct((db, s_new, w), BF16), jax.ShapeDtypeStruct((db, s_new, w), BF16)),
        compiler_params=_cparams(("arbitrary", "arbitrary")),
        name="sample_attention",
    )(page_table, *operands)


PROJ_ROWS = 256
ATTN_BLOCK = 512
SCAN_BLOCK = 512
MOE_ROWS = 256
COMBINE_ROWS = 128
ROUTE_ROWS = 512
DISPATCH_ROWS = 512
DEINTERLEAVE_COLS = 512
PAGES_PER_STEP = 8


def _split_points(d_model):
    sizes = (FOX_WIDTH, FOX_WIDTH, FOX_WIDTH, N_FOX_HEADS, DIFF_WIDTH, DIFF_WIDTH, DIFF_WIDTH, d_model)
    pts, acc = [], 0
    for s in sizes:
        acc += s
        pts.append(acc)
    return pts


def _layer_weights(l, d_model, attn_norm_g, w_in, b_forget, fox_q_norm_g, fox_k_norm_g, diff_q_norm_g,
                   diff_k_norm_g, lambda_q1, lambda_k1, lambda_q2, lambda_k2, diff_out_norm_g, w_fox_up,
                   w_diff_up, w_out, ffn_norm_g, w_router, b_router, w_gate_up, b_gate_up, w_down, b_down):
    lambda_init = 0.8 - 0.6 * math.exp(-0.3 * l)
    wq_f, wk_f, wv_f, w_fl, wq_d, wk_d, wv_d, wg_f, wg_d = jnp.split(w_in[l], _split_points(d_model), axis=1)
    w_main = jnp.concatenate([wq_f, wk_f, wv_f, wq_d, wk_d, wv_d, wg_f, wg_d], axis=1).astype(BF16)
    w_fl = jnp.pad(w_fl, ((0, 0), (0, LANES - N_FOX_HEADS))).astype(BF16)
    b_fl = jnp.pad(b_forget[l].astype(F32), (0, LANES - N_FOX_HEADS)).reshape(1, LANES)
    grp = jnp.arange(FOX_WIDTH) // HEAD_DIM
    ones_bd = (grp[:, None] == grp[None, :]).astype(BF16)
    reps = FOX_WIDTH // HEAD_DIM
    gains = jnp.stack([jnp.tile(g[l].astype(F32), reps)
                       for g in (fox_q_norm_g, fox_k_norm_g, diff_q_norm_g, diff_k_norm_g)])
    lam = (jnp.exp(jnp.sum(lambda_q1[l].astype(F32) * lambda_k1[l].astype(F32)))
           - jnp.exp(jnp.sum(lambda_q2[l].astype(F32) * lambda_k2[l].astype(F32))) + lambda_init)
    w_r = jnp.pad(w_router[l], ((0, 0), (0, LANES - N_EXPERTS))).astype(BF16)
    b_r = jnp.pad(b_router[l].astype(F32), (0, LANES - N_EXPERTS)).reshape(1, LANES)
    w_g, w_u = _deinterleave_gate_up(w_gate_up[l], DEINTERLEAVE_COLS)
    bgu = b_gate_up[l].astype(F32)
    moe_w = (w_g, w_u, bgu[:, None, 0::2], bgu[:, None, 1::2],
             w_down[l].astype(BF16), b_down[l].astype(F32)[:, None, :])
    return dict(
        lambda_init=lambda_init, g1=attn_norm_g[l].astype(F32).reshape(1, d_model), w_main=w_main, w_fl=w_fl,
        b_fl=b_fl, ones_bd=ones_bd, gains=gains, lam=lam.reshape(1).astype(F32),
        g_out=diff_out_norm_g[l].astype(F32).reshape(1, LANES),
        w_fu=w_fox_up[l].astype(BF16), w_du=w_diff_up[l].astype(BF16), w_o=w_out[l].astype(BF16),
        g2=ffn_norm_g[l].astype(F32).reshape(1, d_model), w_r=w_r, b_r=b_r, moe_w=moe_w)


def _project(x, lw):
    b, t, d = x.shape
    outs = _in_projection(x.reshape(b * t, d), lw['g1'], lw['w_main'], lw['w_fl'], lw['b_fl'],
                          lw['ones_bd'], lw['gains'], PROJ_ROWS, t)
    qf, kf, vf, lf, qd, kd, vd, gf, gd = outs[:9]
    logf = lf[:, :N_FOX_HEADS].reshape(b, t, N_FOX_HEADS)
    r3 = lambda a: a.reshape(b, t, a.shape[-1])
    if len(outs) > 9:
        rows4 = lambda a: jnp.transpose(a.reshape(b, N_FOX_HEADS, HEAD_DIM, t), (0, 3, 1, 2))
        fox_rows = (rows4(outs[9]), rows4(outs[10]))
    else:
        fox_rows = (kf.reshape(b, t, N_FOX_HEADS, HEAD_DIM), vf.reshape(b, t, N_FOX_HEADS, HEAD_DIM))
    return r3(qf), r3(kf), r3(vf), logf, r3(qd), r3(kd), r3(vd), gf, gd, fox_rows


def _finish_layer(x, o_f, o_d, gf, gd, lw, moe_rows):
    b, t, d = x.shape
    n = b * t
    y1, h2, logits = _out_projection(x.reshape(n, d), o_f.reshape(n, -1), o_d.reshape(n, -1), gf, gd,
                                     lw['w_fu'], lw['w_du'], lw['w_o'], lw['g2'], lw['w_r'], lw['b_r'],
                                     PROJ_ROWS)
    return _moe_layer(y1, h2, logits, lw['moe_w'], moe_rows, COMBINE_ROWS).reshape(b, t, d)


def _new_rows(fox_rows, logf, kd, vd):
    b, t, _ = kd.shape
    return (fox_rows[0], fox_rows[1], logf,
            kd.reshape(b, t, N_DIFF_HEADS, 2 * HEAD_DIM), vd.reshape(b, t, N_DIFF_HEADS, 2 * HEAD_DIM))


def _prompt_layer(x, lw, rel_table):
    qf, kf, vf, logf, qd, kd, vd, gf, gd, fox_rows = _project(x, lw)
    ct = _prefix_sum_time(jnp.transpose(logf, (0, 2, 1)), SCAN_BLOCK)
    c = jnp.transpose(ct, (0, 2, 1))
    o_f = _fox_attention(qf, kf, vf, c, ATTN_BLOCK)
    o_d = _diff_attention(qd, kd, vd, rel_table, lw['lam'], lw['g_out'], 1.0 - lw['lambda_init'], ATTN_BLOCK)
    return _finish_layer(x, o_f, o_d, gf, gd, lw, MOE_ROWS), _new_rows(fox_rows, logf, kd, vd)


def _sample_layer(x, lw, rel_table, caches, page_table, layer):
    qf, kf, vf, logf, qd, kd, vd, gf, gd, fox_rows = _project(x, lw)
    b, t, _ = logf.shape
    logf_t = jnp.pad(jnp.transpose(logf, (0, 2, 1)), ((0, 0), (0, 0), (0, LANES - t)))
    c = jnp.transpose(_prefix_sum_time(logf_t, LANES)[:, :, :t], (0, 2, 1))
    o_f, o_d = _sample_attention(qf, qd, kf, vf, kd, vd, c, caches, page_table, rel_table, lw['lam'],
                                 lw['g_out'], 1.0 - lw['lambda_init'], PAGES_PER_STEP, layer)
    return _finish_layer(x, o_f, o_d, gf, gd, lw, LANES), _new_rows(fox_rows, logf, kd, vd)


def kernel(x_prompt, x_sample, cache_fox_k, cache_fox_v, cache_fox_logf, cache_diff_k, cache_diff_v, page_table, rel_bias_table, attn_norm_g, w_in, b_forget, fox_q_norm_g, fox_k_norm_g, diff_q_norm_g, diff_k_norm_g, lambda_q1, lambda_k1, lambda_q2, lambda_k2, diff_out_norm_g, w_fox_up, w_diff_up, w_out, ffn_norm_g, w_router, b_router, w_gate_up, b_gate_up, w_down, b_down):
    depth = w_in.shape[0]
    d_model = x_prompt.shape[-1]
    yp, ys = x_prompt, x_sample
    rows_p, rows_s = [], []
    for l in range(depth):
        lw = _layer_weights(l, d_model, attn_norm_g, w_in, b_forget, fox_q_norm_g, fox_k_norm_g,
                            diff_q_norm_g, diff_k_norm_g, lambda_q1, lambda_k1, lambda_q2, lambda_k2,
                            diff_out_norm_g, w_fox_up, w_diff_up, w_out, ffn_norm_g, w_router, b_router,
                            w_gate_up, b_gate_up, w_down, b_down)
        yp, rp = _prompt_layer(yp, lw, rel_bias_table)
        caches = (cache_fox_k, cache_fox_v, cache_diff_k, cache_diff_v,
                  jnp.transpose(cache_fox_logf[l].astype(F32), (0, 2, 1)))
        ys, rs = _sample_layer(ys, lw, rel_bias_table, caches, page_table, l)
        rows_p.append(rp)
        rows_s.append(rs)
    stack = lambda rows, i: jnp.stack([r[i] for r in rows])
    return (yp, ys) + tuple(stack(rows_p, i) for i in range(5)) + tuple(stack(rows_s, i) for i in range(5))
```

```python
import functools
import math

import jax
import jax.numpy as jnp
from jax import lax
from jax.experimental import pallas as pl
from jax.experimental.pallas import tpu as pltpu

F32 = jnp.float32
BF16 = jnp.bfloat16

HEAD_DIM = 64
N_FOX_HEADS = 8
N_DIFF_HEADS = 4
FOX_WIDTH = N_FOX_HEADS * HEAD_DIM
DIFF_WIDTH = N_DIFF_HEADS * 2 * HEAD_DIM
N_BUCKETS = 32
MAX_DISTANCE = 128
N_EXPERTS = 32
TOP_K = 4
SWIGLU_LIMIT = 7.0
SWIGLU_ALPHA = 1.702
RMS_EPS = 1e-6
ATTN_SCALE = HEAD_DIM ** -0.5
LOG2E = math.log2(math.e)
Q_SCALE = ATTN_SCALE * LOG2E
LANES = 128
VMEM_LIMIT = 56 * 1024 * 1024


def _cparams(sem):
    return pltpu.CompilerParams(dimension_semantics=sem, vmem_limit_bytes=VMEM_LIMIT)


def _split3(x):
    x1 = x.astype(BF16)
    r1 = x - x1.astype(F32)
    x2 = r1.astype(BF16)
    x3 = (r1 - x2.astype(F32)).astype(BF16)
    return x1, x2, x3


def _log_sigmoid(x):
    return -(jnp.maximum(-x, 0.0) + jnp.log1p(jnp.exp(-jnp.abs(x))))


def _proj_kernel(x_ref, g_ref, w_ref, wfl_ref, bfl_ref, ones_ref, gains_ref,
                 qf_ref, kf_ref, vf_ref, lf_ref, qd_ref, kd_ref, vd_ref, gf_ref, gd_ref, *t_refs):
    x = x_ref[...]
    ms = jnp.mean(x * x, axis=-1, keepdims=True)
    xn = (x * lax.rsqrt(ms + RMS_EPS) * g_ref[...]).astype(BF16)

    def seg(lo, width):
        return jnp.dot(xn, w_ref[:, lo:lo + width], preferred_element_type=F32)

    ones_bd = ones_ref[...]

    def head_norm(p, gain):
        sq = p * p
        hi = sq.astype(BF16)
        lo = (sq - hi.astype(F32)).astype(BF16)
        ss = (jnp.dot(hi, ones_bd, preferred_element_type=F32)
              + jnp.dot(lo, ones_bd, preferred_element_type=F32))
        return p * lax.rsqrt(ss * (1.0 / HEAD_DIM) + RMS_EPS) * gain

    w = FOX_WIDTH
    qf_ref[...] = (head_norm(seg(0, w), gains_ref[0:1, :]) * Q_SCALE).astype(BF16)
    kf = head_norm(seg(w, w), gains_ref[1:2, :])
    vf = seg(2 * w, w)
    kf_ref[...] = kf
    vf_ref[...] = vf
    if t_refs:
        t_refs[0][0] = kf.T
        t_refs[1][0] = vf.T
    qd_ref[...] = (head_norm(seg(3 * w, w), gains_ref[2:3, :]) * Q_SCALE).astype(BF16)
    kd_ref[...] = head_norm(seg(4 * w, w), gains_ref[3:4, :])
    vd_ref[...] = seg(5 * w, w)
    d = gf_ref.shape[1]
    gf_ref[...] = jax.nn.sigmoid(seg(6 * w, d))
    gd_ref[...] = jax.nn.sigmoid(seg(6 * w + d, d))
    fl = jnp.dot(xn, wfl_ref[...], preferred_element_type=F32) + bfl_ref[...]
    lf_ref[...] = _log_sigmoid(fl)


def _in_projection(x2d, g, w_main, w_fl, b_fl, ones_bd, gains, tm, seq):
    n, d = x2d.shape
    w = FOX_WIDTH
    tm = min(tm, n)
    assert n % tm == 0
    row = lambda i: (i, 0)
    const = lambda i: (0, 0)
    out_shape = (
        jax.ShapeDtypeStruct((n, w), BF16), jax.ShapeDtypeStruct((n, w), F32),
        jax.ShapeDtypeStruct((n, w), F32), jax.ShapeDtypeStruct((n, LANES), F32),
        jax.ShapeDtypeStruct((n, w), BF16), jax.ShapeDtypeStruct((n, w), F32),
        jax.ShapeDtypeStruct((n, w), F32), jax.ShapeDtypeStruct((n, d), F32),
        jax.ShapeDtypeStruct((n, d), F32),
    )
    out_specs = tuple(pl.BlockSpec((tm, s.shape[1]), row) for s in out_shape)
    if seq % tm == 0:
        tiles = seq // tm
        t_shape = jax.ShapeDtypeStruct((n // seq, w, seq), F32)
        t_spec = pl.BlockSpec((1, w, tm), lambda i: (i // tiles, 0, i % tiles))
        out_shape += (t_shape, t_shape)
        out_specs += (t_spec, t_spec)
    return pl.pallas_call(
        _proj_kernel,
        grid=(n // tm,),
        in_specs=[
            pl.BlockSpec((tm, d), row),
            pl.BlockSpec((1, d), const),
            pl.BlockSpec(w_main.shape, const),
            pl.BlockSpec(w_fl.shape, const),
            pl.BlockSpec((1, LANES), const),
            pl.BlockSpec(ones_bd.shape, const),
            pl.BlockSpec(gains.shape, const),
        ],
        out_specs=out_specs,
        out_shape=out_shape,
        compiler_params=_cparams(("arbitrary",)),
        name="in_projection",
    )(x2d, g, w_main, w_fl, b_fl, ones_bd, gains)


def _scan_kernel(x_ref, tri_ref, o_ref, carry_sc):
    @pl.when(pl.program_id(1) == 0)
    def _():
        carry_sc[...] = jnp.zeros_like(carry_sc)

    x = x_ref[0]
    h = x.shape[0]
    x3 = jnp.concatenate(_split3(x), axis=0)
    y3 = jnp.dot(x3, tri_ref[...], preferred_element_type=F32)
    y = y3[0:h] + y3[h:2 * h] + y3[2 * h:3 * h] + carry_sc[:, 0:1]
    o_ref[0] = y
    carry_sc[...] = jnp.broadcast_to(y[:, -1:], carry_sc.shape)


def _prefix_sum_time(xt, tl):
    b, h, t = xt.shape
    tl = min(tl, t)
    assert t % tl == 0
    tri = (jnp.arange(tl)[:, None] <= jnp.arange(tl)[None, :]).astype(BF16)
    return pl.pallas_call(
        _scan_kernel,
        grid=(b, t // tl),
        in_specs=[pl.BlockSpec((1, h, tl), lambda i, j: (i, 0, j)),
                  pl.BlockSpec((tl, tl), lambda i, j: (0, 0))],
        out_specs=pl.BlockSpec((1, h, tl), lambda i, j: (i, 0, j)),
        out_shape=jax.ShapeDtypeStruct((b, h, t), F32),
        scratch_shapes=[pltpu.VMEM((h, LANES), F32)],
        compiler_params=_cparams(("arbitrary", "arbitrary")),
        name="logf_prefix_sum",
    )(xt, tri)


def _lane_tile(x, n):
    return x if n == 1 else jnp.concatenate([x] * n, axis=1)


def _online_softmax(s, m_ref, l_ref, idx):
    tk = s.shape[1]
    m_prev = m_ref[idx]
    m_next = jnp.maximum(m_prev, jnp.max(s, axis=1, keepdims=True))
    p = jnp.exp2(s - _lane_tile(m_next, tk // LANES))
    alpha = jnp.exp2(m_prev - m_next)
    l_ref[idx] = alpha * l_ref[idx] + jnp.sum(p, axis=1, keepdims=True)
    m_ref[idx] = m_next
    return p, alpha


def _half_mask(shape_lanes):
    lane = lax.broadcasted_iota(jnp.int32, (1, shape_lanes), 1)
    return lane < HEAD_DIM


_NT = (((1,), (1,)), ((), ()))


FOX_BIAS_LANES = 8


def _fox_bias_operands(c):
    b, t, h = c.shape
    to_bf16 = lambda x: lax.reduce_precision(x, exponent_bits=8, mantissa_bits=7)
    c1 = to_bf16(c)
    c2 = to_bf16(c - c1)
    c3 = to_bf16(c - c1 - c2)
    pieces = jnp.stack([c1, c2, c3], axis=-1).astype(BF16)
    ones = jnp.ones_like(pieces)
    pad = jnp.zeros(pieces.shape[:-1] + (FOX_BIAS_LANES - 6,), BF16)
    qx = jnp.concatenate([pieces, ones, pad], axis=-1).reshape(b, t, h * FOX_BIAS_LANES)
    kx = jnp.concatenate([ones, -pieces, pad], axis=-1).reshape(b, t, h * FOX_BIAS_LANES)
    fill = jnp.zeros((b, t, LANES - h * FOX_BIAS_LANES), BF16)
    return jnp.concatenate([qx, fill], axis=-1), jnp.concatenate([kx, fill], axis=-1)


def _causal_pairs(nq):
    pairs = [(i, j) for i in range(nq) for j in range(i + 1)]
    return (jnp.asarray([p[0] for p in pairs], jnp.int32), jnp.asarray([p[1] for p in pairs], jnp.int32))


def _fox_kernel(it_ref, jt_ref, q_ref, k_ref, v_ref, qx_ref, kx_ref, o_ref, qaug_sc, m_sc, l_sc, acc_sc):
    i = it_ref[pl.program_id(1)]
    j = jt_ref[pl.program_id(1)]
    tq = q_ref.shape[1]
    tk = k_ref.shape[1]
    first_half = _half_mask(LANES)

    @pl.when(j == 0)
    def _():
        m_sc[...] = jnp.full_like(m_sc, -jnp.inf)
        l_sc[...] = jnp.zeros_like(l_sc)
        acc_sc[...] = jnp.zeros_like(acc_sc)
        qx = qx_ref[0]
        lane_head = lax.broadcasted_iota(jnp.int32, (1, LANES), 1) // FOX_BIAS_LANES
        for h in range(N_FOX_HEADS):
            q2 = q_ref[0, :, (h // 2) * LANES:(h // 2 + 1) * LANES]
            half = first_half if h % 2 == 0 else jnp.logical_not(first_half)
            qaug_sc[h, :, :LANES] = jnp.where(half, q2, jnp.zeros_like(q2))
            qaug_sc[h, :, LANES:] = jnp.where(lane_head == h, qx, jnp.zeros_like(qx))

    def step(masked):
        if masked:
            row = lax.broadcasted_iota(jnp.int32, (tq, tk), 0)
            col = lax.broadcasted_iota(jnp.int32, (tq, tk), 1)
            keep = col <= row
        kx = kx_ref[0]
        for hp in range(N_FOX_HEADS // 2):
            cols = slice(hp * LANES, (hp + 1) * LANES)
            kaug = jnp.concatenate([k_ref[0, :, cols].astype(BF16), kx], axis=1)
            v2 = v_ref[0, :, cols].astype(BF16)
            pv = []
            alphas = []
            for sub in range(2):
                h = 2 * hp + sub
                s = lax.dot_general(qaug_sc[h], kaug, _NT, preferred_element_type=F32)
                if masked:
                    s = jnp.where(keep, s, -jnp.inf)
                p, alpha = _online_softmax(s, m_sc, l_sc, h)
                pv.append(jnp.dot(p.astype(BF16), v2, preferred_element_type=F32))
                alphas.append(alpha)
            acc_sc[hp] = (jnp.where(first_half, alphas[0], alphas[1]) * acc_sc[hp]
                          + jnp.where(first_half, pv[0], pv[1]))

    @pl.when(j < i)
    def _():
        step(False)

    @pl.when(j == i)
    def _():
        step(True)
        for hp in range(N_FOX_HEADS // 2):
            l2 = jnp.where(first_half, l_sc[2 * hp], l_sc[2 * hp + 1])
            o_ref[0, :, hp * LANES:(hp + 1) * LANES] = (acc_sc[hp] / l2).astype(o_ref.dtype)


def _fox_attention(q, k, v, c, tq):
    b, t, w = q.shape
    tq = min(tq, t)
    assert t % tq == 0 and tq % LANES == 0
    nq = t // tq
    qx, kx = _fox_bias_operands(c * LOG2E)
    itab, jtab = _causal_pairs(nq)
    qmap = lambda bi, p, it, jt: (bi, it[p], 0)
    kmap = lambda bi, p, it, jt: (bi, jt[p], 0)
    grid_spec = pltpu.PrefetchScalarGridSpec(
        num_scalar_prefetch=2,
        grid=(b, itab.shape[0]),
        in_specs=[
            pl.BlockSpec((1, tq, w), qmap),
            pl.BlockSpec((1, tq, w), kmap),
            pl.BlockSpec((1, tq, w), kmap),
            pl.BlockSpec((1, tq, LANES), qmap),
            pl.BlockSpec((1, tq, LANES), kmap),
        ],
        out_specs=pl.BlockSpec((1, tq, w), qmap),
        scratch_shapes=[pltpu.VMEM((N_FOX_HEADS, tq, 2 * LANES), BF16),
                        pltpu.VMEM((N_FOX_HEADS, tq, LANES), F32),
                        pltpu.VMEM((N_FOX_HEADS, tq, LANES), F32),
                        pltpu.VMEM((N_FOX_HEADS // 2, tq, LANES), F32)],
    )
    return pl.pallas_call(
        _fox_kernel,
        grid_spec=grid_spec,
        out_shape=jax.ShapeDtypeStruct((b, t, w), BF16),
        compiler_params=_cparams(("arbitrary", "arbitrary")),
        name="fox_attention",
    )(itab, jtab, q, k, v, qx, kx)


def _diff_finalize(acc1, l1, acc2, l2, lam, g_out, out_scale):
    o = acc1 / l1 - lam * (acc2 / l2)
    ms = jnp.mean(o * o, axis=-1, keepdims=True)
    return o * lax.rsqrt(ms + RMS_EPS) * g_out * out_scale


def _diff_kernel(it_ref, jt_ref, lam_ref, q_ref, k_ref, v_ref, bias_ref, go_ref, o_ref, qm_sc, m_sc, l_sc, acc_sc, *,
                 out_scale):
    i = it_ref[pl.program_id(1)]
    j = jt_ref[pl.program_id(1)]

    @pl.when(j == 0)
    def _():
        m_sc[...] = jnp.full_like(m_sc, -jnp.inf)
        l_sc[...] = jnp.zeros_like(l_sc)
        acc_sc[...] = jnp.zeros_like(acc_sc)
        first_half = _half_mask(LANES)
        for idx in range(2 * N_DIFF_HEADS):
            q2 = q_ref[0, :, (idx // 2) * LANES:(idx // 2 + 1) * LANES]
            half = first_half if idx % 2 == 0 else jnp.logical_not(first_half)
            qm_sc[idx] = jnp.where(half, q2, jnp.zeros_like(q2))

    def step(bias_idx):
        for h in range(N_DIFF_HEADS):
            cols = slice(h * LANES, (h + 1) * LANES)
            k2 = k_ref[0, :, cols].astype(BF16)
            v2 = v_ref[0, :, cols].astype(BF16)
            for c in range(2):
                idx = 2 * h + c
                s = lax.dot_general(qm_sc[idx], k2, _NT, preferred_element_type=F32)
                if bias_idx is not None:
                    s = s + bias_ref[bias_idx, h]
                p, alpha = _online_softmax(s, m_sc, l_sc, idx)
                acc_sc[idx] = alpha * acc_sc[idx] + jnp.dot(p.astype(BF16), v2, preferred_element_type=F32)

    @pl.when(j < i - 1)
    def _():
        step(None)

    @pl.when(j == i - 1)
    def _():
        step(1)

    @pl.when(j == i)
    def _():
        step(0)
        lam = lam_ref[0]
        for h in range(N_DIFF_HEADS):
            o = _diff_finalize(acc_sc[2 * h], l_sc[2 * h], acc_sc[2 * h + 1], l_sc[2 * h + 1],
                               lam, go_ref[...], out_scale)
            o_ref[0, :, h * LANES:(h + 1) * LANES] = o.astype(o_ref.dtype)


def _t5_bucket_1d(n):
    max_exact = N_BUCKETS // 2
    nf = jnp.maximum(n, 1).astype(F32)
    large = max_exact + (jnp.log(nf / max_exact) / math.log(MAX_DISTANCE / max_exact)
                         * (N_BUCKETS - max_exact)).astype(jnp.int32)
    large = jnp.minimum(large, N_BUCKETS - 1)
    return jnp.where(n < max_exact, n, large)


def _rel_bias_lookup(rel_table, dist):
    rel = (rel_table.astype(F32) - rel_table[N_BUCKETS - 1].astype(F32)[None, :]) * LOG2E
    bucket = _t5_bucket_1d(jnp.maximum(dist, 0))
    hit = bucket[None, ..., None] == jnp.arange(N_BUCKETS, dtype=jnp.int32)
    return jnp.sum(jnp.where(hit, rel.T.reshape((rel.shape[1],) + (1,) * dist.ndim + (N_BUCKETS,)), 0.0), axis=-1)


def _toeplitz(vals, rows, cols, offset):
    period = rows + cols
    first = jnp.flip(vals[..., offset - cols + 1:offset + 1], axis=-1)
    last = jnp.flip(vals[..., offset + 1:offset + rows], axis=-1)
    fill = jnp.zeros(vals.shape[:-1] + (period - cols - (rows - 1),), vals.dtype)
    w = jnp.concatenate([first, fill, last], axis=-1)
    flat = jnp.tile(w, (1,) * (vals.ndim - 1) + (rows,))[..., :rows * (period - 1)]
    return flat.reshape(vals.shape[:-1] + (rows, period - 1))[..., :cols]


def _diff_attention(q, k, v, rel_table, lam, g_out, out_scale, tq):
    b, t, w = q.shape
    tq = min(tq, t)
    assert t % tq == 0 and tq >= MAX_DISTANCE and tq % LANES == 0
    nq = t // tq
    val = _rel_bias_lookup(rel_table, jnp.arange(2 * tq, dtype=jnp.int32))
    masked = jnp.full((val.shape[0], tq - 1), -jnp.inf, F32)
    bias = jnp.stack([_toeplitz(jnp.concatenate([masked, val[:, :tq]], axis=1), tq, tq, tq - 1),
                      _toeplitz(val[:, 1:], tq, tq, tq - 1)])
    itab, jtab = _causal_pairs(nq)
    qmap = lambda bi, p, it, jt: (bi, it[p], 0)
    kmap = lambda bi, p, it, jt: (bi, jt[p], 0)
    grid_spec = pltpu.PrefetchScalarGridSpec(
        num_scalar_prefetch=2,
        grid=(b, itab.shape[0]),
        in_specs=[
            pl.BlockSpec(memory_space=pltpu.SMEM),
            pl.BlockSpec((1, tq, w), qmap),
            pl.BlockSpec((1, tq, w), kmap),
            pl.BlockSpec((1, tq, w), kmap),
            pl.BlockSpec(bias.shape, lambda bi, p, it, jt: (0, 0, 0, 0)),
            pl.BlockSpec((1, LANES), lambda bi, p, it, jt: (0, 0)),
        ],
        out_specs=pl.BlockSpec((1, tq, w), qmap),
        scratch_shapes=[pltpu.VMEM((2 * N_DIFF_HEADS, tq, LANES), BF16),
                        pltpu.VMEM((2 * N_DIFF_HEADS, tq, LANES), F32),
                        pltpu.VMEM((2 * N_DIFF_HEADS, tq, LANES), F32),
                        pltpu.VMEM((2 * N_DIFF_HEADS, tq, LANES), F32)],
    )
    return pl.pallas_call(
        functools.partial(_diff_kernel, out_scale=out_scale),
        grid_spec=grid_spec,
        out_shape=jax.ShapeDtypeStruct((b, t, w), BF16),
        compiler_params=_cparams(("arbitrary", "arbitrary")),
        name="diff_attention",
    )(itab, jtab, lam, q, k, v, bias, g_out)


def _out_kernel(x_ref, of_ref, od_ref, gf_ref, gd_ref, wf_ref, wd_ref, wo_ref, g2_ref, wr_ref, br_ref,
                y_ref, h2_ref, lg_ref):
    merged = (gf_ref[...] * jnp.dot(of_ref[...], wf_ref[...], preferred_element_type=F32)
              + gd_ref[...] * jnp.dot(od_ref[...], wd_ref[...], preferred_element_type=F32))
    y = x_ref[...] + jnp.dot(merged.astype(BF16), wo_ref[...], preferred_element_type=F32)
    y_ref[...] = y
    ms = jnp.mean(y * y, axis=-1, keepdims=True)
    h2 = y * lax.rsqrt(ms + RMS_EPS) * g2_ref[...]
    h2_ref[...] = h2
    lg_ref[...] = jnp.dot(h2.astype(BF16), wr_ref[...], preferred_element_type=F32) + br_ref[...]


def _out_projection(x2d, o_f, o_d, gate_f, gate_d, w_fu, w_du, w_o, g2, w_r, b_r, tm):
    n, d = x2d.shape
    tm = min(tm, n)
    assert n % tm == 0
    row = lambda i: (i, 0)
    const = lambda i: (0, 0)
    return pl.pallas_call(
        _out_kernel,
        grid=(n // tm,),
        in_specs=[
            pl.BlockSpec((tm, d), row), pl.BlockSpec((tm, o_f.shape[1]), row),
            pl.BlockSpec((tm, o_d.shape[1]), row), pl.BlockSpec((tm, d), row), pl.BlockSpec((tm, d), row),
            pl.BlockSpec(w_fu.shape, const), pl.BlockSpec(w_du.shape, const), pl.BlockSpec(w_o.shape, const),
            pl.BlockSpec((1, d), const), pl.BlockSpec(w_r.shape, const), pl.BlockSpec((1, LANES), const),
        ],
        out_specs=(pl.BlockSpec((tm, d), row), pl.BlockSpec((tm, d), row), pl.BlockSpec((tm, LANES), row)),
        out_shape=(jax.ShapeDtypeStruct((n, d), F32), jax.ShapeDtypeStruct((n, d), F32),
                   jax.ShapeDtypeStruct((n, LANES), F32)),
        compiler_params=_cparams(("arbitrary",)),
        name="out_projection",
    )(x2d, o_f, o_d, gate_f, gate_d, w_fu, w_du, w_o, g2, w_r, b_r)


def _deinterleave_kernel(w_ref, perm_ref, g_ref, u_ref):
    w = w_ref[0].astype(BF16)
    r = jnp.dot(w, perm_ref[...], preferred_element_type=F32).astype(BF16)
    half = r.shape[1] // 2
    g_ref[0] = r[:, :half]
    u_ref[0] = r[:, half:]


def _deinterleave_gate_up(w_gu, tn):
    e, d, f2 = w_gu.shape
    assert f2 % tn == 0
    half = tn // 2
    col = jnp.arange(tn)
    perm = (jnp.where(col % 2 == 0, col // 2, half + col // 2)[:, None] == col[None, :]).astype(BF16)
    out = jax.ShapeDtypeStruct((e, d, f2 // 2), BF16)
    return pl.pallas_call(
        _deinterleave_kernel,
        grid=(e, f2 // tn),
        in_specs=[pl.BlockSpec((1, d, tn), lambda i, j: (i, 0, j)),
                  pl.BlockSpec((tn, tn), lambda i, j: (0, 0))],
        out_specs=(pl.BlockSpec((1, d, half), lambda i, j: (i, 0, j)),
                   pl.BlockSpec((1, d, half), lambda i, j: (i, 0, j))),
        out_shape=(out, out),
        compiler_params=_cparams(("arbitrary", "arbitrary")),
        name="deinterleave_gate_up",
    )(w_gu, perm)


def _route_kernel(lg_ref, tril_ref, idx_ref, gate_ref, rank_ref, cnt_ref, carry_sc):
    @pl.when(pl.program_id(0) == 0)
    def _():
        carry_sc[...] = jnp.zeros_like(carry_sc)

    x = lg_ref[...]
    lane = lax.broadcasted_iota(jnp.int32, x.shape, 1)
    lane_f = lane.astype(F32)
    x = jnp.where(lane < N_EXPERTS, x, -jnp.inf)
    carry = carry_sc[0:1, :]
    vals, idxs, ranks = [], [], []
    for _ in range(TOP_K):
        m = jnp.max(x, axis=1, keepdims=True)
        idx = jnp.min(jnp.where(x == m, lane_f, float(LANES)), axis=1, keepdims=True)
        onehot = lane_f == idx
        oh = jnp.where(onehot, 1.0, 0.0)
        before = jnp.dot(tril_ref[...], oh.astype(BF16), preferred_element_type=F32)
        ranks.append(jnp.sum(jnp.where(onehot, before + carry, 0.0), axis=1, keepdims=True))
        carry = carry + jnp.sum(oh, axis=0, keepdims=True)
        x = jnp.where(onehot, -jnp.inf, x)
        vals.append(m)
        idxs.append(idx)
    carry_sc[0:1, :] = carry
    cnt_ref[...] = jnp.broadcast_to(carry, cnt_ref.shape)
    e = [jnp.exp(v - vals[0]) for v in vals]
    tot = e[0]
    for ek in e[1:]:
        tot = tot + ek
    gates = jnp.zeros(x.shape, F32)
    idx_out = jnp.zeros(x.shape, F32)
    rank_out = jnp.zeros(x.shape, F32)
    for k in range(TOP_K):
        gates = jnp.where(lane == k, e[k] / tot, gates)
        idx_out = jnp.where(lane == k, idxs[k], idx_out)
        rank_out = jnp.where(lane == k, ranks[k], rank_out)
    gate_ref[...] = gates
    idx_ref[...] = idx_out.astype(jnp.int32)
    rank_ref[...] = rank_out.astype(jnp.int32)


def _route(logits, tm):
    n = logits.shape[0]
    tm = min(tm, n)
    assert n % tm == 0
    tril = (jnp.arange(tm)[None, :] < jnp.arange(tm)[:, None]).astype(BF16)
    row = lambda i: (i, 0)
    idx, gates, rank, cnt = pl.pallas_call(
        _route_kernel,
        grid=(n // tm,),
        in_specs=[pl.BlockSpec((tm, LANES), row), pl.BlockSpec((tm, tm), lambda i: (0, 0))],
        out_specs=(pl.BlockSpec((tm, LANES), row), pl.BlockSpec((tm, LANES), row),
                   pl.BlockSpec((tm, LANES), row), pl.BlockSpec((8, LANES), lambda i: (0, 0))),
        out_shape=(jax.ShapeDtypeStruct((n, LANES), jnp.int32), jax.ShapeDtypeStruct((n, LANES), F32),
                   jax.ShapeDtypeStruct((n, LANES), jnp.int32), jax.ShapeDtypeStruct((8, LANES), F32)),
        scratch_shapes=[pltpu.VMEM((8, LANES), F32)],
        compiler_params=_cparams(("arbitrary",)),
        name="moe_route",
    )(logits, tril)
    return (idx[:, :TOP_K], gates[:, :TOP_K], rank[:, :TOP_K],
            cnt[0, :N_EXPERTS].astype(jnp.int32))


def _dispatch_kernel(slot_ref, pad_ref, h_ref, xs_out, sem):
    rows = h_ref.shape[0]
    n_pad = pad_ref.shape[0]

    @pl.when(pl.program_id(0) == 0)
    def _():
        def pad_body(r, carry):
            pltpu.make_async_copy(h_ref.at[pl.ds(0, 1)], xs_out.at[pl.ds(pad_ref[r], 1)], sem).start()
            return carry
        lax.fori_loop(0, n_pad, pad_body, 0, unroll=8)
        for _ in range(n_pad // rows):
            pltpu.make_async_copy(h_ref, xs_out.at[pl.ds(0, rows)], sem).wait()

    def body(r, carry):
        for k in range(TOP_K):
            s = slot_ref[0, 0, r * TOP_K + k]
            pltpu.make_async_copy(h_ref.at[pl.ds(r, 1)], xs_out.at[pl.ds(s, 1)], sem).start()
        return carry

    lax.fori_loop(0, rows, body, 0, unroll=2)
    for _ in range(TOP_K):
        pltpu.make_async_copy(h_ref, xs_out.at[pl.ds(0, rows)], sem).wait()


def _dispatch(h2, slot, pad_slots, n_slots, tm):
    n, d = h2.shape
    tm = min(tm, n)
    assert n % tm == 0 and pad_slots.shape[0] % tm == 0
    slot3 = slot.reshape(n // tm, 1, tm * TOP_K)
    return pl.pallas_call(
        _dispatch_kernel,
        grid=(n // tm,),
        in_specs=[pl.BlockSpec((1, 1, tm * TOP_K), lambda i: (i, 0, 0), memory_space=pltpu.SMEM),
                  pl.BlockSpec(memory_space=pltpu.SMEM),
                  pl.BlockSpec((tm, d), lambda i: (i, 0))],
        out_specs=pl.BlockSpec(memory_space=pl.ANY),
        out_shape=jax.ShapeDtypeStruct((n_slots, d), F32),
        scratch_shapes=[pltpu.SemaphoreType.DMA(())],
        compiler_params=_cparams(("arbitrary",)),
        name="moe_dispatch",
    )(slot3, pad_slots, h2)


def _row_gather_wait(src_hbm, n_rows, dst, sem):
    pltpu.make_async_copy(src_hbm.at[pl.ds(0, n_rows)], dst, sem).wait()


def _moe_kernel(be_ref, nused_ref, x_ref, wg_ref, wu_ref, bg_ref, bu_ref, wd_ref, bd_ref, y_ref):
    i = pl.program_id(0)
    n_used = nused_ref[0]

    @pl.when(i < n_used)
    def _():
        x = x_ref[...].astype(BF16)
        gate = jnp.dot(x, wg_ref[0], preferred_element_type=F32) + bg_ref[0]
        up = jnp.dot(x, wu_ref[0], preferred_element_type=F32) + bu_ref[0]
        gate = jnp.minimum(gate, SWIGLU_LIMIT)
        up = jnp.clip(up, -SWIGLU_LIMIT, SWIGLU_LIMIT)
        glu = gate * jax.nn.sigmoid(SWIGLU_ALPHA * gate)
        act = ((up + 1.0) * glu).astype(BF16)
        y_ref[...] = jnp.dot(act, wd_ref[0], preferred_element_type=F32) + bd_ref[0]

    @pl.when(i >= n_used)
    def _():
        y_ref[...] = jnp.zeros_like(y_ref)


def _expert_ffn(xs, block_expert, n_used, w_g, w_u, b_g, b_u, w_d, b_d, bm):
    n_slots, d = xs.shape
    n_blocks = block_expert.shape[0]
    dff = w_g.shape[2]
    wmap = lambda i, be, nu: (be[i], 0, 0)
    grid_spec = pltpu.PrefetchScalarGridSpec(
        num_scalar_prefetch=2,
        grid=(n_blocks,),
        in_specs=[
            pl.BlockSpec((bm, d), lambda i, be, nu: (i, 0)),
            pl.BlockSpec((1, d, dff), wmap), pl.BlockSpec((1, d, dff), wmap),
            pl.BlockSpec((1, 1, dff), wmap), pl.BlockSpec((1, 1, dff), wmap),
            pl.BlockSpec((1, dff, d), wmap), pl.BlockSpec((1, 1, d), wmap),
        ],
        out_specs=pl.BlockSpec((bm, d), lambda i, be, nu: (i, 0)),
    )
    return pl.pallas_call(
        _moe_kernel,
        grid_spec=grid_spec,
        out_shape=jax.ShapeDtypeStruct((n_slots, d), F32),
        compiler_params=_cparams(("arbitrary",)),
        name="expert_ffn",
    )(block_expert, n_used, xs, w_g, w_u, b_g, b_u, w_d, b_d)


def _combine_kernel(idx_ref, idx_next_ref, yb_hbm, resid_ref, gates_ref, o_ref, buf, sem):
    i = pl.program_id(0)
    n = pl.num_programs(0)
    tc = resid_ref.shape[0]
    slot = lax.rem(i, 2)

    def start(idx, par):
        def body(r, carry):
            pltpu.make_async_copy(yb_hbm.at[pl.ds(idx[0, 0, r], 1)], buf.at[par, pl.ds(r, 1)],
                                  sem.at[par]).start()
            return carry
        lax.fori_loop(0, TOP_K * tc, body, 0, unroll=8)

    @pl.when(i == 0)
    def _():
        start(idx_ref, 0)

    @pl.when(i + 1 < n)
    def _():
        start(idx_next_ref, 1 - slot)

    _row_gather_wait(yb_hbm, TOP_K * tc, buf.at[slot], sem.at[slot])
    acc = jnp.zeros(o_ref.shape, F32)
    for k in range(TOP_K):
        acc = acc + buf[slot, k * tc:(k + 1) * tc, :] * gates_ref[:, k:k + 1]
    o_ref[...] = resid_ref[...] + acc


def _combine(yb, slot_tm, gates, resid, tc):
    n, d = resid.shape
    n_tiles = n // tc
    rows = TOP_K * tc
    return pl.pallas_call(
        _combine_kernel,
        grid=(n_tiles,),
        in_specs=[
            pl.BlockSpec((1, 1, rows), lambda i: (i, 0, 0), memory_space=pltpu.SMEM),
            pl.BlockSpec((1, 1, rows), lambda i: (jnp.minimum(i + 1, n_tiles - 1), 0, 0),
                         memory_space=pltpu.SMEM),
            pl.BlockSpec(memory_space=pl.ANY),
            pl.BlockSpec((tc, d), lambda i: (i, 0)),
            pl.BlockSpec((tc, TOP_K), lambda i: (i, 0)),
        ],
        out_specs=pl.BlockSpec((tc, d), lambda i: (i, 0)),
        out_shape=jax.ShapeDtypeStruct((n, d), F32),
        scratch_shapes=[pltpu.VMEM((2, rows, d), F32), pltpu.SemaphoreType.DMA((2,))],
        compiler_params=_cparams(("arbitrary",)),
        name="moe_combine",
    )(slot_tm, slot_tm, yb, resid, gates)


def _moe_layer(y1, h2, logits, moe_w, bm, tc):
    n, d = y1.shape
    w_g, w_u, b_g, b_u, w_d, b_d = moe_w
    bm = min(bm, n)
    tc = min(tc, n)
    idx, gates, rank, counts = _route(logits, ROUTE_ROWS)
    padded = (counts + bm - 1) // bm * bm
    pend = jnp.cumsum(padded)
    pstart = pend - padded
    experts = jnp.arange(N_EXPERTS, dtype=jnp.int32)
    slot = rank + jnp.sum(jnp.where(idx[..., None] == experts, pstart, 0), axis=-1).astype(jnp.int32)
    n_blocks = -(-(n * TOP_K) // bm) + N_EXPERTS
    block_start = jnp.arange(n_blocks, dtype=jnp.int32) * bm
    block_expert = jnp.minimum(jnp.sum(pend[None, :] <= block_start[:, None], axis=1),
                               N_EXPERTS - 1).astype(jnp.int32)
    n_used = (pend[-1] // bm).astype(jnp.int32).reshape(1)
    assert (n * TOP_K) % bm == 0
    r = jnp.arange(bm, dtype=jnp.int32)
    is_pad = r[None, :] < (padded - counts)[:, None]
    tail_rank = (jnp.cumsum(jnp.logical_not(is_pad).reshape(-1)) - 1).reshape(N_EXPERTS, bm)
    pad_slots = jnp.where(is_pad, (pstart + counts)[:, None] + r[None, :], pend[-1] + tail_rank)
    xs = _dispatch(h2, slot, pad_slots.reshape(-1).astype(jnp.int32), n_blocks * bm, DISPATCH_ROWS)
    yb = _expert_ffn(xs, block_expert, n_used, w_g, w_u, b_g, b_u, w_d, b_d, bm)
    slot_km = slot.reshape(n // tc, tc, TOP_K).transpose(0, 2, 1).reshape(n // tc, 1, TOP_K * tc)
    return _combine(yb, slot_km, gates, y1, tc)


def _sample_attn_kernel(pt_ref, lam_ref, qsf_ref, qsd_ref, cncol_ref, cnrow_ref, snew_ref, blast_ref,
                        knf_ref, vnf_ref, knd_ref, vnd_ref, tri_ref, go_ref, *rest, pps, out_scale):
    page_refs = rest[:5 * pps]
    of_ref, od_ref, m_sc, l_sc, accf_sc, accd_sc, carry_sc = rest[5 * pps:]
    kf_refs = page_refs[0 * pps:1 * pps]
    vf_refs = page_refs[1 * pps:2 * pps]
    kd_refs = page_refs[2 * pps:3 * pps]
    vd_refs = page_refs[3 * pps:4 * pps]
    lft_refs = page_refs[4 * pps:5 * pps]
    g = pl.program_id(1)
    n_g = pl.num_programs(1)
    nh = N_FOX_HEADS
    nd_heads = N_DIFF_HEADS
    half_rows = cncol_ref.shape[1] // 2
    tok = half_rows // nh

    @pl.when(g == 0)
    def _():
        m_sc[...] = jnp.full_like(m_sc, -jnp.inf)
        l_sc[...] = jnp.zeros_like(l_sc)
        accf_sc[...] = jnp.zeros_like(accf_sc)
        accd_sc[...] = jnp.zeros_like(accd_sc)
        carry_sc[...] = jnp.zeros_like(carry_sc)

    _NN = (((1,), (0,)), ((), ()))

    def scores(kf_heads, kd_heads, fox_dims):
        rows = [lax.dot_general(qsf_ref[0, h], kf_heads[h], fox_dims, preferred_element_type=F32)
                for h in range(nh)]
        rows += [lax.dot_general(qsd_ref[0, h], kd_heads[h], _NT, preferred_element_type=F32)
                 for h in range(nd_heads)]
        return jnp.concatenate(rows, axis=0)

    def update(s, vf_heads, vd_heads, fox_dims):
        p, alpha = _online_softmax(s, m_sc, l_sc, slice(None))
        pv_f = jnp.concatenate(
            [lax.dot_general(p[h * tok:(h + 1) * tok].astype(BF16), vf_heads[h], fox_dims,
                             preferred_element_type=F32) for h in range(nh)], axis=0)
        pv_d = jnp.concatenate(
            [jnp.dot(p[half_rows + 2 * h * tok:half_rows + 2 * (h + 1) * tok].astype(BF16), vd_heads[h],
                     preferred_element_type=F32) for h in range(nd_heads)], axis=0)
        accf_sc[...] = alpha[:half_rows, :HEAD_DIM] * accf_sc[...] + pv_f
        accd_sc[...] = alpha[half_rows:] * accd_sc[...] + pv_d

    def fox_page_heads(refs):
        return [jnp.concatenate([ref[0, 0, h] for ref in refs], axis=1).astype(BF16) for h in range(nh)]

    def diff_page_heads(refs):
        page = refs[0].shape[2] // nd_heads
        return [jnp.concatenate([ref[0, 0, pl.ds(h, page, stride=nd_heads), :] for ref in refs],
                                axis=0).astype(BF16) for h in range(nd_heads)]

    def past_chunk(with_bias):
        carry = carry_sc[...]
        suf = [None] * pps
        for p in reversed(range(pps)):
            x = lft_refs[p][0]
            x3 = jnp.concatenate(_split3(x), axis=0)
            y3 = jnp.dot(x3, tri_ref[...], preferred_element_type=F32)
            within = y3[0:nh] + y3[nh:2 * nh] + y3[2 * nh:3 * nh]
            suf[p] = within + carry
            carry = carry + (within[:, 0:1] + x[:, 0:1])
        carry_sc[...] = carry
        sufc = jnp.concatenate(suf, axis=1) * LOG2E
        r = sufc.shape[1]
        suf_rows = jnp.concatenate(
            [jnp.broadcast_to(sufc[h:h + 1, :], (tok, r)) for h in range(nh)]
            + [jnp.zeros((half_rows, r), F32)], axis=0)
        s = scores(fox_page_heads(kf_refs), diff_page_heads(kd_refs), _NN)
        s = s + _lane_tile(cncol_ref[0], r // LANES) + suf_rows
        if with_bias:
            s = s + blast_ref[...]
        update(s, fox_page_heads(vf_refs), diff_page_heads(vd_refs), _NT)

    @pl.when(g == 0)
    def _():
        past_chunk(True)

    @pl.when(g > 0)
    def _():
        past_chunk(False)

    @pl.when(g == n_g - 1)
    def _():
        def new_heads(ref, n_heads):
            width = ref.shape[2] // n_heads
            return [ref[0, :, h * width:(h + 1) * width].astype(BF16) for h in range(n_heads)]

        s = scores(new_heads(knf_ref, nh), new_heads(knd_ref, nd_heads), _NT)
        s = s + (cncol_ref[0] - cnrow_ref[0]) + snew_ref[...]
        update(s, new_heads(vnf_ref, nh), new_heads(vnd_ref, nd_heads), _NN)

        l = l_sc[...]
        nf = accf_sc[...] / l[:half_rows, :HEAD_DIM]
        for h in range(nh):
            of_ref[0, :, h * HEAD_DIM:(h + 1) * HEAD_DIM] = nf[h * tok:(h + 1) * tok].astype(of_ref.dtype)

        nd = accd_sc[...] / l[half_rows:]
        lam = lam_ref[0]
        for h in range(N_DIFF_HEADS):
            cols = slice(h * LANES, (h + 1) * LANES)
            a1 = nd[(2 * h) * tok:(2 * h + 1) * tok]
            a2 = nd[(2 * h + 1) * tok:(2 * h + 2) * tok]
            o = a1 - lam * a2
            ms = jnp.mean(o * o, axis=-1, keepdims=True)
            od_ref[0, :, cols] = (o * lax.rsqrt(ms + RMS_EPS) * go_ref[...] * out_scale).astype(od_ref.dtype)


def _sample_attention(qf, qd, kf_new, vf_new, kd_new, vd_new, c_new, caches, page_table, rel_table, lam,
                      g_out, out_scale, pps, layer):
    db, s_new, w = qf.shape
    ck_f, cv_f, ck_d, cv_d, clf_t = caches
    page = ck_f.shape[2]
    ck_f, cv_f = (jnp.transpose(a, (0, 1, 3, 4, 2)) for a in (ck_f, cv_f))
    ck_d, cv_d = (a.reshape(a.shape[0], a.shape[1], page * N_DIFF_HEADS, 2 * HEAD_DIM) for a in (ck_d, cv_d))
    n_pages = page_table.shape[1]
    pps = min(pps, n_pages)
    assert n_pages % pps == 0 and page == LANES and s_new * N_FOX_HEADS * 2 == LANES
    n_g = n_pages // pps
    r = pps * page
    nh = N_FOX_HEADS
    half_rows = nh * s_new

    qsf = jnp.transpose(qf.reshape(db, s_new, nh, HEAD_DIM), (0, 2, 1, 3))
    qd_h = jnp.transpose(qd.reshape(db, s_new, N_DIFF_HEADS, 2 * HEAD_DIM), (0, 2, 1, 3))
    first = (jnp.arange(2 * HEAD_DIM) < HEAD_DIM)
    qsd = jnp.concatenate([jnp.where(first, qd_h, jnp.zeros((), qd.dtype)),
                           jnp.where(first, jnp.zeros((), qd.dtype), qd_h)], axis=2)

    c_new = c_new * LOG2E
    c_rows =jnp.transpose(c_new, (0, 2, 1)).reshape(db, half_rows)
    cn_col = jnp.concatenate([c_rows, jnp.zeros_like(c_rows)], axis=1)[:, :, None]
    cn_col = jnp.broadcast_to(cn_col, (db, LANES, LANES))
    c_keys = jnp.repeat(jnp.transpose(c_new, (0, 2, 1)), s_new, axis=1)
    cn_row = jnp.zeros((db, LANES, LANES), F32).at[:, :half_rows, :s_new].set(c_keys)

    t_of_row = jnp.arange(LANES, dtype=jnp.int32) % s_new
    key = jnp.arange(LANES, dtype=jnp.int32)
    valid_new = (key[None, :] <= t_of_row[:, None]) & (key[None, :] < s_new)
    is_diff = (jnp.arange(LANES) >= half_rows)
    head_of_row = jnp.clip((jnp.arange(LANES) - half_rows) // (2 * s_new), 0, N_DIFF_HEADS - 1)

    def rel_rows(dist):
        per_head = _rel_bias_lookup(rel_table, dist)
        vals = jnp.zeros(dist.shape, F32)
        for h in range(N_DIFF_HEADS):
            vals = jnp.where(head_of_row[:, None] == h, per_head[h], vals)
        return jnp.where(is_diff[:, None], vals, 0.0)

    s_new_tile = jnp.where(valid_new, rel_rows(t_of_row[:, None] - key[None, :]), -jnp.inf)
    val = _rel_bias_lookup(rel_table, jnp.arange(r + s_new, dtype=jnp.int32))
    per_head = _toeplitz(val, s_new, r, r)
    b_last = jnp.concatenate(
        [jnp.zeros((half_rows, r), F32),
         jnp.broadcast_to(per_head[:, None], (N_DIFF_HEADS, 2, s_new, r)).reshape(half_rows, r)], axis=0)

    def pad_rows(x):
        return jnp.pad(x, ((0, 0), (0, LANES - s_new), (0, 0)))

    tri = (jnp.arange(page)[:, None] > jnp.arange(page)[None, :]).astype(BF16)

    bmap = lambda b, g, pt: (b, 0, 0)
    bmap4 = lambda b, g, pt: (b, 0, 0, 0)
    const2 = lambda b, g, pt: (0, 0)

    def page_map(p):
        return lambda b, g, pt: (pt[b, (n_g - 1 - g) * pps + p], 0, 0)

    def cache_map(p, ndim):
        return lambda b, g, pt: (layer, pt[b, (n_g - 1 - g) * pps + p]) + (0,) * (ndim - 2)

    in_specs = [
        pl.BlockSpec(memory_space=pltpu.SMEM),
        pl.BlockSpec((1,) + qsf.shape[1:], bmap4), pl.BlockSpec((1,) + qsd.shape[1:], bmap4),
        pl.BlockSpec((1, LANES, LANES), bmap), pl.BlockSpec((1, LANES, LANES), bmap),
        pl.BlockSpec((LANES, LANES), const2), pl.BlockSpec((LANES, r), const2),
        pl.BlockSpec((1, LANES, w), bmap), pl.BlockSpec((1, LANES, w), bmap),
        pl.BlockSpec((1, LANES, w), bmap), pl.BlockSpec((1, LANES, w), bmap),
        pl.BlockSpec((page, page), const2), pl.BlockSpec((1, LANES), const2),
    ]
    operands = [lam, qsf, qsd, cn_col, cn_row, s_new_tile, b_last,
                pad_rows(kf_new), pad_rows(vf_new), pad_rows(kd_new), pad_rows(vd_new), tri, g_out]
    for arr in (ck_f, cv_f, ck_d, cv_d):
        for p in range(pps):
            in_specs.append(pl.BlockSpec((1, 1) + arr.shape[2:], cache_map(p, arr.ndim)))
            operands.append(arr)
    for p in range(pps):
        in_specs.append(pl.BlockSpec((1, nh, page), page_map(p)))
        operands.append(clf_t)

    grid_spec = pltpu.PrefetchScalarGridSpec(
        num_scalar_prefetch=1,
        grid=(db, n_g),
        in_specs=in_specs,
        out_specs=(pl.BlockSpec((1, s_new, w), bmap), pl.BlockSpec((1, s_new, w), bmap)),
        scratch_shapes=[pltpu.VMEM((LANES, LANES), F32), pltpu.VMEM((LANES, LANES), F32),
                        pltpu.VMEM((half_rows, HEAD_DIM), F32), pltpu.VMEM((half_rows, 2 * HEAD_DIM), F32),
                        pltpu.VMEM((nh, LANES), F32)],
    )
    return pl.pallas_call(
        functools.partial(_sample_attn_kernel, pps=pps, out_scale=out_scale),
        grid_spec=grid_spec,
        out_shape=(jax.ShapeDtypeStruct((db, s_new, w), BF16), jax.ShapeDtypeStruct((db, s_new, w), BF16)),
        compiler_params=_cparams(("arbitrary", "arbitrary")),
        name="sample_attention",
    )(page_table, *operands)


PROJ_ROWS = 256
ATTN_BLOCK = 512
SCAN_BLOCK = 512
MOE_ROWS = 256
COMBINE_ROWS = 256
ROUTE_ROWS = 512
DISPATCH_ROWS = 512
DEINTERLEAVE_COLS = 512
PAGES_PER_STEP = 8


def _split_points(d_model):
    sizes = (FOX_WIDTH, FOX_WIDTH, FOX_WIDTH, N_FOX_HEADS, DIFF_WIDTH, DIFF_WIDTH, DIFF_WIDTH, d_model)
    pts, acc = [], 0
    for s in sizes:
        acc += s
        pts.append(acc)
    return pts


def _layer_weights(l, d_model, attn_norm_g, w_in, b_forget, fox_q_norm_g, fox_k_norm_g, diff_q_norm_g,
                   diff_k_norm_g, lambda_q1, lambda_k1, lambda_q2, lambda_k2, diff_out_norm_g, w_fox_up,
                   w_diff_up, w_out, ffn_norm_g, w_router, b_router, w_gate_up, b_gate_up, w_down, b_down):
    lambda_init = 0.8 - 0.6 * math.exp(-0.3 * l)
    wq_f, wk_f, wv_f, w_fl, wq_d, wk_d, wv_d, wg_f, wg_d = jnp.split(w_in[l], _split_points(d_model), axis=1)
    w_main = jnp.concatenate([wq_f, wk_f, wv_f, wq_d, wk_d, wv_d, wg_f, wg_d], axis=1).astype(BF16)
    w_fl = jnp.pad(w_fl, ((0, 0), (0, LANES - N_FOX_HEADS))).astype(BF16)
    b_fl = jnp.pad(b_forget[l].astype(F32), (0, LANES - N_FOX_HEADS)).reshape(1, LANES)
    grp = jnp.arange(FOX_WIDTH) // HEAD_DIM
    ones_bd = (grp[:, None] == grp[None, :]).astype(BF16)
    reps = FOX_WIDTH // HEAD_DIM
    gains = jnp.stack([jnp.tile(g[l].astype(F32), reps)
                       for g in (fox_q_norm_g, fox_k_norm_g, diff_q_norm_g, diff_k_norm_g)])
    lam = (jnp.exp(jnp.sum(lambda_q1[l].astype(F32) * lambda_k1[l].astype(F32)))
           - jnp.exp(jnp.sum(lambda_q2[l].astype(F32) * lambda_k2[l].astype(F32))) + lambda_init)
    w_r = jnp.pad(w_router[l], ((0, 0), (0, LANES - N_EXPERTS))).astype(BF16)
    b_r = jnp.pad(b_router[l].astype(F32), (0, LANES - N_EXPERTS)).reshape(1, LANES)
    w_g, w_u = _deinterleave_gate_up(w_gate_up[l], DEINTERLEAVE_COLS)
    bgu = b_gate_up[l].astype(F32)
    moe_w = (w_g, w_u, bgu[:, None, 0::2], bgu[:, None, 1::2],
             w_down[l].astype(BF16), b_down[l].astype(F32)[:, None, :])
    return dict(
        lambda_init=lambda_init, g1=attn_norm_g[l].astype(F32).reshape(1, d_model), w_main=w_main, w_fl=w_fl,
        b_fl=b_fl, ones_bd=ones_bd, gains=gains, lam=lam.reshape(1).astype(F32),
        g_out=diff_out_norm_g[l].astype(F32).reshape(1, LANES),
        w_fu=w_fox_up[l].astype(BF16), w_du=w_diff_up[l].astype(BF16), w_o=w_out[l].astype(BF16),
        g2=ffn_norm_g[l].astype(F32).reshape(1, d_model), w_r=w_r, b_r=b_r, moe_w=moe_w)


def _project(x, lw):
    b, t, d = x.shape
    outs = _in_projection(x.reshape(b * t, d), lw['g1'], lw['w_main'], lw['w_fl'], lw['b_fl'],
                          lw['ones_bd'], lw['gains'], PROJ_ROWS, t)
    qf, kf, vf, lf, qd, kd, vd, gf, gd = outs[:9]
    logf = lf[:, :N_FOX_HEADS].reshape(b, t, N_FOX_HEADS)
    r3 = lambda a: a.reshape(b, t, a.shape[-1])
    if len(outs) > 9:
        rows4 = lambda a: jnp.transpose(a.reshape(b, N_FOX_HEADS, HEAD_DIM, t), (0, 3, 1, 2))
        fox_rows = (rows4(outs[9]), rows4(outs[10]))
    else:
        fox_rows = (kf.reshape(b, t, N_FOX_HEADS, HEAD_DIM), vf.reshape(b, t, N_FOX_HEADS, HEAD_DIM))
    return r3(qf), r3(kf), r3(vf), logf, r3(qd), r3(kd), r3(vd), gf, gd, fox_rows


def _finish_layer(x, o_f, o_d, gf, gd, lw, moe_rows):
    b, t, d = x.shape
    n = b * t
    y1, h2, logits = _out_projection(x.reshape(n, d), o_f.reshape(n, -1), o_d.reshape(n, -1), gf, gd,
                                     lw['w_fu'], lw['w_du'], lw['w_o'], lw['g2'], lw['w_r'], lw['b_r'],
                                     PROJ_ROWS)
    return _moe_layer(y1, h2, logits, lw['moe_w'], moe_rows, COMBINE_ROWS).reshape(b, t, d)


def _new_rows(fox_rows, logf, kd, vd):
    b, t, _ = kd.shape
    return (fox_rows[0], fox_rows[1], logf,
            kd.reshape(b, t, N_DIFF_HEADS, 2 * HEAD_DIM), vd.reshape(b, t, N_DIFF_HEADS, 2 * HEAD_DIM))


def _prompt_layer(x, lw, rel_table):
    qf, kf, vf, logf, qd, kd, vd, gf, gd, fox_rows = _project(x, lw)
    ct = _prefix_sum_time(jnp.transpose(logf, (0, 2, 1)), SCAN_BLOCK)
    c = jnp.transpose(ct, (0, 2, 1))
    o_f = _fox_attention(qf, kf, vf, c, ATTN_BLOCK)
    o_d = _diff_attention(qd, kd, vd, rel_table, lw['lam'], lw['g_out'], 1.0 - lw['lambda_init'], ATTN_BLOCK)
    return _finish_layer(x, o_f, o_d, gf, gd, lw, MOE_ROWS), _new_rows(fox_rows, logf, kd, vd)


def _sample_layer(x, lw, rel_table, caches, page_table, layer):
    qf, kf, vf, logf, qd, kd, vd, gf, gd, fox_rows = _project(x, lw)
    b, t, _ = logf.shape
    logf_t = jnp.pad(jnp.transpose(logf, (0, 2, 1)), ((0, 0), (0, 0), (0, LANES - t)))
    c = jnp.transpose(_prefix_sum_time(logf_t, LANES)[:, :, :t], (0, 2, 1))
    o_f, o_d = _sample_attention(qf, qd, kf, vf, kd, vd, c, caches, page_table, rel_table, lw['lam'],
                                 lw['g_out'], 1.0 - lw['lambda_init'], PAGES_PER_STEP, layer)
    return _finish_layer(x, o_f, o_d, gf, gd, lw, LANES), _new_rows(fox_rows, logf, kd, vd)


def kernel(x_prompt, x_sample, cache_fox_k, cache_fox_v, cache_fox_logf, cache_diff_k, cache_diff_v, page_table, rel_bias_table, attn_norm_g, w_in, b_forget, fox_q_norm_g, fox_k_norm_g, diff_q_norm_g, diff_k_norm_g, lambda_q1, lambda_k1, lambda_q2, lambda_k2, diff_out_norm_g, w_fox_up, w_diff_up, w_out, ffn_norm_g, w_router, b_router, w_gate_up, b_gate_up, w_down, b_down):
    depth = w_in.shape[0]
    d_model = x_prompt.shape[-1]
    yp, ys = x_prompt, x_sample
    rows_p, rows_s = [], []
    for l in range(depth):
        lw = _layer_weights(l, d_model, attn_norm_g, w_in, b_forget, fox_q_norm_g, fox_k_norm_g,
                            diff_q_norm_g, diff_k_norm_g, lambda_q1, lambda_k1, lambda_q2, lambda_k2,
                            diff_out_norm_g, w_fox_up, w_diff_up, w_out, ffn_norm_g, w_router, b_router,
                            w_gate_up, b_gate_up, w_down, b_down)
        yp, rp = _prompt_layer(yp, lw, rel_bias_table)
        caches = (cache_fox_k, cache_fox_v, cache_diff_k, cache_diff_v,
                  jnp.transpose(cache_fox_logf[l].astype(F32), (0, 2, 1)))
        ys, rs = _sample_layer(ys, lw, rel_bias_table, caches, page_table, l)
        rows_p.append(rp)
        rows_s.append(rs)
    stack = lambda rows, i: jnp.stack([r[i] for r in rows])
    return (yp, ys) + tuple(stack(rows_p, i) for i in range(5)) + tuple(stack(rows_s, i) for i in range(5))
```

```python
import functools
import math

import jax
import jax.numpy as jnp
from jax import lax
from jax.experimental import pallas as pl
from jax.experimental.pallas import tpu as pltpu

F32 = jnp.float32
BF16 = jnp.bfloat16

HEAD_DIM = 64
N_FOX_HEADS = 8
N_DIFF_HEADS = 4
FOX_WIDTH = N_FOX_HEADS * HEAD_DIM
DIFF_WIDTH = N_DIFF_HEADS * 2 * HEAD_DIM
N_BUCKETS = 32
MAX_DISTANCE = 128
N_EXPERTS = 32
TOP_K = 4
SWIGLU_LIMIT = 7.0
SWIGLU_ALPHA = 1.702
RMS_EPS = 1e-6
ATTN_SCALE = HEAD_DIM ** -0.5
LOG2E = math.log2(math.e)
Q_SCALE = ATTN_SCALE * LOG2E
LANES = 128
VMEM_LIMIT = 56 * 1024 * 1024


def _cparams(sem):
    return pltpu.CompilerParams(dimension_semantics=sem, vmem_limit_bytes=VMEM_LIMIT)


def _split3(x):
    x1 = x.astype(BF16)
    r1 = x - x1.astype(F32)
    x2 = r1.astype(BF16)
    x3 = (r1 - x2.astype(F32)).astype(BF16)
    return x1, x2, x3


def _log_sigmoid(x):
    return -(jnp.maximum(-x, 0.0) + jnp.log1p(jnp.exp(-jnp.abs(x))))


def _proj_kernel(x_ref, g_ref, w_ref, wfl_ref, bfl_ref, ones_ref, gains_ref,
                 qf_ref, kf_ref, vf_ref, lf_ref, qd_ref, kd_ref, vd_ref, gf_ref, gd_ref, *t_refs):
    x = x_ref[...]
    ms = jnp.mean(x * x, axis=-1, keepdims=True)
    xn = (x * lax.rsqrt(ms + RMS_EPS) * g_ref[...]).astype(BF16)

    def seg(lo, width):
        return jnp.dot(xn, w_ref[:, lo:lo + width], preferred_element_type=F32)

    ones_bd = ones_ref[...]

    def head_norm(p, gain):
        sq = p * p
        hi = sq.astype(BF16)
        lo = (sq - hi.astype(F32)).astype(BF16)
        ss = (jnp.dot(hi, ones_bd, preferred_element_type=F32)
              + jnp.dot(lo, ones_bd, preferred_element_type=F32))
        return p * lax.rsqrt(ss * (1.0 / HEAD_DIM) + RMS_EPS) * gain

    w = FOX_WIDTH
    qf_ref[...] = (head_norm(seg(0, w), gains_ref[0:1, :]) * Q_SCALE).astype(BF16)
    kf = head_norm(seg(w, w), gains_ref[1:2, :])
    vf = seg(2 * w, w)
    kf_ref[...] = kf
    vf_ref[...] = vf
    if t_refs:
        t_refs[0][0] = kf.T
        t_refs[1][0] = vf.T
    qd_ref[...] = (head_norm(seg(3 * w, w), gains_ref[2:3, :]) * Q_SCALE).astype(BF16)
    kd_ref[...] = head_norm(seg(4 * w, w), gains_ref[3:4, :])
    vd_ref[...] = seg(5 * w, w)
    d = gf_ref.shape[1]
    gf_ref[...] = jax.nn.sigmoid(seg(6 * w, d))
    gd_ref[...] = jax.nn.sigmoid(seg(6 * w + d, d))
    fl = jnp.dot(xn, wfl_ref[...], preferred_element_type=F32) + bfl_ref[...]
    lf_ref[...] = _log_sigmoid(fl)


def _in_projection(x2d, g, w_main, w_fl, b_fl, ones_bd, gains, tm, seq):
    n, d = x2d.shape
    w = FOX_WIDTH
    tm = min(tm, n)
    assert n % tm == 0
    row = lambda i: (i, 0)
    const = lambda i: (0, 0)
    out_shape = (
        jax.ShapeDtypeStruct((n, w), BF16), jax.ShapeDtypeStruct((n, w), F32),
        jax.ShapeDtypeStruct((n, w), F32), jax.ShapeDtypeStruct((n, LANES), F32),
        jax.ShapeDtypeStruct((n, w), BF16), jax.ShapeDtypeStruct((n, w), F32),
        jax.ShapeDtypeStruct((n, w), F32), jax.ShapeDtypeStruct((n, d), F32),
        jax.ShapeDtypeStruct((n, d), F32),
    )
    out_specs = tuple(pl.BlockSpec((tm, s.shape[1]), row) for s in out_shape)
    if seq % tm == 0:
        tiles = seq // tm
        t_shape = jax.ShapeDtypeStruct((n // seq, w, seq), F32)
        t_spec = pl.BlockSpec((1, w, tm), lambda i: (i // tiles, 0, i % tiles))
        out_shape += (t_shape, t_shape)
        out_specs += (t_spec, t_spec)
    return pl.pallas_call(
        _proj_kernel,
        grid=(n // tm,),
        in_specs=[
            pl.BlockSpec((tm, d), row),
            pl.BlockSpec((1, d), const),
            pl.BlockSpec(w_main.shape, const),
            pl.BlockSpec(w_fl.shape, const),
            pl.BlockSpec((1, LANES), const),
            pl.BlockSpec(ones_bd.shape, const),
            pl.BlockSpec(gains.shape, const),
        ],
        out_specs=out_specs,
        out_shape=out_shape,
        compiler_params=_cparams(("arbitrary",)),
        name="in_projection",
    )(x2d, g, w_main, w_fl, b_fl, ones_bd, gains)


def _scan_kernel(x_ref, tri_ref, o_ref, carry_sc):
    @pl.when(pl.program_id(1) == 0)
    def _():
        carry_sc[...] = jnp.zeros_like(carry_sc)

    x = x_ref[0]
    h = x.shape[0]
    x3 = jnp.concatenate(_split3(x), axis=0)
    y3 = jnp.dot(x3, tri_ref[...], preferred_element_type=F32)
    y = y3[0:h] + y3[h:2 * h] + y3[2 * h:3 * h] + carry_sc[:, 0:1]
    o_ref[0] = y
    carry_sc[...] = jnp.broadcast_to(y[:, -1:], carry_sc.shape)


def _prefix_sum_time(xt, tl):
    b, h, t = xt.shape
    tl = min(tl, t)
    assert t % tl == 0
    tri = (jnp.arange(tl)[:, None] <= jnp.arange(tl)[None, :]).astype(BF16)
    return pl.pallas_call(
        _scan_kernel,
        grid=(b, t // tl),
        in_specs=[pl.BlockSpec((1, h, tl), lambda i, j: (i, 0, j)),
                  pl.BlockSpec((tl, tl), lambda i, j: (0, 0))],
        out_specs=pl.BlockSpec((1, h, tl), lambda i, j: (i, 0, j)),
        out_shape=jax.ShapeDtypeStruct((b, h, t), F32),
        scratch_shapes=[pltpu.VMEM((h, LANES), F32)],
        compiler_params=_cparams(("arbitrary", "arbitrary")),
        name="logf_prefix_sum",
    )(xt, tri)


def _lane_tile(x, n):
    return x if n == 1 else jnp.concatenate([x] * n, axis=1)


def _online_softmax(s, m_ref, l_ref, idx):
    tk = s.shape[1]
    m_prev = m_ref[idx]
    m_next = jnp.maximum(m_prev, jnp.max(s, axis=1, keepdims=True))
    p = jnp.exp2(s - _lane_tile(m_next, tk // LANES))
    alpha = jnp.exp2(m_prev - m_next)
    l_ref[idx] = alpha * l_ref[idx] + jnp.sum(p, axis=1, keepdims=True)
    m_ref[idx] = m_next
    return p, alpha


def _half_mask(shape_lanes):
    lane = lax.broadcasted_iota(jnp.int32, (1, shape_lanes), 1)
    return lane < HEAD_DIM


_NT = (((1,), (1,)), ((), ()))


FOX_BIAS_LANES = 8


def _fox_bias_operands(c):
    b, t, h = c.shape
    to_bf16 = lambda x: lax.reduce_precision(x, exponent_bits=8, mantissa_bits=7)
    c1 = to_bf16(c)
    c2 = to_bf16(c - c1)
    c3 = to_bf16(c - c1 - c2)
    pieces = jnp.stack([c1, c2, c3], axis=-1).astype(BF16)
    ones = jnp.ones_like(pieces)
    pad = jnp.zeros(pieces.shape[:-1] + (FOX_BIAS_LANES - 6,), BF16)
    qx = jnp.concatenate([pieces, ones, pad], axis=-1).reshape(b, t, h * FOX_BIAS_LANES)
    kx = jnp.concatenate([ones, -pieces, pad], axis=-1).reshape(b, t, h * FOX_BIAS_LANES)
    fill = jnp.zeros((b, t, LANES - h * FOX_BIAS_LANES), BF16)
    return jnp.concatenate([qx, fill], axis=-1), jnp.concatenate([kx, fill], axis=-1)


def _causal_pairs(nq):
    pairs = [(i, j) for i in range(nq) for j in range(i + 1)]
    return (jnp.asarray([p[0] for p in pairs], jnp.int32), jnp.asarray([p[1] for p in pairs], jnp.int32))


def _fox_kernel(it_ref, jt_ref, q_ref, k_ref, v_ref, qx_ref, kx_ref, o_ref, qaug_sc, m_sc, l_sc, acc_sc):
    i = it_ref[pl.program_id(1)]
    j = jt_ref[pl.program_id(1)]
    tq = q_ref.shape[1]
    tk = k_ref.shape[1]
    first_half = _half_mask(LANES)

    @pl.when(j == 0)
    def _():
        m_sc[...] = jnp.full_like(m_sc, -jnp.inf)
        l_sc[...] = jnp.zeros_like(l_sc)
        acc_sc[...] = jnp.zeros_like(acc_sc)
        qx = qx_ref[0]
        lane_head = lax.broadcasted_iota(jnp.int32, (1, LANES), 1) // FOX_BIAS_LANES
        for h in range(N_FOX_HEADS):
            q2 = q_ref[0, :, (h // 2) * LANES:(h // 2 + 1) * LANES]
            half = first_half if h % 2 == 0 else jnp.logical_not(first_half)
            qaug_sc[h, :, :LANES] = jnp.where(half, q2, jnp.zeros_like(q2))
            qaug_sc[h, :, LANES:] = jnp.where(lane_head == h, qx, jnp.zeros_like(qx))

    def step(masked):
        if masked:
            row = lax.broadcasted_iota(jnp.int32, (tq, tk), 0)
            col = lax.broadcasted_iota(jnp.int32, (tq, tk), 1)
            keep = col <= row
        kx = kx_ref[0]
        for hp in range(N_FOX_HEADS // 2):
            cols = slice(hp * LANES, (hp + 1) * LANES)
            kaug = jnp.concatenate([k_ref[0, :, cols].astype(BF16), kx], axis=1)
            v2 = v_ref[0, :, cols].astype(BF16)
            pv = []
            alphas = []
            for sub in range(2):
                h = 2 * hp + sub
                s = lax.dot_general(qaug_sc[h], kaug, _NT, preferred_element_type=F32)
                if masked:
                    s = jnp.where(keep, s, -jnp.inf)
                p, alpha = _online_softmax(s, m_sc, l_sc, h)
                pv.append(jnp.dot(p.astype(BF16), v2, preferred_element_type=F32))
                alphas.append(alpha)
            acc_sc[hp] = (jnp.where(first_half, alphas[0], alphas[1]) * acc_sc[hp]
                          + jnp.where(first_half, pv[0], pv[1]))

    @pl.when(j < i)
    def _():
        step(False)

    @pl.when(j == i)
    def _():
        step(True)
        for hp in range(N_FOX_HEADS // 2):
            l2 = jnp.where(first_half, l_sc[2 * hp], l_sc[2 * hp + 1])
            o_ref[0, :, hp * LANES:(hp + 1) * LANES] = (acc_sc[hp] / l2).astype(o_ref.dtype)


def _fox_attention(q, k, v, c, tq):
    b, t, w = q.shape
    tq = min(tq, t)
    assert t % tq == 0 and tq % LANES == 0
    nq = t // tq
    qx, kx = _fox_bias_operands(c * LOG2E)
    itab, jtab = _causal_pairs(nq)
    qmap = lambda bi, p, it, jt: (bi, it[p], 0)
    kmap = lambda bi, p, it, jt: (bi, jt[p], 0)
    grid_spec = pltpu.PrefetchScalarGridSpec(
        num_scalar_prefetch=2,
        grid=(b, itab.shape[0]),
        in_specs=[
            pl.BlockSpec((1, tq, w), qmap),
            pl.BlockSpec((1, tq, w), kmap),
            pl.BlockSpec((1, tq, w), kmap),
            pl.BlockSpec((1, tq, LANES), qmap),
            pl.BlockSpec((1, tq, LANES), kmap),
        ],
        out_specs=pl.BlockSpec((1, tq, w), qmap),
        scratch_shapes=[pltpu.VMEM((N_FOX_HEADS, tq, 2 * LANES), BF16),
                        pltpu.VMEM((N_FOX_HEADS, tq, LANES), F32),
                        pltpu.VMEM((N_FOX_HEADS, tq, LANES), F32),
                        pltpu.VMEM((N_FOX_HEADS // 2, tq, LANES), F32)],
    )
    return pl.pallas_call(
        _fox_kernel,
        grid_spec=grid_spec,
        out_shape=jax.ShapeDtypeStruct((b, t, w), BF16),
        compiler_params=_cparams(("arbitrary", "arbitrary")),
        name="fox_attention",
    )(itab, jtab, q, k, v, qx, kx)


def _diff_finalize(acc1, l1, acc2, l2, lam, g_out, out_scale):
    o = acc1 / l1 - lam * (acc2 / l2)
    ms = jnp.mean(o * o, axis=-1, keepdims=True)
    return o * lax.rsqrt(ms + RMS_EPS) * g_out * out_scale


def _diff_kernel(it_ref, jt_ref, lam_ref, q_ref, k_ref, v_ref, bias_ref, go_ref, o_ref, qm_sc, m_sc, l_sc, acc_sc, *,
                 out_scale):
    i = it_ref[pl.program_id(1)]
    j = jt_ref[pl.program_id(1)]

    @pl.when(j == 0)
    def _():
        m_sc[...] = jnp.full_like(m_sc, -jnp.inf)
        l_sc[...] = jnp.zeros_like(l_sc)
        acc_sc[...] = jnp.zeros_like(acc_sc)
        first_half = _half_mask(LANES)
        for idx in range(2 * N_DIFF_HEADS):
            q2 = q_ref[0, :, (idx // 2) * LANES:(idx // 2 + 1) * LANES]
            half = first_half if idx % 2 == 0 else jnp.logical_not(first_half)
            qm_sc[idx] = jnp.where(half, q2, jnp.zeros_like(q2))

    def step(bias_idx):
        for h in range(N_DIFF_HEADS):
            cols = slice(h * LANES, (h + 1) * LANES)
            k2 = k_ref[0, :, cols].astype(BF16)
            v2 = v_ref[0, :, cols].astype(BF16)
            for c in range(2):
                idx = 2 * h + c
                s = lax.dot_general(qm_sc[idx], k2, _NT, preferred_element_type=F32)
                if bias_idx is not None:
                    s = s + bias_ref[bias_idx, h]
                p, alpha = _online_softmax(s, m_sc, l_sc, idx)
                acc_sc[idx] = alpha * acc_sc[idx] + jnp.dot(p.astype(BF16), v2, preferred_element_type=F32)

    @pl.when(j < i - 1)
    def _():
        step(None)

    @pl.when(j == i - 1)
    def _():
        step(1)

    @pl.when(j == i)
    def _():
        step(0)
        lam = lam_ref[0]
        for h in range(N_DIFF_HEADS):
            o = _diff_finalize(acc_sc[2 * h], l_sc[2 * h], acc_sc[2 * h + 1], l_sc[2 * h + 1],
                               lam, go_ref[...], out_scale)
            o_ref[0, :, h * LANES:(h + 1) * LANES] = o.astype(o_ref.dtype)


def _t5_bucket_1d(n):
    max_exact = N_BUCKETS // 2
    nf = jnp.maximum(n, 1).astype(F32)
    large = max_exact + (jnp.log(nf / max_exact) / math.log(MAX_DISTANCE / max_exact)
                         * (N_BUCKETS - max_exact)).astype(jnp.int32)
    large = jnp.minimum(large, N_BUCKETS - 1)
    return jnp.where(n < max_exact, n, large)


def _rel_bias_lookup(rel_table, dist):
    rel = (rel_table.astype(F32) - rel_table[N_BUCKETS - 1].astype(F32)[None, :]) * LOG2E
    bucket = _t5_bucket_1d(jnp.maximum(dist, 0))
    hit = bucket[None, ..., None] == jnp.arange(N_BUCKETS, dtype=jnp.int32)
    return jnp.sum(jnp.where(hit, rel.T.reshape((rel.shape[1],) + (1,) * dist.ndim + (N_BUCKETS,)), 0.0), axis=-1)


def _toeplitz(vals, rows, cols, offset):
    period = rows + cols
    first = jnp.flip(vals[..., offset - cols + 1:offset + 1], axis=-1)
    last = jnp.flip(vals[..., offset + 1:offset + rows], axis=-1)
    fill = jnp.zeros(vals.shape[:-1] + (period - cols - (rows - 1),), vals.dtype)
    w = jnp.concatenate([first, fill, last], axis=-1)
    flat = jnp.tile(w, (1,) * (vals.ndim - 1) + (rows,))[..., :rows * (period - 1)]
    return flat.reshape(vals.shape[:-1] + (rows, period - 1))[..., :cols]


def _diff_attention(q, k, v, rel_table, lam, g_out, out_scale, tq):
    b, t, w = q.shape
    tq = min(tq, t)
    assert t % tq == 0 and tq >= MAX_DISTANCE and tq % LANES == 0
    nq = t // tq
    val = _rel_bias_lookup(rel_table, jnp.arange(2 * tq, dtype=jnp.int32))
    masked = jnp.full((val.shape[0], tq - 1), -jnp.inf, F32)
    bias = jnp.stack([_toeplitz(jnp.concatenate([masked, val[:, :tq]], axis=1), tq, tq, tq - 1),
                      _toeplitz(val[:, 1:], tq, tq, tq - 1)])
    itab, jtab = _causal_pairs(nq)
    qmap = lambda bi, p, it, jt: (bi, it[p], 0)
    kmap = lambda bi, p, it, jt: (bi, jt[p], 0)
    grid_spec = pltpu.PrefetchScalarGridSpec(
        num_scalar_prefetch=2,
        grid=(b, itab.shape[0]),
        in_specs=[
            pl.BlockSpec(memory_space=pltpu.SMEM),
            pl.BlockSpec((1, tq, w), qmap),
            pl.BlockSpec((1, tq, w), kmap),
            pl.BlockSpec((1, tq, w), kmap),
            pl.BlockSpec(bias.shape, lambda bi, p, it, jt: (0, 0, 0, 0)),
            pl.BlockSpec((1, LANES), lambda bi, p, it, jt: (0, 0)),
        ],
        out_specs=pl.BlockSpec((1, tq, w), qmap),
        scratch_shapes=[pltpu.VMEM((2 * N_DIFF_HEADS, tq, LANES), BF16),
                        pltpu.VMEM((2 * N_DIFF_HEADS, tq, LANES), F32),
                        pltpu.VMEM((2 * N_DIFF_HEADS, tq, LANES), F32),
                        pltpu.VMEM((2 * N_DIFF_HEADS, tq, LANES), F32)],
    )
    return pl.pallas_call(
        functools.partial(_diff_kernel, out_scale=out_scale),
        grid_spec=grid_spec,
        out_shape=jax.ShapeDtypeStruct((b, t, w), BF16),
        compiler_params=_cparams(("arbitrary", "arbitrary")),
        name="diff_attention",
    )(itab, jtab, lam, q, k, v, bias, g_out)


def _out_kernel(x_ref, of_ref, od_ref, gf_ref, gd_ref, wf_ref, wd_ref, wo_ref, g2_ref, wr_ref, br_ref,
                y_ref, h2_ref, lg_ref):
    merged = (gf_ref[...] * jnp.dot(of_ref[...], wf_ref[...], preferred_element_type=F32)
              + gd_ref[...] * jnp.dot(od_ref[...], wd_ref[...], preferred_element_type=F32))
    y = x_ref[...] + jnp.dot(merged.astype(BF16), wo_ref[...], preferred_element_type=F32)
    y_ref[...] = y
    ms = jnp.mean(y * y, axis=-1, keepdims=True)
    h2 = y * lax.rsqrt(ms + RMS_EPS) * g2_ref[...]
    h2_ref[...] = h2
    lg_ref[...] = jnp.dot(h2.astype(BF16), wr_ref[...], preferred_element_type=F32) + br_ref[...]


def _out_projection(x2d, o_f, o_d, gate_f, gate_d, w_fu, w_du, w_o, g2, w_r, b_r, tm):
    n, d = x2d.shape
    tm = min(tm, n)
    assert n % tm == 0
    row = lambda i: (i, 0)
    const = lambda i: (0, 0)
    return pl.pallas_call(
        _out_kernel,
        grid=(n // tm,),
        in_specs=[
            pl.BlockSpec((tm, d), row), pl.BlockSpec((tm, o_f.shape[1]), row),
            pl.BlockSpec((tm, o_d.shape[1]), row), pl.BlockSpec((tm, d), row), pl.BlockSpec((tm, d), row),
            pl.BlockSpec(w_fu.shape, const), pl.BlockSpec(w_du.shape, const), pl.BlockSpec(w_o.shape, const),
            pl.BlockSpec((1, d), const), pl.BlockSpec(w_r.shape, const), pl.BlockSpec((1, LANES), const),
        ],
        out_specs=(pl.BlockSpec((tm, d), row), pl.BlockSpec((tm, d), row), pl.BlockSpec((tm, LANES), row)),
        out_shape=(jax.ShapeDtypeStruct((n, d), F32), jax.ShapeDtypeStruct((n, d), F32),
                   jax.ShapeDtypeStruct((n, LANES), F32)),
        compiler_params=_cparams(("arbitrary",)),
        name="out_projection",
    )(x2d, o_f, o_d, gate_f, gate_d, w_fu, w_du, w_o, g2, w_r, b_r)


def _deinterleave_kernel(w_ref, perm_ref, g_ref, u_ref):
    w = w_ref[0].astype(BF16)
    r = jnp.dot(w, perm_ref[...], preferred_element_type=F32).astype(BF16)
    half = r.shape[1] // 2
    g_ref[0] = r[:, :half]
    u_ref[0] = r[:, half:]


def _deinterleave_gate_up(w_gu, tn):
    e, d, f2 = w_gu.shape
    assert f2 % tn == 0
    half = tn // 2
    col = jnp.arange(tn)
    perm = (jnp.where(col % 2 == 0, col // 2, half + col // 2)[:, None] == col[None, :]).astype(BF16)
    out = jax.ShapeDtypeStruct((e, d, f2 // 2), BF16)
    return pl.pallas_call(
        _deinterleave_kernel,
        grid=(e, f2 // tn),
        in_specs=[pl.BlockSpec((1, d, tn), lambda i, j: (i, 0, j)),
                  pl.BlockSpec((tn, tn), lambda i, j: (0, 0))],
        out_specs=(pl.BlockSpec((1, d, half), lambda i, j: (i, 0, j)),
                   pl.BlockSpec((1, d, half), lambda i, j: (i, 0, j))),
        out_shape=(out, out),
        compiler_params=_cparams(("arbitrary", "arbitrary")),
        name="deinterleave_gate_up",
    )(w_gu, perm)


def _route_kernel(lg_ref, tril_ref, idx_ref, gate_ref, rank_ref, cnt_ref, carry_sc):
    @pl.when(pl.program_id(0) == 0)
    def _():
        carry_sc[...] = jnp.zeros_like(carry_sc)

    x = lg_ref[...]
    lane = lax.broadcasted_iota(jnp.int32, x.shape, 1)
    lane_f = lane.astype(F32)
    x = jnp.where(lane < N_EXPERTS, x, -jnp.inf)
    carry = carry_sc[0:1, :]
    vals, idxs, ranks = [], [], []
    for _ in range(TOP_K):
        m = jnp.max(x, axis=1, keepdims=True)
        idx = jnp.min(jnp.where(x == m, lane_f, float(LANES)), axis=1, keepdims=True)
        onehot = lane_f == idx
        oh = jnp.where(onehot, 1.0, 0.0)
        before = jnp.dot(tril_ref[...], oh.astype(BF16), preferred_element_type=F32)
        ranks.append(jnp.sum(jnp.where(onehot, before + carry, 0.0), axis=1, keepdims=True))
        carry = carry + jnp.sum(oh, axis=0, keepdims=True)
        x = jnp.where(onehot, -jnp.inf, x)
        vals.append(m)
        idxs.append(idx)
    carry_sc[0:1, :] = carry
    cnt_ref[...] = jnp.broadcast_to(carry, cnt_ref.shape)
    e = [jnp.exp(v - vals[0]) for v in vals]
    tot = e[0]
    for ek in e[1:]:
        tot = tot + ek
    gates = jnp.zeros(x.shape, F32)
    idx_out = jnp.zeros(x.shape, F32)
    rank_out = jnp.zeros(x.shape, F32)
    for k in range(TOP_K):
        gates = jnp.where(lane == k, e[k] / tot, gates)
        idx_out = jnp.where(lane == k, idxs[k], idx_out)
        rank_out = jnp.where(lane == k, ranks[k], rank_out)
    gate_ref[...] = gates
    idx_ref[...] = idx_out.astype(jnp.int32)
    rank_ref[...] = rank_out.astype(jnp.int32)


def _route(logits, tm):
    n = logits.shape[0]
    tm = min(tm, n)
    assert n % tm == 0
    tril = (jnp.arange(tm)[None, :] < jnp.arange(tm)[:, None]).astype(BF16)
    row = lambda i: (i, 0)
    idx, gates, rank, cnt = pl.pallas_call(
        _route_kernel,
        grid=(n // tm,),
        in_specs=[pl.BlockSpec((tm, LANES), row), pl.BlockSpec((tm, tm), lambda i: (0, 0))],
        out_specs=(pl.BlockSpec((tm, LANES), row), pl.BlockSpec((tm, LANES), row),
                   pl.BlockSpec((tm, LANES), row), pl.BlockSpec((8, LANES), lambda i: (0, 0))),
        out_shape=(jax.ShapeDtypeStruct((n, LANES), jnp.int32), jax.ShapeDtypeStruct((n, LANES), F32),
                   jax.ShapeDtypeStruct((n, LANES), jnp.int32), jax.ShapeDtypeStruct((8, LANES), F32)),
        scratch_shapes=[pltpu.VMEM((8, LANES), F32)],
        compiler_params=_cparams(("arbitrary",)),
        name="moe_route",
    )(logits, tril)
    return (idx[:, :TOP_K], gates[:, :TOP_K], rank[:, :TOP_K],
            cnt[0, :N_EXPERTS].astype(jnp.int32))


def _dispatch_kernel(slot_ref, pad_ref, h_ref, xs_out, sem):
    rows = h_ref.shape[0]
    n_pad = pad_ref.shape[0]

    @pl.when(pl.program_id(0) == 0)
    def _():
        def pad_body(r, carry):
            pltpu.make_async_copy(h_ref.at[pl.ds(0, 1)], xs_out.at[pl.ds(pad_ref[r], 1)], sem).start()
            return carry
        lax.fori_loop(0, n_pad, pad_body, 0, unroll=8)
        for _ in range(n_pad // rows):
            pltpu.make_async_copy(h_ref, xs_out.at[pl.ds(0, rows)], sem).wait()

    def body(r, carry):
        for k in range(TOP_K):
            s = slot_ref[0, 0, r * TOP_K + k]
            pltpu.make_async_copy(h_ref.at[pl.ds(r, 1)], xs_out.at[pl.ds(s, 1)], sem).start()
        return carry

    lax.fori_loop(0, rows, body, 0, unroll=2)
    for _ in range(TOP_K):
        pltpu.make_async_copy(h_ref, xs_out.at[pl.ds(0, rows)], sem).wait()


def _dispatch(h2, slot, pad_slots, n_slots, tm):
    n, d = h2.shape
    tm = min(tm, n)
    assert n % tm == 0 and pad_slots.shape[0] % tm == 0
    slot3 = slot.reshape(n // tm, 1, tm * TOP_K)
    return pl.pallas_call(
        _dispatch_kernel,
        grid=(n // tm,),
        in_specs=[pl.BlockSpec((1, 1, tm * TOP_K), lambda i: (i, 0, 0), memory_space=pltpu.SMEM),
                  pl.BlockSpec(memory_space=pltpu.SMEM),
                  pl.BlockSpec((tm, d), lambda i: (i, 0))],
        out_specs=pl.BlockSpec(memory_space=pl.ANY),
        out_shape=jax.ShapeDtypeStruct((n_slots, d), F32),
        scratch_shapes=[pltpu.SemaphoreType.DMA(())],
        compiler_params=_cparams(("arbitrary",)),
        name="moe_dispatch",
    )(slot3, pad_slots, h2)


def _row_gather_wait(src_hbm, n_rows, dst, sem):
    pltpu.make_async_copy(src_hbm.at[pl.ds(0, n_rows)], dst, sem).wait()


def _moe_kernel(be_ref, nused_ref, x_ref, wg_ref, wu_ref, bg_ref, bu_ref, wd_ref, bd_ref, y_ref):
    i = pl.program_id(0)
    n_used = nused_ref[0]

    @pl.when(i < n_used)
    def _():
        x = x_ref[...].astype(BF16)
        gate = jnp.dot(x, wg_ref[0], preferred_element_type=F32) + bg_ref[0]
        up = jnp.dot(x, wu_ref[0], preferred_element_type=F32) + bu_ref[0]
        gate = jnp.minimum(gate, SWIGLU_LIMIT)
        up = jnp.clip(up, -SWIGLU_LIMIT, SWIGLU_LIMIT)
        glu = gate * jax.nn.sigmoid(SWIGLU_ALPHA * gate)
        act = ((up + 1.0) * glu).astype(BF16)
        y_ref[...] = jnp.dot(act, wd_ref[0], preferred_element_type=F32) + bd_ref[0]

    @pl.when(i >= n_used)
    def _():
        y_ref[...] = jnp.zeros_like(y_ref)


def _expert_ffn(xs, block_expert, n_used, w_g, w_u, b_g, b_u, w_d, b_d, bm):
    n_slots, d = xs.shape
    n_blocks = block_expert.shape[0]
    dff = w_g.shape[2]
    wmap = lambda i, be, nu: (be[i], 0, 0)
    grid_spec = pltpu.PrefetchScalarGridSpec(
        num_scalar_prefetch=2,
        grid=(n_blocks,),
        in_specs=[
            pl.BlockSpec((bm, d), lambda i, be, nu: (i, 0)),
            pl.BlockSpec((1, d, dff), wmap), pl.BlockSpec((1, d, dff), wmap),
            pl.BlockSpec((1, 1, dff), wmap), pl.BlockSpec((1, 1, dff), wmap),
            pl.BlockSpec((1, dff, d), wmap), pl.BlockSpec((1, 1, d), wmap),
        ],
        out_specs=pl.BlockSpec((bm, d), lambda i, be, nu: (i, 0)),
    )
    return pl.pallas_call(
        _moe_kernel,
        grid_spec=grid_spec,
        out_shape=jax.ShapeDtypeStruct((n_slots, d), F32),
        compiler_params=_cparams(("arbitrary",)),
        name="expert_ffn",
    )(block_expert, n_used, xs, w_g, w_u, b_g, b_u, w_d, b_d)


def _combine_kernel(idx_ref, idx_next_ref, yb_hbm, resid_ref, gates_ref, o_ref, buf, sem):
    i = pl.program_id(0)
    n = pl.num_programs(0)
    tc = resid_ref.shape[0]
    slot = lax.rem(i, 2)

    def start(idx, par):
        def body(r, carry):
            pltpu.make_async_copy(yb_hbm.at[pl.ds(idx[0, 0, r], 1)], buf.at[par, pl.ds(r, 1)],
                                  sem.at[par]).start()
            return carry
        lax.fori_loop(0, TOP_K * tc, body, 0, unroll=8)

    @pl.when(i == 0)
    def _():
        start(idx_ref, 0)

    @pl.when(i + 1 < n)
    def _():
        start(idx_next_ref, 1 - slot)

    _row_gather_wait(yb_hbm, TOP_K * tc, buf.at[slot], sem.at[slot])
    acc = jnp.zeros(o_ref.shape, F32)
    for k in range(TOP_K):
        acc = acc + buf[slot, k * tc:(k + 1) * tc, :] * gates_ref[:, k:k + 1]
    o_ref[...] = resid_ref[...] + acc


def _combine(yb, slot_tm, gates, resid, tc):
    n, d = resid.shape
    n_tiles = n // tc
    rows = TOP_K * tc
    return pl.pallas_call(
        _combine_kernel,
        grid=(n_tiles,),
        in_specs=[
            pl.BlockSpec((1, 1, rows), lambda i: (i, 0, 0), memory_space=pltpu.SMEM),
            pl.BlockSpec((1, 1, rows), lambda i: (jnp.minimum(i + 1, n_tiles - 1), 0, 0),
                         memory_space=pltpu.SMEM),
            pl.BlockSpec(memory_space=pl.ANY),
            pl.BlockSpec((tc, d), lambda i: (i, 0)),
            pl.BlockSpec((tc, TOP_K), lambda i: (i, 0)),
        ],
        out_specs=pl.BlockSpec((tc, d), lambda i: (i, 0)),
        out_shape=jax.ShapeDtypeStruct((n, d), F32),
        scratch_shapes=[pltpu.VMEM((2, rows, d), F32), pltpu.SemaphoreType.DMA((2,))],
        compiler_params=_cparams(("arbitrary",)),
        name="moe_combine",
    )(slot_tm, slot_tm, yb, resid, gates)


def _moe_layer(y1, h2, logits, moe_w, bm, tc):
    n, d = y1.shape
    w_g, w_u, b_g, b_u, w_d, b_d = moe_w
    bm = min(bm, n)
    tc = min(tc, n)
    idx, gates, rank, counts = _route(logits, ROUTE_ROWS)
    padded = (counts + bm - 1) // bm * bm
    pend = jnp.cumsum(padded)
    pstart = pend - padded
    experts = jnp.arange(N_EXPERTS, dtype=jnp.int32)
    slot = rank + jnp.sum(jnp.where(idx[..., None] == experts, pstart, 0), axis=-1).astype(jnp.int32)
    n_blocks = -(-(n * TOP_K) // bm) + N_EXPERTS
    block_start = jnp.arange(n_blocks, dtype=jnp.int32) * bm
    block_expert = jnp.minimum(jnp.sum(pend[None, :] <= block_start[:, None], axis=1),
                               N_EXPERTS - 1).astype(jnp.int32)
    n_used = (pend[-1] // bm).astype(jnp.int32).reshape(1)
    assert (n * TOP_K) % bm == 0
    r = jnp.arange(bm, dtype=jnp.int32)
    is_pad = r[None, :] < (padded - counts)[:, None]
    tail_rank = (jnp.cumsum(jnp.logical_not(is_pad).reshape(-1)) - 1).reshape(N_EXPERTS, bm)
    pad_slots = jnp.where(is_pad, (pstart + counts)[:, None] + r[None, :], pend[-1] + tail_rank)
    xs = _dispatch(h2, slot, pad_slots.reshape(-1).astype(jnp.int32), n_blocks * bm, DISPATCH_ROWS)
    yb = _expert_ffn(xs, block_expert, n_used, w_g, w_u, b_g, b_u, w_d, b_d, bm)
    slot_km = slot.reshape(n // tc, tc, TOP_K).transpose(0, 2, 1).reshape(n // tc, 1, TOP_K * tc)
    return _combine(yb, slot_km, gates, y1, tc)


def _sample_attn_kernel(pt_ref, lam_ref, qsf_ref, qsd_ref, cncol_ref, cnrow_ref, snew_ref, blast_ref,
                        knf_ref, vnf_ref, knd_ref, vnd_ref, tri_ref, go_ref, *rest, pps, out_scale):
    page_refs = rest[:5 * pps]
    of_ref, od_ref, m_sc, l_sc, accf_sc, accd_sc, carry_sc = rest[5 * pps:]
    kf_refs = page_refs[0 * pps:1 * pps]
    vf_refs = page_refs[1 * pps:2 * pps]
    kd_refs = page_refs[2 * pps:3 * pps]
    vd_refs = page_refs[3 * pps:4 * pps]
    lft_refs = page_refs[4 * pps:5 * pps]
    g = pl.program_id(1)
    n_g = pl.num_programs(1)
    nh = N_FOX_HEADS
    nd_heads = N_DIFF_HEADS
    half_rows = cncol_ref.shape[1] // 2
    tok = half_rows // nh

    @pl.when(g == 0)
    def _():
        m_sc[...] = jnp.full_like(m_sc, -jnp.inf)
        l_sc[...] = jnp.zeros_like(l_sc)
        accf_sc[...] = jnp.zeros_like(accf_sc)
        accd_sc[...] = jnp.zeros_like(accd_sc)
        carry_sc[...] = jnp.zeros_like(carry_sc)

    _NN = (((1,), (0,)), ((), ()))

    def scores(kf_heads, kd_heads, fox_dims):
        rows = [lax.dot_general(qsf_ref[0, h], kf_heads[h], fox_dims, preferred_element_type=F32)
                for h in range(nh)]
        rows += [lax.dot_general(qsd_ref[0, h], kd_heads[h], _NT, preferred_element_type=F32)
                 for h in range(nd_heads)]
        return jnp.concatenate(rows, axis=0)

    def update(s, vf_heads, vd_heads, fox_dims):
        p, alpha = _online_softmax(s, m_sc, l_sc, slice(None))
        pv_f = jnp.concatenate(
            [lax.dot_general(p[h * tok:(h + 1) * tok].astype(BF16), vf_heads[h], fox_dims,
                             preferred_element_type=F32) for h in range(nh)], axis=0)
        pv_d = jnp.concatenate(
            [jnp.dot(p[half_rows + 2 * h * tok:half_rows + 2 * (h + 1) * tok].astype(BF16), vd_heads[h],
                     preferred_element_type=F32) for h in range(nd_heads)], axis=0)
        accf_sc[...] = alpha[:half_rows, :HEAD_DIM] * accf_sc[...] + pv_f
        accd_sc[...] = alpha[half_rows:] * accd_sc[...] + pv_d

    def fox_page_heads(refs):
        return [jnp.concatenate([ref[0, 0, h] for ref in refs], axis=1).astype(BF16) for h in range(nh)]

    def diff_page_heads(refs):
        page = refs[0].shape[2] // nd_heads
        return [jnp.concatenate([ref[0, 0, pl.ds(h, page, stride=nd_heads), :] for ref in refs],
                                axis=0).astype(BF16) for h in range(nd_heads)]

    def past_chunk(with_bias):
        carry = carry_sc[...]
        suf = [None] * pps
        for p in reversed(range(pps)):
            x = lft_refs[p][0]
            x3 = jnp.concatenate(_split3(x), axis=0)
            y3 = jnp.dot(x3, tri_ref[...], preferred_element_type=F32)
            within = y3[0:nh] + y3[nh:2 * nh] + y3[2 * nh:3 * nh]
            suf[p] = within + carry
            carry = carry + (within[:, 0:1] + x[:, 0:1])
        carry_sc[...] = carry
        sufc = jnp.concatenate(suf, axis=1) * LOG2E
        r = sufc.shape[1]
        suf_rows = jnp.concatenate(
            [jnp.broadcast_to(sufc[h:h + 1, :], (tok, r)) for h in range(nh)]
            + [jnp.zeros((half_rows, r), F32)], axis=0)
        s = scores(fox_page_heads(kf_refs), diff_page_heads(kd_refs), _NN)
        s = s + _lane_tile(cncol_ref[0], r // LANES) + suf_rows
        if with_bias:
            s = s + blast_ref[...]
        update(s, fox_page_heads(vf_refs), diff_page_heads(vd_refs), _NT)

    @pl.when(g == 0)
    def _():
        past_chunk(True)

    @pl.when(g > 0)
    def _():
        past_chunk(False)

    @pl.when(g == n_g - 1)
    def _():
        def new_heads(ref, n_heads):
            width = ref.shape[2] // n_heads
            return [ref[0, :, h * width:(h + 1) * width].astype(BF16) for h in range(n_heads)]

        s = scores(new_heads(knf_ref, nh), new_heads(knd_ref, nd_heads), _NT)
        s = s + (cncol_ref[0] - cnrow_ref[0]) + snew_ref[...]
        update(s, new_heads(vnf_ref, nh), new_heads(vnd_ref, nd_heads), _NN)

        l = l_sc[...]
        nf = accf_sc[...] / l[:half_rows, :HEAD_DIM]
        for h in range(nh):
            of_ref[0, :, h * HEAD_DIM:(h + 1) * HEAD_DIM] = nf[h * tok:(h + 1) * tok].astype(of_ref.dtype)

        nd = accd_sc[...] / l[half_rows:]
        lam = lam_ref[0]
        for h in range(N_DIFF_HEADS):
            cols = slice(h * LANES, (h + 1) * LANES)
            a1 = nd[(2 * h) * tok:(2 * h + 1) * tok]
            a2 = nd[(2 * h + 1) * tok:(2 * h + 2) * tok]
            o = a1 - lam * a2
            ms = jnp.mean(o * o, axis=-1, keepdims=True)
            od_ref[0, :, cols] = (o * lax.rsqrt(ms + RMS_EPS) * go_ref[...] * out_scale).astype(od_ref.dtype)


def _sample_attention(qf, qd, kf_new, vf_new, kd_new, vd_new, c_new, caches, page_table, rel_table, lam,
                      g_out, out_scale, pps, layer):
    db, s_new, w = qf.shape
    ck_f, cv_f, ck_d, cv_d, clf_t = caches
    page = ck_f.shape[2]
    ck_f, cv_f = (jnp.transpose(a, (0, 1, 3, 4, 2)) for a in (ck_f, cv_f))
    ck_d, cv_d = (a.reshape(a.shape[0], a.shape[1], page * N_DIFF_HEADS, 2 * HEAD_DIM) for a in (ck_d, cv_d))
    n_pages = page_table.shape[1]
    pps = min(pps, n_pages)
    assert n_pages % pps == 0 and page == LANES and s_new * N_FOX_HEADS * 2 == LANES
    n_g = n_pages // pps
    r = pps * page
    nh = N_FOX_HEADS
    half_rows = nh * s_new

    qsf = jnp.transpose(qf.reshape(db, s_new, nh, HEAD_DIM), (0, 2, 1, 3))
    qd_h = jnp.transpose(qd.reshape(db, s_new, N_DIFF_HEADS, 2 * HEAD_DIM), (0, 2, 1, 3))
    first = (jnp.arange(2 * HEAD_DIM) < HEAD_DIM)
    qsd = jnp.concatenate([jnp.where(first, qd_h, jnp.zeros((), qd.dtype)),
                           jnp.where(first, jnp.zeros((), qd.dtype), qd_h)], axis=2)

    c_new = c_new * LOG2E
    c_rows =jnp.transpose(c_new, (0, 2, 1)).reshape(db, half_rows)
    cn_col = jnp.concatenate([c_rows, jnp.zeros_like(c_rows)], axis=1)[:, :, None]
    cn_col = jnp.broadcast_to(cn_col, (db, LANES, LANES))
    c_keys = jnp.repeat(jnp.transpose(c_new, (0, 2, 1)), s_new, axis=1)
    cn_row = jnp.zeros((db, LANES, LANES), F32).at[:, :half_rows, :s_new].set(c_keys)

    t_of_row = jnp.arange(LANES, dtype=jnp.int32) % s_new
    key = jnp.arange(LANES, dtype=jnp.int32)
    valid_new = (key[None, :] <= t_of_row[:, None]) & (key[None, :] < s_new)
    is_diff = (jnp.arange(LANES) >= half_rows)
    head_of_row = jnp.clip((jnp.arange(LANES) - half_rows) // (2 * s_new), 0, N_DIFF_HEADS - 1)

    def rel_rows(dist):
        per_head = _rel_bias_lookup(rel_table, dist)
        vals = jnp.zeros(dist.shape, F32)
        for h in range(N_DIFF_HEADS):
            vals = jnp.where(head_of_row[:, None] == h, per_head[h], vals)
        return jnp.where(is_diff[:, None], vals, 0.0)

    s_new_tile = jnp.where(valid_new, rel_rows(t_of_row[:, None] - key[None, :]), -jnp.inf)
    val = _rel_bias_lookup(rel_table, jnp.arange(r + s_new, dtype=jnp.int32))
    per_head = _toeplitz(val, s_new, r, r)
    b_last = jnp.concatenate(
        [jnp.zeros((half_rows, r), F32),
         jnp.broadcast_to(per_head[:, None], (N_DIFF_HEADS, 2, s_new, r)).reshape(half_rows, r)], axis=0)

    def pad_rows(x):
        return jnp.pad(x, ((0, 0), (0, LANES - s_new), (0, 0)))

    tri = (jnp.arange(page)[:, None] > jnp.arange(page)[None, :]).astype(BF16)

    bmap = lambda b, g, pt: (b, 0, 0)
    bmap4 = lambda b, g, pt: (b, 0, 0, 0)
    const2 = lambda b, g, pt: (0, 0)

    def page_map(p):
        return lambda b, g, pt: (pt[b, (n_g - 1 - g) * pps + p], 0, 0)

    def cache_map(p, ndim):
        return lambda b, g, pt: (layer, pt[b, (n_g - 1 - g) * pps + p]) + (0,) * (ndim - 2)

    in_specs = [
        pl.BlockSpec(memory_space=pltpu.SMEM),
        pl.BlockSpec((1,) + qsf.shape[1:], bmap4), pl.BlockSpec((1,) + qsd.shape[1:], bmap4),
        pl.BlockSpec((1, LANES, LANES), bmap), pl.BlockSpec((1, LANES, LANES), bmap),
        pl.BlockSpec((LANES, LANES), const2), pl.BlockSpec((LANES, r), const2),
        pl.BlockSpec((1, LANES, w), bmap), pl.BlockSpec((1, LANES, w), bmap),
        pl.BlockSpec((1, LANES, w), bmap), pl.BlockSpec((1, LANES, w), bmap),
        pl.BlockSpec((page, page), const2), pl.BlockSpec((1, LANES), const2),
    ]
    operands = [lam, qsf, qsd, cn_col, cn_row, s_new_tile, b_last,
                pad_rows(kf_new), pad_rows(vf_new), pad_rows(kd_new), pad_rows(vd_new), tri, g_out]
    for arr in (ck_f, cv_f, ck_d, cv_d):
        for p in range(pps):
            in_specs.append(pl.BlockSpec((1, 1) + arr.shape[2:], cache_map(p, arr.ndim)))
            operands.append(arr)
    for p in range(pps):
        in_specs.append(pl.BlockSpec((1, nh, page), page_map(p)))
        operands.append(clf_t)

    grid_spec = pltpu.PrefetchScalarGridSpec(
        num_scalar_prefetch=1,
        grid=(db, n_g),
        in_specs=in_specs,
        out_specs=(pl.BlockSpec((1, s_new, w), bmap), pl.BlockSpec((1, s_new, w), bmap)),
        scratch_shapes=[pltpu.VMEM((LANES, LANES), F32), pltpu.VMEM((LANES, LANES), F32),
                        pltpu.VMEM((half_rows, HEAD_DIM), F32), pltpu.VMEM((half_rows, 2 * HEAD_DIM), F32),
                        pltpu.VMEM((nh, LANES), F32)],
    )
    return pl.pallas_call(
        functools.partial(_sample_attn_kernel, pps=pps, out_scale=out_scale),
        grid_spec=grid_spec,
        out_shape=(jax.ShapeDtypeStruct((db, s_new, w), BF16), jax.ShapeDtypeStruct((db, s_new, w), BF16)),
        compiler_params=_cparams(("arbitrary", "arbitrary")),
        name="sample_attention",
    )(page_table, *operands)


PROJ_ROWS = 256
ATTN_BLOCK = 512
SCAN_BLOCK = 512
MOE_ROWS = 512
COMBINE_ROWS = 256
ROUTE_ROWS = 512
DISPATCH_ROWS = 512
DEINTERLEAVE_COLS = 512
PAGES_PER_STEP = 8


def _split_points(d_model):
    sizes = (FOX_WIDTH, FOX_WIDTH, FOX_WIDTH, N_FOX_HEADS, DIFF_WIDTH, DIFF_WIDTH, DIFF_WIDTH, d_model)
    pts, acc = [], 0
    for s in sizes:
        acc += s
        pts.append(acc)
    return pts


def _layer_weights(l, d_model, attn_norm_g, w_in, b_forget, fox_q_norm_g, fox_k_norm_g, diff_q_norm_g,
                   diff_k_norm_g, lambda_q1, lambda_k1, lambda_q2, lambda_k2, diff_out_norm_g, w_fox_up,
                   w_diff_up, w_out, ffn_norm_g, w_router, b_router, w_gate_up, b_gate_up, w_down, b_down):
    lambda_init = 0.8 - 0.6 * math.exp(-0.3 * l)
    wq_f, wk_f, wv_f, w_fl, wq_d, wk_d, wv_d, wg_f, wg_d = jnp.split(w_in[l], _split_points(d_model), axis=1)
    w_main = jnp.concatenate([wq_f, wk_f, wv_f, wq_d, wk_d, wv_d, wg_f, wg_d], axis=1).astype(BF16)
    w_fl = jnp.pad(w_fl, ((0, 0), (0, LANES - N_FOX_HEADS))).astype(BF16)
    b_fl = jnp.pad(b_forget[l].astype(F32), (0, LANES - N_FOX_HEADS)).reshape(1, LANES)
    grp = jnp.arange(FOX_WIDTH) // HEAD_DIM
    ones_bd = (grp[:, None] == grp[None, :]).astype(BF16)
    reps = FOX_WIDTH // HEAD_DIM
    gains = jnp.stack([jnp.tile(g[l].astype(F32), reps)
                       for g in (fox_q_norm_g, fox_k_norm_g, diff_q_norm_g, diff_k_norm_g)])
    lam = (jnp.exp(jnp.sum(lambda_q1[l].astype(F32) * lambda_k1[l].astype(F32)))
           - jnp.exp(jnp.sum(lambda_q2[l].astype(F32) * lambda_k2[l].astype(F32))) + lambda_init)
    w_r = jnp.pad(w_router[l], ((0, 0), (0, LANES - N_EXPERTS))).astype(BF16)
    b_r = jnp.pad(b_router[l].astype(F32), (0, LANES - N_EXPERTS)).reshape(1, LANES)
    w_g, w_u = _deinterleave_gate_up(w_gate_up[l], DEINTERLEAVE_COLS)
    bgu = b_gate_up[l].astype(F32)
    moe_w = (w_g, w_u, bgu[:, None, 0::2], bgu[:, None, 1::2],
             w_down[l].astype(BF16), b_down[l].astype(F32)[:, None, :])
    return dict(
        lambda_init=lambda_init, g1=attn_norm_g[l].astype(F32).reshape(1, d_model), w_main=w_main, w_fl=w_fl,
        b_fl=b_fl, ones_bd=ones_bd, gains=gains, lam=lam.reshape(1).astype(F32),
        g_out=diff_out_norm_g[l].astype(F32).reshape(1, LANES),
        w_fu=w_fox_up[l].astype(BF16), w_du=w_diff_up[l].astype(BF16), w_o=w_out[l].astype(BF16),
        g2=ffn_norm_g[l].astype(F32).reshape(1, d_model), w_r=w_r, b_r=b_r, moe_w=moe_w)


def _project(x, lw):
    b, t, d = x.shape
    outs = _in_projection(x.reshape(b * t, d), lw['g1'], lw['w_main'], lw['w_fl'], lw['b_fl'],
                          lw['ones_bd'], lw['gains'], PROJ_ROWS, t)
    qf, kf, vf, lf, qd, kd, vd, gf, gd = outs[:9]
    logf = lf[:, :N_FOX_HEADS].reshape(b, t, N_FOX_HEADS)
    r3 = lambda a: a.reshape(b, t, a.shape[-1])
    if len(outs) > 9:
        rows4 = lambda a: jnp.transpose(a.reshape(b, N_FOX_HEADS, HEAD_DIM, t), (0, 3, 1, 2))
        fox_rows = (rows4(outs[9]), rows4(outs[10]))
    else:
        fox_rows = (kf.reshape(b, t, N_FOX_HEADS, HEAD_DIM), vf.reshape(b, t, N_FOX_HEADS, HEAD_DIM))
    return r3(qf), r3(kf), r3(vf), logf, r3(qd), r3(kd), r3(vd), gf, gd, fox_rows


def _finish_layer(x, o_f, o_d, gf, gd, lw, moe_rows):
    b, t, d = x.shape
    n = b * t
    y1, h2, logits = _out_projection(x.reshape(n, d), o_f.reshape(n, -1), o_d.reshape(n, -1), gf, gd,
                                     lw['w_fu'], lw['w_du'], lw['w_o'], lw['g2'], lw['w_r'], lw['b_r'],
                                     PROJ_ROWS)
    return _moe_layer(y1, h2, logits, lw['moe_w'], moe_rows, COMBINE_ROWS).reshape(b, t, d)


def _new_rows(fox_rows, logf, kd, vd):
    b, t, _ = kd.shape
    return (fox_rows[0], fox_rows[1], logf,
            kd.reshape(b, t, N_DIFF_HEADS, 2 * HEAD_DIM), vd.reshape(b, t, N_DIFF_HEADS, 2 * HEAD_DIM))


def _prompt_layer(x, lw, rel_table):
    qf, kf, vf, logf, qd, kd, vd, gf, gd, fox_rows = _project(x, lw)
    ct = _prefix_sum_time(jnp.transpose(logf, (0, 2, 1)), SCAN_BLOCK)
    c = jnp.transpose(ct, (0, 2, 1))
    o_f = _fox_attention(qf, kf, vf, c, ATTN_BLOCK)
    o_d = _diff_attention(qd, kd, vd, rel_table, lw['lam'], lw['g_out'], 1.0 - lw['lambda_init'], ATTN_BLOCK)
    return _finish_layer(x, o_f, o_d, gf, gd, lw, MOE_ROWS), _new_rows(fox_rows, logf, kd, vd)


def _sample_layer(x, lw, rel_table, caches, page_table, layer):
    qf, kf, vf, logf, qd, kd, vd, gf, gd, fox_rows = _project(x, lw)
    b, t, _ = logf.shape
    logf_t = jnp.pad(jnp.transpose(logf, (0, 2, 1)), ((0, 0), (0, 0), (0, LANES - t)))
    c = jnp.transpose(_prefix_sum_time(logf_t, LANES)[:, :, :t], (0, 2, 1))
    o_f, o_d = _sample_attention(qf, qd, kf, vf, kd, vd, c, caches, page_table, rel_table, lw['lam'],
                                 lw['g_out'], 1.0 - lw['lambda_init'], PAGES_PER_STEP, layer)
    return _finish_layer(x, o_f, o_d, gf, gd, lw, LANES), _new_rows(fox_rows, logf, kd, vd)


def kernel(x_prompt, x_sample, cache_fox_k, cache_fox_v, cache_fox_logf, cache_diff_k, cache_diff_v, page_table, rel_bias_table, attn_norm_g, w_in, b_forget, fox_q_norm_g, fox_k_norm_g, diff_q_norm_g, diff_k_norm_g, lambda_q1, lambda_k1, lambda_q2, lambda_k2, diff_out_norm_g, w_fox_up, w_diff_up, w_out, ffn_norm_g, w_router, b_router, w_gate_up, b_gate_up, w_down, b_down):
    depth = w_in.shape[0]
    d_model = x_prompt.shape[-1]
    yp, ys = x_prompt, x_sample
    rows_p, rows_s = [], []
    for l in range(depth):
        lw = _layer_weights(l, d_model, attn_norm_g, w_in, b_forget, fox_q_norm_g, fox_k_norm_g,
                            diff_q_norm_g, diff_k_norm_g, lambda_q1, lambda_k1, lambda_q2, lambda_k2,
                            diff_out_norm_g, w_fox_up, w_diff_up, w_out, ffn_norm_g, w_router, b_router,
                            w_gate_up, b_gate_up, w_down, b_down)
        yp, rp = _prompt_layer(yp, lw, rel_bias_table)
        caches = (cache_fox_k, cache_fox_v, cache_diff_k, cache_diff_v,
                  jnp.transpose(cache_fox_logf[l].astype(F32), (0, 2, 1)))
        ys, rs = _sample_layer(ys, lw, rel_bias_table, caches, page_table, l)
        rows_p.append(rp)
        rows_s.append(rs)
    stack = lambda rows, i: jnp.stack([r[i] for r in rows])
    return (yp, ys) + tuple(stack(rows_p, i) for i in range(5)) + tuple(stack(rows_s, i) for i in range(5))
```
